```python
import math
import jax
import jax.numpy as jnp
from jax import lax
import numpy as np

D_MODEL = 1024
BATCH = 8
SEQ = 2048
DEPTH = 2
DEC_BATCH = 128
DEC_SEQ = 1
PAST_LEN = 8192
PAGE_SIZE = 128

N_HEADS = 8
QK_NOPE = 64
QK_ROPE = 32
QK_DIM = QK_NOPE + QK_ROPE
V_DIM = 64
Q_LORA = 384
KV_LORA = 256
ROPE_THETA = 10000.0
Q_BLOCK = 128
CONV_W = 3
CONV_DIM = 512
CHUNK = 128
GM_GROUPS = 8
GM_DIM = 512
GM_GROUP_DIM = GM_DIM // GM_GROUPS
N_EXPERTS = 64
TOP_K = 8
N_EXPERT_GROUPS = 8
TOPK_GROUPS = 4
D_EXPERT = 256
D_SHARED = 256
ROUTED_SCALE = 2.5
TOKEN_BLOCK = 128
N_BRANCH = 3
EPS = 1e-6

OFF_Q = 0
OFF_KV = OFF_Q + Q_LORA
OFF_KR = OFF_KV + KV_LORA
OFF_CB = OFF_KR + QK_ROPE
OFF_CC = OFF_CB + CONV_DIM
OFF_CH = OFF_CC + CONV_DIM
OFF_U = OFF_CH + CONV_DIM
OFF_V = OFF_U + GM_DIM
OFF_G = OFF_V + GM_DIM
IN_DIM = OFF_G + N_BRANCH * D_MODEL

kernel_name = "hybrid_mla_conv_gmlp_moe_adaln_step"


def rmsnorm(x, g):
    xf = x.astype(jnp.float32)
    inv = lax.rsqrt(jnp.mean(xf * xf, axis=-1, keepdims=True) + EPS)
    return (xf * inv).astype(x.dtype) * g


def layernorm(x, g):
    xf = x.astype(jnp.float32)
    xc = xf - jnp.mean(xf, axis=-1, keepdims=True)
    inv = lax.rsqrt(jnp.mean(xc * xc, axis=-1, keepdims=True) + EPS)
    return (xc * inv).astype(x.dtype) * g


def rope_cos_sin(pos):
    inv_freq = ROPE_THETA ** (-jnp.arange(0, QK_ROPE, 2, dtype=jnp.float32) / QK_ROPE)
    ang = pos.astype(jnp.float32)[:, None] * inv_freq[None, :]
    return jnp.cos(ang), jnp.sin(ang)


def apply_rope(x, cos, sin):
    x1, x2 = jnp.split(x.astype(jnp.float32), 2, axis=-1)
    return jnp.concatenate([x1 * cos - x2 * sin, x1 * sin + x2 * cos], axis=-1).astype(x.dtype)


def adaln_modulation(c, w_ada, b_ada):
    m = jnp.einsum('nd,de->ne', jax.nn.silu(c), w_ada) + b_ada
    return jnp.split(m[:, None, :], 6, axis=-1)


def mla_queries(cq_raw, pos, g_q_lat, w_uq, g_qk_q):
    q = jnp.einsum('ntc,chd->nthd', rmsnorm(cq_raw, g_q_lat), w_uq)
    cos, sin = rope_cos_sin(pos)
    q = jnp.concatenate([q[..., :QK_NOPE], apply_rope(q[..., QK_NOPE:], cos[:, None, :], sin[:, None, :])], axis=-1)
    return rmsnorm(q, g_qk_q) * (QK_DIM ** -0.5)


def mla_keys(c_kv, k_rope, w_uk, g_qk_k):
    k_nope = jnp.einsum('...tc,chd->...thd', c_kv, w_uk)
    k_rot = jnp.broadcast_to(k_rope[..., None, :], k_nope.shape[:-1] + (QK_ROPE,))
    return rmsnorm(jnp.concatenate([k_nope, k_rot], axis=-1), g_qk_k)


def attend(q, k, c_kv, w_uv, q_pos, k_pos):
    s = jnp.einsum('...qhd,...khd->...hqk', q, k).astype(jnp.float32)
    s = jnp.where(k_pos[None, :] <= q_pos[:, None], s, -jnp.inf)
    prob = jax.nn.softmax(s, axis=-1).astype(c_kv.dtype)
    o_lat = jnp.einsum('...hqk,...kc->...qhc', prob, c_kv)
    return jnp.einsum('...qhc,chd->...qhd', o_lat, w_uv)


def mla_prompt(q, c_kv, k_rope, pos, w_uk, w_uv, g_qk_k):
    n, t = q.shape[:2]
    nb = t // Q_BLOCK
    k = mla_keys(c_kv, k_rope, w_uk, g_qk_k)
    q_blocks = q.reshape(n, nb, Q_BLOCK, N_HEADS, QK_DIM).swapaxes(0, 1)
    pos_blocks = pos.reshape(nb, Q_BLOCK)

    def block(args):
        q_b, pos_b = args
        return attend(q_b, k, c_kv, w_uv, pos_b, pos)

    out = lax.map(block, (q_blocks, pos_blocks))
    return out.swapaxes(0, 1).reshape(n, t, N_HEADS * V_DIM)


def mla_sample(q, c_new, kr_new, pos, pool_c, pool_kr, page_table, w_uk, w_uv, g_qk_k):
    n, t = q.shape[:2]
    k_pos = jnp.concatenate([jnp.arange(PAST_LEN), pos])

    def one(args):
        q_i, pages_i, c_i, kr_i = args
        c_all = jnp.concatenate([pool_c[pages_i].reshape(PAST_LEN, KV_LORA), c_i], axis=0)
        kr_all = jnp.concatenate([pool_kr[pages_i].reshape(PAST_LEN, QK_ROPE), kr_i], axis=0)
        k = mla_keys(c_all, kr_all, w_uk, g_qk_k)
        return attend(q_i, k, c_all, w_uv, pos, k_pos)

    out = lax.map(one, (q, page_table, c_new, kr_new))
    return out.reshape(n, t, N_HEADS * V_DIM)


def short_conv(z_full, w_conv):
    return lax.conv_general_dilated(z_full, w_conv[:, None, :], (1,), 'VALID',
                                    dimension_numbers=('NWC', 'WIO', 'NWC'),
                                    feature_group_count=CONV_DIM)


def chunk_spatial_gate(u, v_n, w_s, b_s):
    n, nc, tc, _ = u.shape
    w = jnp.where(jnp.tril(jnp.ones((tc, tc), dtype=bool)), w_s[:, :tc, :tc], 0)
    vg = v_n.reshape(n, nc, tc, GM_GROUPS, GM_GROUP_DIM)
    s = jnp.einsum('gts,ncsgd->nctgd', w, vg) + b_s[:, :tc].T[None, None, :, :, None]
    return u * s.reshape(n, nc, tc, GM_DIM)


def sigmoid_topk_gates(h, w_router, b_router):
    scores = jax.nn.sigmoid(jnp.einsum('td,de->te', h, w_router).astype(jnp.float32))
    choice = scores + b_router.astype(jnp.float32)
    grp = choice.reshape(-1, N_EXPERT_GROUPS, N_EXPERTS // N_EXPERT_GROUPS)
    grp_score = jnp.sum(lax.top_k(grp, 2)[0], axis=-1)
    _, gidx = lax.top_k(grp_score, TOPK_GROUPS)
    gmask = jnp.sum(jax.nn.one_hot(gidx, N_EXPERT_GROUPS, dtype=jnp.float32), axis=-2)
    emask = jnp.repeat(gmask, N_EXPERTS // N_EXPERT_GROUPS, axis=-1)
    _, idx = lax.top_k(jnp.where(emask > 0, choice, -jnp.inf), TOP_K)
    w = jnp.take_along_axis(scores, idx, axis=-1)
    w = w / jnp.sum(w, axis=-1, keepdims=True) * ROUTED_SCALE
    return jnp.sum(jax.nn.one_hot(idx, N_EXPERTS, dtype=jnp.float32) * w[..., None], axis=-2)


def moe_ffn(h, lw):
    n, t, d = h.shape
    flat = h.reshape(n * t, d)

    def block(hb):
        gates = sigmoid_topk_gates(hb, lw['w_router'], lw['b_router']).astype(hb.dtype)
        a = jnp.einsum('td,edf->tef', hb, lw['w_e_gate'])
        b = jnp.einsum('td,edf->tef', hb, lw['w_e_up'])
        routed = jnp.einsum('tef,efd->td', jax.nn.silu(a) * b * gates[..., None], lw['w_e_down'])
        shared = (jax.nn.silu(hb @ lw['w_sh_gate']) * (hb @ lw['w_sh_up'])) @ lw['w_sh_down']
        return routed + shared

    if flat.shape[0] % TOKEN_BLOCK == 0:
        out = lax.map(block, flat.reshape(-1, TOKEN_BLOCK, d)).reshape(n * t, d)
    else:
        out = block(flat)
    return out.reshape(n, t, d)


def decoder_layer(x, c, pos, conv_hist, paged, lw):
    n, t, _ = x.shape
    shift_m, scale_m, gate_m, shift_f, scale_f, gate_f = adaln_modulation(c, lw['w_ada'], lw['b_ada'])
    h = rmsnorm(x, lw['g_norm_mix']) * (1.0 + scale_m) + shift_m
    p = jnp.einsum('ntd,de->nte', h, lw['w_in'])

    q = mla_queries(p[..., OFF_Q:OFF_KV], pos, lw['g_q_lat'], lw['w_uq'], lw['g_qk_q'])
    c_kv = rmsnorm(p[..., OFF_KV:OFF_KR], lw['g_kv_lat'])
    cos, sin = rope_cos_sin(pos)
    k_rope = apply_rope(p[..., OFF_KR:OFF_CB], cos, sin)
    if paged is None:
        o_a = mla_prompt(q, c_kv, k_rope, pos, lw['w_uk'], lw['w_uv'], lw['g_qk_k'])
    else:
        pool_c, pool_kr, page_table = paged
        o_a = mla_sample(q, c_kv, k_rope, pos, pool_c, pool_kr, page_table, lw['w_uk'], lw['w_uv'], lw['g_qk_k'])
    br_a = jnp.einsum('nte,ed->ntd', o_a, lw['w_br_a'])

    gate_b = p[..., OFF_CB:OFF_CC]
    z = p[..., OFF_CC:OFF_CH] * p[..., OFF_CH:OFF_U]
    z_full = jnp.concatenate([conv_hist, z], axis=1)
    br_b = jnp.einsum('ntc,cd->ntd', gate_b * short_conv(z_full, lw['w_conv']), lw['w_br_b'])
    conv_tail = z_full[:, z_full.shape[1] - (CONV_W - 1):]

    tc = min(t, CHUNK)
    u = p[..., OFF_U:OFF_V].reshape(n, t // tc, tc, GM_DIM)
    v_n = layernorm(p[..., OFF_V:OFF_G], lw['g_v_ln'])
    sg = chunk_spatial_gate(u, v_n.reshape(n, t // tc, tc, GM_DIM), lw['w_spatial'], lw['b_spatial'])
    br_c = jnp.einsum('nte,ed->ntd', sg.reshape(n, t, GM_DIM), lw['w_br_c'])
    chunk_v = v_n[:, t - tc:]

    bgate = jax.nn.sigmoid(p[..., OFF_G:]).reshape(n, t, N_BRANCH, D_MODEL)
    merged = bgate[:, :, 0] * br_a + bgate[:, :, 1] * br_b + bgate[:, :, 2] * br_c
    x = x + gate_m * jnp.einsum('ntd,de->nte', merged, lw['w_out'])

    h2 = rmsnorm(x, lw['g_norm_ffn']) * (1.0 + scale_f) + shift_f
    x = x + gate_f * moe_ffn(h2, lw)
    return x, (c_kv, k_rope, conv_tail, chunk_v)


def setup_inputs(seed: int = 0) -> dict:
    key = jax.random.key(seed)
    ks = list(jax.random.split(key, 40))
    f32 = jnp.float32

    def nrm(i, shape, scale):
        return jax.random.normal(ks[i], shape, f32) * scale

    def gain(i, shape):
        return 1.0 + nrm(i, shape, 0.01)

    n_pages = PAST_LEN // PAGE_SIZE
    n_used = DEC_BATCH * n_pages
    n_pool = n_used + max(1, n_used // 4)
    page_table = jax.random.permutation(ks[5], n_pool)[:n_used].reshape(DEC_BATCH, n_pages).astype(jnp.int32)
    return {
        'x_prompt': nrm(0, (BATCH, SEQ, D_MODEL), 1.0),
        'x_sample': nrm(1, (DEC_BATCH, DEC_SEQ, D_MODEL), 1.0),
        'cache_kv_latent': nrm(2, (DEPTH, n_pool, PAGE_SIZE, KV_LORA), 1.0),
        'cache_k_rope': nrm(3, (DEPTH, n_pool, PAGE_SIZE, QK_ROPE), 1.0),
        'state_conv': nrm(4, (DEPTH, DEC_BATCH, CONV_W - 1, CONV_DIM), 1.0),
        'page_table': page_table,
        'c_prompt': nrm(6, (BATCH, D_MODEL), 1.0),
        'c_sample': nrm(7, (DEC_BATCH, D_MODEL), 1.0),
        'w_ada': nrm(8, (DEPTH, D_MODEL, 6 * D_MODEL), 0.5 * D_MODEL ** -0.5),
        'b_ada': nrm(9, (DEPTH, 6 * D_MODEL), 0.01),
        'g_norm_mix': gain(10, (DEPTH, D_MODEL)),
        'g_norm_ffn': gain(11, (DEPTH, D_MODEL)),
        'w_in': nrm(12, (DEPTH, D_MODEL, IN_DIM), D_MODEL ** -0.5),
        'g_q_lat': gain(13, (DEPTH, Q_LORA)),
        'w_uq': nrm(14, (DEPTH, Q_LORA, N_HEADS, QK_DIM), Q_LORA ** -0.5),
        'g_kv_lat': gain(15, (DEPTH, KV_LORA)),
        'w_uk': nrm(16, (DEPTH, KV_LORA, N_HEADS, QK_NOPE), KV_LORA ** -0.5),
        'w_uv': nrm(17, (DEPTH, KV_LORA, N_HEADS, V_DIM), KV_LORA ** -0.5),
        'g_qk_q': gain(18, (DEPTH, QK_DIM)),
        'g_qk_k': gain(19, (DEPTH, QK_DIM)),
        'w_br_a': nrm(20, (DEPTH, N_HEADS * V_DIM, D_MODEL), (N_HEADS * V_DIM) ** -0.5),
        'w_conv': nrm(21, (DEPTH, CONV_W, CONV_DIM), CONV_W ** -0.5),
        'w_br_b': nrm(22, (DEPTH, CONV_DIM, D_MODEL), CONV_DIM ** -0.5),
        'g_v_ln': gain(23, (DEPTH, GM_DIM)),
        'w_spatial': nrm(24, (DEPTH, GM_GROUPS, CHUNK, CHUNK), CHUNK ** -0.5),
        'b_spatial': 1.0 + nrm(25, (DEPTH, GM_GROUPS, CHUNK), 0.1),
        'w_br_c': nrm(26, (DEPTH, GM_DIM, D_MODEL), GM_DIM ** -0.5),
        'w_out': nrm(27, (DEPTH, D_MODEL, D_MODEL), D_MODEL ** -0.5),
        'w_router': nrm(28, (DEPTH, D_MODEL, N_EXPERTS), D_MODEL ** -0.5),
        'b_router': nrm(29, (DEPTH, N_EXPERTS), 0.01),
        'w_e_gate': nrm(30, (DEPTH, N_EXPERTS, D_MODEL, D_EXPERT), D_MODEL ** -0.5),
        'w_e_up': nrm(31, (DEPTH, N_EXPERTS, D_MODEL, D_EXPERT), D_MODEL ** -0.5),
        'w_e_down': nrm(32, (DEPTH, N_EXPERTS, D_EXPERT, D_MODEL), D_EXPERT ** -0.5),
        'w_sh_gate': nrm(33, (DEPTH, D_MODEL, D_SHARED), D_MODEL ** -0.5),
        'w_sh_up': nrm(34, (DEPTH, D_MODEL, D_SHARED), D_MODEL ** -0.5),
        'w_sh_down': nrm(35, (DEPTH, D_SHARED, D_MODEL), D_SHARED ** -0.5),
    }


def reference(x_prompt, x_sample, cache_kv_latent, cache_k_rope, state_conv, page_table, c_prompt, c_sample,
              w_ada, b_ada, g_norm_mix, g_norm_ffn, w_in, g_q_lat, w_uq, g_kv_lat, w_uk, w_uv, g_qk_q, g_qk_k,
              w_br_a, w_conv, w_br_b, g_v_ln, w_spatial, b_spatial, w_br_c, w_out, w_router, b_router,
              w_e_gate, w_e_up, w_e_down, w_sh_gate, w_sh_up, w_sh_down):
    pos_p = jnp.arange(x_prompt.shape[1])
    pos_s = PAST_LEN + jnp.arange(x_sample.shape[1])
    conv_zero = jnp.zeros((x_prompt.shape[0], CONV_W - 1, CONV_DIM), x_prompt.dtype)
    hp, hs = x_prompt, x_sample
    st_p, st_s = [], []
    for l in range(DEPTH):
        lw = dict(w_ada=w_ada[l], b_ada=b_ada[l], g_norm_mix=g_norm_mix[l], g_norm_ffn=g_norm_ffn[l],
                  w_in=w_in[l], g_q_lat=g_q_lat[l], w_uq=w_uq[l], g_kv_lat=g_kv_lat[l], w_uk=w_uk[l],
                  w_uv=w_uv[l], g_qk_q=g_qk_q[l], g_qk_k=g_qk_k[l], w_br_a=w_br_a[l], w_conv=w_conv[l],
                  w_br_b=w_br_b[l], g_v_ln=g_v_ln[l], w_spatial=w_spatial[l], b_spatial=b_spatial[l],
                  w_br_c=w_br_c[l], w_out=w_out[l], w_router=w_router[l], b_router=b_router[l],
                  w_e_gate=w_e_gate[l], w_e_up=w_e_up[l], w_e_down=w_e_down[l],
                  w_sh_gate=w_sh_gate[l], w_sh_up=w_sh_up[l], w_sh_down=w_sh_down[l])
        hp, sp = decoder_layer(hp, c_prompt, pos_p, conv_zero, None, lw)
        hs, ss = decoder_layer(hs, c_sample, pos_s, state_conv[l],
                               (cache_kv_latent[l], cache_k_rope[l], page_table), lw)
        st_p.append(sp)
        st_s.append(ss)
    new_kv_latent_prompt = jnp.stack([s[0] for s in st_p])
    new_k_rope_prompt = jnp.stack([s[1] for s in st_p])
    new_conv_prompt = jnp.stack([s[2] for s in st_p])
    new_chunk_v_prompt = jnp.stack([s[3] for s in st_p])
    new_kv_latent_sample = jnp.stack([s[0] for s in st_s])
    new_k_rope_sample = jnp.stack([s[1] for s in st_s])
    new_conv_sample = jnp.stack([s[2] for s in st_s])
    new_chunk_v_sample = jnp.stack([s[3] for s in st_s])
    return (hp, hs, new_kv_latent_prompt, new_k_rope_prompt, new_conv_prompt, new_chunk_v_prompt,
            new_kv_latent_sample, new_k_rope_sample, new_conv_sample, new_chunk_v_sample)
```

```python
import functools

import jax
import jax.numpy as jnp
from jax import lax
from jax.experimental import pallas as pl
from jax.experimental.pallas import tpu as pltpu

F32 = jnp.float32
BF16 = jnp.bfloat16

D_MODEL = 1024
N_HEADS = 8
QK_NOPE = 64
QK_ROPE = 32
QK_DIM = QK_NOPE + QK_ROPE
V_DIM = 64
Q_LORA = 384
KV_LORA = 256
ROPE_THETA = 10000.0
CONV_W = 3
CONV_DIM = 512
CHUNK = 128
GM_GROUPS = 8
GM_DIM = 512
GM_GROUP_DIM = GM_DIM // GM_GROUPS
N_EXPERTS = 64
TOP_K = 8
N_EXPERT_GROUPS = 8
GROUP_SIZE = N_EXPERTS // N_EXPERT_GROUPS
TOPK_GROUPS = 4
D_EXPERT = 256
ROUTED_SCALE = 2.5
N_BRANCH = 3
EPS = 1e-6
PAGE_SIZE = 128

LANES = 128
HP = LANES
VMEM_LIMIT = 56 * 1024 * 1024

S_Q = 0
S_KV = S_Q + Q_LORA
S_KRM = S_KV + KV_LORA
S_KRR = S_KRM + HP
S_CB = S_KRR + HP
S_CC = S_CB + CONV_DIM
S_CH = S_CC + CONV_DIM
S_U = S_CH + CONV_DIM
S_V = S_U + GM_DIM
S_G = S_V + GM_DIM
S_END = S_G + N_BRANCH * D_MODEL

TM_IN = 256
TQ = 512
TK = 512
TM_MERGE = 512
TM_ROUTE = 512
TM_MOE = 1024
EXPERTS_PER_STEP = 2
PAGES_PER_STEP = 16


def _cparams(sem):
    return pltpu.CompilerParams(dimension_semantics=sem, vmem_limit_bytes=VMEM_LIMIT)


def _const_spec(shape):
    nd = len(shape)
    return pl.BlockSpec(shape, lambda *_: (0,) * nd, pipeline_mode=pl.Buffered(1))


def _dot(a, b):
    return jnp.dot(a, b, preferred_element_type=F32)


def _rms_rows(x):
    return x * lax.rsqrt(jnp.mean(x * x, axis=-1, keepdims=True) + EPS)


def _silu(x):
    return x * jax.nn.sigmoid(x)


def _adaln_kernel(c_ref, w_ref, b_ref, o_ref):
    s = _silu(c_ref[...]).astype(BF16)
    o_ref[...] = _dot(s, w_ref[...].astype(BF16)) + b_ref[...]


def _adaln(c_all, w_ada, b_ada):
    n, d = c_all.shape
    e = w_ada.shape[1]
    tn = 1536
    return pl.pallas_call(
        _adaln_kernel,
        grid=(e // tn,),
        in_specs=[pl.BlockSpec((n, d), lambda j: (0, 0)),
                  pl.BlockSpec((d, tn), lambda j: (0, j)),
                  pl.BlockSpec((1, tn), lambda j: (0, j))],
        out_specs=pl.BlockSpec((n, tn), lambda j: (0, j)),
        out_shape=jax.ShapeDtypeStruct((n, e), F32),
        compiler_params=_cparams(("arbitrary",)),
        name="adaln",
    )(c_all, w_ada, b_ada.reshape(1, e))


def _inproj_common(x_ref, sc_ref, sh_ref, gmix_ref, w2_ref):
    h = (_rms_rows(x_ref[...]) * gmix_ref[...] * (1.0 + sc_ref[...]) + sh_ref[...]).astype(BF16)

    def seg(a, b):
        return _dot(h, w2_ref[:, a:b])

    return seg


def _heads_q(seg, gq_ref, wuq_ref, gqq_ref, cos, sin):
    cqn = (_rms_rows(seg(S_Q, S_KV)) * gq_ref[...]).astype(BF16)
    q2 = _dot(cqn, wuq_ref[...])
    out = []
    for hh in range(N_HEADS):
        qm = q2[:, hh * HP:(hh + 1) * HP]
        qr = q2[:, (N_HEADS + hh) * HP:(N_HEADS + hh + 1) * HP]
        qh = qm * cos + qr * sin
        inv = lax.rsqrt(jnp.sum(qh * qh, axis=-1, keepdims=True) * (1.0 / QK_DIM) + EPS)
        out.append(qh * inv * gqq_ref[...] * (QK_DIM ** -0.5))
    return out


def _latent_kv(seg, gkv_ref, cos, sin):
    ckv = _rms_rows(seg(S_KV, S_KRM)) * gkv_ref[...]
    krp = seg(S_KRM, S_KRR) * cos + seg(S_KRR, S_CB) * sin
    return ckv, krp


def _gates_and_partial(seg, brb_in, brc_in, wbrb_ref, wbrc_ref, bg0_ref, part_ref):
    brb = _dot(brb_in.astype(BF16), wbrb_ref[...])
    brc = _dot(brc_in.astype(BF16), wbrc_ref[...])
    bg = jax.nn.sigmoid(seg(S_G, S_END))
    bg0_ref[...] = bg[:, :D_MODEL].astype(BF16)
    part_ref[...] = bg[:, D_MODEL:2 * D_MODEL] * brb + bg[:, 2 * D_MODEL:] * brc


def _layernorm_rows(v, g):
    xc = v - jnp.mean(v, axis=-1, keepdims=True)
    return xc * lax.rsqrt(jnp.mean(xc * xc, axis=-1, keepdims=True) + EPS) * g


def _inproj_prompt_kernel(x_ref, sc_ref, sh_ref, gmix_ref, w2_ref, gq_ref, wuq_ref, gqq_ref,
                          gkv_ref, wuk_ref, gqk_ref, wuv_ref, cos_ref, sin_ref, wconv_ref,
                          wbrb_ref, gvln_ref, wsp_ref, bsp_ref, wbrc_ref,
                          q_ref, k_ref, v_ref, ckv_ref, kr_ref, tail_ref, cv_ref, bg0_ref, part_ref,
                          zbuf_ref, *, tm):
    t = pl.program_id(1)
    seg = _inproj_common(x_ref, sc_ref, sh_ref, gmix_ref, w2_ref)
    cos = cos_ref[...]
    sin = sin_ref[...]

    qs = _heads_q(seg, gq_ref, wuq_ref, gqq_ref, cos, sin)
    for hh in range(N_HEADS):
        q_ref[hh] = qs[hh].astype(BF16)
    ckv, krp = _latent_kv(seg, gkv_ref, cos, sin)
    ckv_ref[...] = ckv
    kr_ref[...] = krp
    ckv_b = ckv.astype(BF16)
    k2 = _dot(ckv_b, wuk_ref[...])
    v2 = _dot(ckv_b, wuv_ref[...])
    for hh in range(N_HEADS):
        kh = k2[:, hh * HP:(hh + 1) * HP] + krp
        inv = lax.rsqrt(jnp.sum(kh * kh, axis=-1, keepdims=True) * (1.0 / QK_DIM) + EPS)
        k_ref[hh] = (kh * inv * gqk_ref[...]).astype(BF16)
        v_ref[hh] = v2[:, hh * HP:(hh + 1) * HP].astype(BF16)

    gate_b = seg(S_CB, S_CC)
    z = seg(S_CC, S_CH) * seg(S_CH, S_U)

    @pl.when(t == 0)
    def _():
        zbuf_ref[0:8, :] = jnp.zeros((8, CONV_DIM), F32)

    zbuf_ref[8:8 + tm, :] = z
    z1 = zbuf_ref[7:7 + tm, :]
    z2 = zbuf_ref[6:6 + tm, :]
    wc = wconv_ref[...]
    y = wc[0:1, :] * z2 + wc[1:2, :] * z1 + wc[2:3, :] * z
    zbuf_ref[0:8, :] = z[tm - 8:tm, :]
    tail_ref[...] = z[tm - 8:tm, :]

    u = seg(S_U, S_V)
    vn = _layernorm_rows(seg(S_V, S_G), gvln_ref[...])
    cv_ref[...] = vn[tm - CHUNK:tm, :]
    vnb = vn.astype(BF16)
    rows = lax.broadcasted_iota(jnp.int32, (GM_GROUPS * CHUNK, CHUNK), 0) % CHUNK
    cols = lax.broadcasted_iota(jnp.int32, (GM_GROUPS * CHUNK, CHUNK), 1)
    wsp = jnp.where(cols <= rows, wsp_ref[...], 0.0).astype(BF16)
    lane_grp = lax.broadcasted_iota(jnp.int32, (CHUNK, GM_DIM), 1) // GM_GROUP_DIM
    sgs = []
    for c in range(tm // CHUNK):
        r = _dot(wsp, vnb[c * CHUNK:(c + 1) * CHUNK, :])
        s = bsp_ref[...]
        for g in range(GM_GROUPS):
            s = s + jnp.where(lane_grp == g, r[g * CHUNK:(g + 1) * CHUNK, :], 0.0)
        sgs.append(u[c * CHUNK:(c + 1) * CHUNK, :] * s)
    sg = jnp.concatenate(sgs, axis=0)

    _gates_and_partial(seg, gate_b * y, sg, wbrb_ref, wbrc_ref, bg0_ref, part_ref)


def _inproj_sample_kernel(x_ref, sc_ref, sh_ref, gmix_ref, w2_ref, gq_ref, wuq_ref, gqq_ref,
                          gkv_ref, gqk_ref, cos_ref, sin_ref, wconv_ref, h0_ref, h1_ref,
                          wbrb_ref, gvln_ref, coef_ref, bias_ref, wbrc_ref,
                          qg_ref, ckv_ref, kr_ref, z_ref, cv_ref, bg0_ref, part_ref):
    seg = _inproj_common(x_ref, sc_ref, sh_ref, gmix_ref, w2_ref)
    cos = cos_ref[...]
    sin = sin_ref[...]
    qs = _heads_q(seg, gq_ref, wuq_ref, gqq_ref, cos, sin)
    for hh in range(N_HEADS):
        qg_ref[hh] = qs[hh] * gqk_ref[...]
    ckv, krp = _latent_kv(seg, gkv_ref, cos, sin)
    ckv_ref[...] = ckv
    kr_ref[...] = krp

    gate_b = seg(S_CB, S_CC)
    z = seg(S_CC, S_CH) * seg(S_CH, S_U)
    wc = wconv_ref[...]
    y = wc[0:1, :] * h0_ref[...] + wc[1:2, :] * h1_ref[...] + wc[2:3, :] * z
    z_ref[...] = z

    u = seg(S_U, S_V)
    vn = _layernorm_rows(seg(S_V, S_G), gvln_ref[...])
    cv_ref[...] = vn
    sg = u * (vn * coef_ref[...] + bias_ref[...])

    _gates_and_partial(seg, gate_b * y, sg, wbrb_ref, wbrc_ref, bg0_ref, part_ref)


def _inproj_prompt(x, scale_m, shift_m, lw, cos, sin):
    b, t, d = x.shape
    tm = TM_IN
    nt = t // tm
    tok = lambda w: pl.BlockSpec((None, tm, w), lambda i, j: (i, j, 0))
    mod = pl.BlockSpec((None, 1, d), lambda i, j: (i, 0, 0))
    head = pl.BlockSpec((None, N_HEADS, tm, HP), lambda i, j: (i, 0, j, 0))
    in_specs = [
        tok(d), mod, mod, _const_spec((1, d)), _const_spec((d, S_END)),
        _const_spec((1, Q_LORA)), _const_spec((Q_LORA, 2 * N_HEADS * HP)), _const_spec((1, HP)),
        _const_spec((1, KV_LORA)), _const_spec((KV_LORA, N_HEADS * HP)), _const_spec((1, HP)),
        _const_spec((KV_LORA, N_HEADS * HP)),
        pl.BlockSpec((tm, HP), lambda i, j: (j, 0)), pl.BlockSpec((tm, HP), lambda i, j: (j, 0)),
        _const_spec((CONV_W, CONV_DIM)), _const_spec((CONV_DIM, d)), _const_spec((1, GM_DIM)),
        _const_spec((GM_GROUPS * CHUNK, CHUNK)), _const_spec((CHUNK, GM_DIM)), _const_spec((GM_DIM, d)),
    ]
    out_specs = [
        head, head, head, tok(KV_LORA), tok(HP),
        pl.BlockSpec((None, 8, CONV_DIM), lambda i, j: (i, 0, 0)),
        pl.BlockSpec((None, CHUNK, GM_DIM), lambda i, j: (i, 0, 0)),
        tok(d), tok(d),
    ]
    out_shape = [
        jax.ShapeDtypeStruct((b, N_HEADS, t, HP), BF16),
        jax.ShapeDtypeStruct((b, N_HEADS, t, HP), BF16),
        jax.ShapeDtypeStruct((b, N_HEADS, t, HP), BF16),
        jax.ShapeDtypeStruct((b, t, KV_LORA), F32),
        jax.ShapeDtypeStruct((b, t, HP), F32),
        jax.ShapeDtypeStruct((b, 8, CONV_DIM), F32),
        jax.ShapeDtypeStruct((b, CHUNK, GM_DIM), F32),
        jax.ShapeDtypeStruct((b, t, d), BF16),
        jax.ShapeDtypeStruct((b, t, d), F32),
    ]
    return pl.pallas_call(
        functools.partial(_inproj_prompt_kernel, tm=tm),
        grid=(b, nt),
        in_specs=in_specs, out_specs=out_specs, out_shape=out_shape,
        scratch_shapes=[pltpu.VMEM((tm + 8, CONV_DIM), F32)],
        compiler_params=_cparams(("arbitrary", "arbitrary")),
        name="inproj_prompt",
    )(x, scale_m, shift_m, lw["g_norm_mix"], lw["w2"], lw["g_q_lat"], lw["wuq2"], lw["g_qk_q"],
      lw["g_kv_lat"], lw["wuk"], lw["g_qk_k"], lw["wuv"], cos, sin, lw["w_conv"],
      lw["w_br_b"], lw["g_v_ln"], lw["w_sp"], lw["b_sp"], lw["w_br_c"])


def _inproj_sample(x, scale_m, shift_m, lw, cos, sin, hist0, hist1):
    n, d = x.shape
    full = lambda *s: pl.BlockSpec(s, lambda i: (0,) * len(s))
    in_specs = [
        full(n, d), full(n, d), full(n, d), full(1, d), full(d, S_END),
        full(1, Q_LORA), full(Q_LORA, 2 * N_HEADS * HP), full(1, HP),
        full(1, KV_LORA), full(1, HP), full(1, HP), full(1, HP),
        full(CONV_W, CONV_DIM), full(n, CONV_DIM), full(n, CONV_DIM),
        full(CONV_DIM, d), full(1, GM_DIM), full(1, GM_DIM), full(1, GM_DIM), full(GM_DIM, d),
    ]
    out_specs = [full(N_HEADS, n, HP), full(n, KV_LORA), full(n, HP), full(n, CONV_DIM),
                 full(n, GM_DIM), full(n, d), full(n, d)]
    out_shape = [
        jax.ShapeDtypeStruct((N_HEADS, n, HP), F32),
        jax.ShapeDtypeStruct((n, KV_LORA), F32),
        jax.ShapeDtypeStruct((n, HP), F32),
        jax.ShapeDtypeStruct((n, CONV_DIM), F32),
        jax.ShapeDtypeStruct((n, GM_DIM), F32),
        jax.ShapeDtypeStruct((n, d), BF16),
        jax.ShapeDtypeStruct((n, d), F32),
    ]
    return pl.pallas_call(
        _inproj_sample_kernel,
        grid=(1,),
        in_specs=in_specs, out_specs=out_specs, out_shape=out_shape,
        compiler_params=_cparams(("arbitrary",)),
        name="inproj_sample",
    )(x, scale_m, shift_m, lw["g_norm_mix"], lw["w2"], lw["g_q_lat"], lw["wuq2"], lw["g_qk_q"],
      lw["g_kv_lat"], lw["g_qk_k"], cos, sin, lw["w_conv"], hist0, hist1,
      lw["w_br_b"], lw["g_v_ln"], lw["sp_coef"], lw["b_sp"][0:1], lw["w_br_c"])


def _flash_kernel(q_ref, k_ref, v_ref, o_ref, *, tq, tk):
    qi = pl.program_id(2)
    q = q_ref[...]
    row = qi * tq + lax.broadcasted_iota(jnp.int32, (tq, tk), 0)
    col0 = lax.broadcasted_iota(jnp.int32, (tq, tk), 1)

    def body(j, carry):
        m, l, acc = carry
        start = pl.multiple_of(j * tk, tk)
        k = k_ref[pl.ds(start, tk), :]
        v = v_ref[pl.ds(start, tk), :]
        s = lax.dot_general(q, k, (((1,), (1,)), ((), ())), preferred_element_type=F32)
        s = jnp.where(col0 + j * tk <= row, s, -jnp.inf)
        m_new = jnp.maximum(m, jnp.max(s, axis=-1, keepdims=True))
        alpha = jnp.exp(m - m_new)
        p = jnp.exp(s - m_new)
        l = alpha * l + jnp.sum(p, axis=-1, keepdims=True)
        acc = alpha * acc + _dot(p.astype(BF16), v)
        return m_new, l, acc

    n_blocks = ((qi + 1) * tq + tk - 1) // tk
    m0 = jnp.full((tq, 1), -jnp.inf, F32)
    l0 = jnp.zeros((tq, 1), F32)
    a0 = jnp.zeros((tq, HP), F32)
    m, l, acc = lax.fori_loop(0, n_blocks, body, (m0, l0, a0))
    o_ref[...] = (acc / l).astype(BF16)


def _flash_attention(q, k, v):
    b, h, t, _ = q.shape
    tq, tk = TQ, TK
    return pl.pallas_call(
        functools.partial(_flash_kernel, tq=tq, tk=tk),
        grid=(b, h, t // tq),
        in_specs=[pl.BlockSpec((None, None, tq, HP), lambda i, j, n: (i, j, n, 0)),
                  pl.BlockSpec((None, None, t, HP), lambda i, j, n: (i, j, 0, 0)),
                  pl.BlockSpec((None, None, t, HP), lambda i, j, n: (i, j, 0, 0))],
        out_specs=pl.BlockSpec((None, tq, HP), lambda i, j, n: (i, n, j)),
        out_shape=jax.ShapeDtypeStruct((b, t, h * HP), BF16),
        compiler_params=_cparams(("arbitrary", "arbitrary", "arbitrary")),
        name="flash_prompt",
    )(q, k, v)


def _paged_kernel(pt_ref, qn_ref, qrt_ref, cnew_ref, krnew_ref, wk_ref, e_ref, *rest, n_pg):
    c_refs = rest[:n_pg]
    kr_refs = rest[n_pg:2 * n_pg]
    o_ref = rest[2 * n_pg]
    m_ref, l_ref, acc_ref = rest[2 * n_pg + 1:]
    j = pl.program_id(1)
    nj = pl.num_programs(1)
    hr = 16

    @pl.when(j == 0)
    def _():
        m_ref[...] = jnp.full(m_ref.shape, -jnp.inf, F32)
        l_ref[...] = jnp.zeros(l_ref.shape, F32)
        acc_ref[...] = jnp.zeros(acc_ref.shape, F32)

    qn = qn_ref[...]
    qrt = qrt_ref[...].astype(BF16)
    e = e_ref[...]

    def update(cbf, krf, valid_rows):
        cb = cbf.astype(BF16)
        kn = _dot(cb, wk_ref[...])
        ssq = _dot((kn * kn).astype(BF16), e) + jnp.sum(krf * krf, axis=-1, keepdims=True)
        s = _dot((kn * qn).astype(BF16), e) + _dot(krf.astype(BF16), qrt)
        s = s * lax.rsqrt(ssq * (1.0 / QK_DIM) + EPS)
        st = s.T[:hr, :]
        if valid_rows is not None:
            kcol = lax.broadcasted_iota(jnp.int32, st.shape, 1)
            st = jnp.where(kcol < valid_rows, st, -jnp.inf)
        m_old = m_ref[...]
        m_new = jnp.maximum(m_old, jnp.max(st, axis=-1, keepdims=True))
        alpha = jnp.exp(m_old - m_new)
        p = jnp.exp(st - m_new)
        l_ref[...] = alpha * l_ref[...] + jnp.sum(p, axis=-1, keepdims=True)
        acc_ref[...] = alpha * acc_ref[...] + _dot(p.astype(BF16), cb)
        m_ref[...] = m_new

    update(jnp.concatenate([r[...] for r in c_refs], axis=0),
           jnp.concatenate([r[...] for r in kr_refs], axis=0), None)

    @pl.when(j == nj - 1)
    def _():
        first_c = lax.broadcasted_iota(jnp.int32, (LANES, KV_LORA), 0) == 0
        first_r = lax.broadcasted_iota(jnp.int32, (LANES, QK_ROPE), 0) == 0
        cn = jnp.where(first_c, jnp.broadcast_to(cnew_ref[...], (LANES, KV_LORA)), 0.0)
        kn_ = jnp.where(first_r, jnp.broadcast_to(krnew_ref[...], (LANES, QK_ROPE)), 0.0)
        update(cn, kn_, 1)
        o_ref[...] = (acc_ref[...] / l_ref[...])[:N_HEADS, :]


def _paged_attention(layer, page_table, qn, qrt, cnew, krnew, wk, e_ind, pool_c, pool_kr):
    s, n_pages = page_table.shape
    n_pg = PAGES_PER_STEP
    nj = n_pages // n_pg
    pt = page_table.reshape(-1)

    def page_spec(width, p):
        return pl.BlockSpec((None, None, PAGE_SIZE, width),
                            lambda i, j, pt_ref: (layer, pt_ref[i * n_pages + j * n_pg + p], 0, 0))

    per_seq = lambda *shape: pl.BlockSpec((None,) + shape, lambda i, j, pt_ref: (i,) + (0,) * len(shape))
    const = lambda *shape: pl.BlockSpec(shape, lambda i, j, pt_ref: (0,) * len(shape))
    in_specs = ([per_seq(1, N_HEADS * QK_NOPE), per_seq(QK_ROPE, LANES), per_seq(1, KV_LORA),
                 per_seq(1, QK_ROPE), const(KV_LORA, N_HEADS * QK_NOPE), const(N_HEADS * QK_NOPE, LANES)]
                + [page_spec(KV_LORA, p) for p in range(n_pg)]
                + [page_spec(QK_ROPE, p) for p in range(n_pg)])
    grid_spec = pltpu.PrefetchScalarGridSpec(
        num_scalar_prefetch=1,
        grid=(s, nj),
        in_specs=in_specs,
        out_specs=per_seq(N_HEADS, KV_LORA),
        scratch_shapes=[pltpu.VMEM((16, 1), F32), pltpu.VMEM((16, 1), F32), pltpu.VMEM((16, KV_LORA), F32)],
    )
    return pl.pallas_call(
        functools.partial(_paged_kernel, n_pg=n_pg),
        grid_spec=grid_spec,
        out_shape=jax.ShapeDtypeStruct((s, N_HEADS, KV_LORA), F32),
        compiler_params=_cparams(("arbitrary", "arbitrary")),
        name="paged_sample",
    )(pt, qn, qrt, cnew, krnew, wk, e_ind, *([pool_c] * n_pg), *([pool_kr] * n_pg))


def _uv_kernel(ol_ref, wuv_ref, o_ref):
    for hh in range(N_HEADS):
        ol = ol_ref[:, hh * KV_LORA:(hh + 1) * KV_LORA].astype(BF16)
        o_ref[:, hh * HP:(hh + 1) * HP] = _dot(ol, wuv_ref[:, hh * HP:(hh + 1) * HP]).astype(BF16)


def _uv_project(olat, wuv):
    n = olat.shape[0]
    full = lambda *s: pl.BlockSpec(s, lambda i: (0,) * len(s))
    return pl.pallas_call(
        _uv_kernel, grid=(1,),
        in_specs=[full(n, N_HEADS * KV_LORA), full(KV_LORA, N_HEADS * HP)],
        out_specs=full(n, N_HEADS * HP),
        out_shape=jax.ShapeDtypeStruct((n, N_HEADS * HP), BF16),
        compiler_params=_cparams(("arbitrary",)),
        name="uv_sample",
    )(olat, wuv)


def _merge_kernel(oa_ref, wbra_ref, bg0_ref, part_ref, x_ref, gm_ref, wout_ref, gffn_ref, scf_ref, shf_ref,
                  x1_ref, h2_ref):
    bra = _dot(oa_ref[...], wbra_ref[...])
    merged = bg0_ref[...].astype(F32) * bra + part_ref[...]
    x1 = x_ref[...] + gm_ref[...] * _dot(merged.astype(BF16), wout_ref[...])
    x1_ref[...] = x1
    h2_ref[...] = (_rms_rows(x1) * gffn_ref[...] * (1.0 + scf_ref[...]) + shf_ref[...]).astype(BF16)


def _merge(oa, bg0, part, x, gate_m, scale_f, shift_f, lw, tm):
    g, t, d = x.shape
    mt = gate_m.shape[1]
    tok = lambda w: pl.BlockSpec((None, tm, w), lambda i, j: (i, j, 0))
    if mt == 1:
        mod = pl.BlockSpec((None, 1, d), lambda i, j: (i, 0, 0))
    else:
        mod = pl.BlockSpec((None, tm, d), lambda i, j: (i, j, 0))
    return pl.pallas_call(
        _merge_kernel,
        grid=(g, t // tm),
        in_specs=[tok(N_HEADS * HP), _const_spec((N_HEADS * HP, d)), tok(d), tok(d), tok(d), mod,
                  _const_spec((d, d)), _const_spec((1, d)), mod, mod],
        out_specs=[tok(d), tok(d)],
        out_shape=[jax.ShapeDtypeStruct((g, t, d), F32), jax.ShapeDtypeStruct((g, t, d), BF16)],
        compiler_params=_cparams(("arbitrary", "arbitrary")),
        name="merge",
    )(oa, lw["w_br_a"], bg0, part, x, gate_m, lw["w_out"], lw["g_norm_ffn"], scale_f, shift_f)


def _router_kernel(h_ref, wr_ref, br_ref, g_ref):
    tm = h_ref.shape[0]
    scores = jax.nn.sigmoid(_dot(h_ref[...], wr_ref[...]))
    lane = lax.broadcasted_iota(jnp.int32, (tm, LANES), 1)
    neg = -jnp.inf
    choice = jnp.where(lane < N_EXPERTS, scores + br_ref[...], neg)
    grp = lane // GROUP_SIZE

    gsc = []
    for g in range(N_EXPERT_GROUPS):
        xg = jnp.where(grp == g, choice, neg)
        m1 = jnp.max(xg, axis=-1, keepdims=True)
        cnt = jnp.sum(jnp.where(xg == m1, 1.0, 0.0), axis=-1, keepdims=True)
        m2 = jnp.max(jnp.where(xg < m1, xg, neg), axis=-1, keepdims=True)
        gsc.append(m1 + jnp.where(cnt >= 2.0, m1, m2))

    emask = jnp.zeros((tm, LANES), F32)
    for g in range(N_EXPERT_GROUPS):
        rank = jnp.zeros((tm, 1), F32)
        for o in range(N_EXPERT_GROUPS):
            if o == g:
                continue
            beats = (gsc[o] > gsc[g]) if o > g else (gsc[o] >= gsc[g])
            rank = rank + jnp.where(beats, 1.0, 0.0)
        keep = jnp.where(rank < float(TOPK_GROUPS), 1.0, 0.0)
        emask = emask + jnp.where(grp == g, keep, 0.0)

    cand = jnp.where(emask > 0.0, choice, neg)
    picked = jnp.zeros((tm, LANES), F32)
    for _ in range(TOP_K):
        m = jnp.max(cand, axis=-1, keepdims=True)
        idx = jnp.min(jnp.where(cand == m, lane, LANES), axis=-1, keepdims=True)
        hit = lane == idx
        picked = jnp.where(hit, 1.0, picked)
        cand = jnp.where(hit, neg, cand)

    w = jnp.where(picked > 0.0, scores, 0.0)
    g_ref[...] = w / jnp.sum(w, axis=-1, keepdims=True) * ROUTED_SCALE


def _router(h2, lw, tm):
    m, d = h2.shape
    return pl.pallas_call(
        _router_kernel,
        grid=(m // tm,),
        in_specs=[pl.BlockSpec((tm, d), lambda i: (i, 0)), _const_spec((d, LANES)), _const_spec((1, LANES))],
        out_specs=pl.BlockSpec((tm, LANES), lambda i: (i, 0)),
        out_shape=jax.ShapeDtypeStruct((m, LANES), F32),
        compiler_params=_cparams(("arbitrary",)),
        name="router",
    )(h2, lw["w_router"], lw["b_router"])


def _moe_kernel(h_ref, g_ref, wg_ref, wu_ref, wd_ref, wsg_ref, wsu_ref, wsd_ref, x1_ref, gf_ref,
                o_ref, acc_ref, *, epb):
    e = pl.program_id(2)
    h = h_ref[...]
    tm = h.shape[0]

    @pl.when(e == 0)
    def _():
        hs = (_silu(_dot(h, wsg_ref[...])) * _dot(h, wsu_ref[...])).astype(BF16)
        acc_ref[...] = _dot(hs, wsd_ref[...])

    lane = lax.broadcasted_iota(jnp.int32, (tm, LANES), 1)
    gates = g_ref[...]
    for jj in range(epb):
        a = _dot(h, wg_ref[jj])
        b = _dot(h, wu_ref[jj])
        gcol = jnp.sum(jnp.where(lane == e * epb + jj, gates, 0.0), axis=-1, keepdims=True)
        hh = (_silu(a) * b * gcol).astype(BF16)
        acc_ref[...] += _dot(hh, wd_ref[jj])

    @pl.when(e == pl.num_programs(2) - 1)
    def _():
        o_ref[...] = x1_ref[...] + gf_ref[...] * acc_ref[...]


def _moe(h2, gates, x1, gate_f, lw, tm):
    g, t, d = x1.shape
    mt = gate_f.shape[1]
    epb = EXPERTS_PER_STEP
    tok = lambda w: pl.BlockSpec((None, tm, w), lambda i, j, e: (i, j, 0))
    if mt == 1:
        mod = pl.BlockSpec((None, 1, d), lambda i, j, e: (i, 0, 0))
    else:
        mod = pl.BlockSpec((None, tm, d), lambda i, j, e: (i, j, 0))
    return pl.pallas_call(
        functools.partial(_moe_kernel, epb=epb),
        grid=(g, t // tm, N_EXPERTS // epb),
        in_specs=[tok(d), tok(LANES),
                  pl.BlockSpec((epb, d, D_EXPERT), lambda i, j, e: (e, 0, 0)),
                  pl.BlockSpec((epb, d, D_EXPERT), lambda i, j, e: (e, 0, 0)),
                  pl.BlockSpec((epb, D_EXPERT, d), lambda i, j, e: (e, 0, 0)),
                  _const_spec((d, D_EXPERT)), _const_spec((d, D_EXPERT)), _const_spec((D_EXPERT, d)),
                  tok(d), mod],
        out_specs=tok(d),
        out_shape=jax.ShapeDtypeStruct((g, t, d), F32),
        scratch_shapes=[pltpu.VMEM((tm, d), F32)],
        compiler_params=_cparams(("arbitrary", "arbitrary", "arbitrary")),
        name="moe",
    )(h2, gates, lw["w_e_gate"], lw["w_e_up"], lw["w_e_down"], lw["w_sh_gate"], lw["w_sh_up"],
      lw["w_sh_down"], x1, gate_f)


def _pad_heads(w, width):
    pad = [(0, 0)] * (w.ndim - 1) + [(0, HP - width)]
    wp = jnp.pad(w, pad)
    return wp.reshape(w.shape[:-2] + (w.shape[-2] * HP,))


def _rot_cols(w):
    half = QK_ROPE // 2
    return jnp.concatenate([-w[..., half:], w[..., :half]], axis=-1)


def _prep_layer(l, p):
    d = D_MODEL
    w_in = p["w_in"][l]
    off_kv = Q_LORA
    off_kr = off_kv + KV_LORA
    off_cb = off_kr + QK_ROPE
    off_g = off_cb + 3 * CONV_DIM + 2 * GM_DIM
    wkr = w_in[:, off_kr:off_cb]
    lead = jnp.zeros((d, QK_NOPE), F32)
    trail = jnp.zeros((d, HP - QK_DIM), F32)
    w2 = jnp.concatenate([
        w_in[:, :off_kr],
        lead, wkr, trail,
        lead, _rot_cols(wkr), trail,
        w_in[:, off_cb:],
    ], axis=1).astype(BF16)
    assert w2.shape[1] == S_END and off_g + N_BRANCH * d == w_in.shape[1]

    wuq = p["w_uq"][l]
    wuq_rot = jnp.concatenate([jnp.zeros_like(wuq[..., :QK_NOPE]), _rot_cols(wuq[..., QK_NOPE:])], axis=-1)
    wuq2 = jnp.concatenate([_pad_heads(wuq, QK_DIM), _pad_heads(wuq_rot, QK_DIM)], axis=-1).astype(BF16)

    def pad_gain(gq):
        return jnp.pad(gq, (0, HP - QK_DIM)).reshape(1, HP)

    w_br_a = jnp.pad(p["w_br_a"][l].reshape(N_HEADS, V_DIM, d), ((0, 0), (0, HP - V_DIM), (0, 0)))
    w_sp = p["w_spatial"][l]
    b_sp = jnp.repeat(p["b_spatial"][l].T, GM_GROUP_DIM, axis=1)
    return dict(
        w_ada=p["w_ada"][l], b_ada=p["b_ada"][l],
        g_norm_mix=p["g_norm_mix"][l].reshape(1, d), g_norm_ffn=p["g_norm_ffn"][l].reshape(1, d),
        w2=w2, g_q_lat=p["g_q_lat"][l].reshape(1, Q_LORA), wuq2=wuq2,
        g_qk_q=pad_gain(p["g_qk_q"][l]), g_qk_k=pad_gain(p["g_qk_k"][l]),
        g_kv_lat=p["g_kv_lat"][l].reshape(1, KV_LORA),
        wuk=_pad_heads(p["w_uk"][l], QK_NOPE).astype(BF16),
        wuk_flat=p["w_uk"][l].reshape(KV_LORA, N_HEADS * QK_NOPE).astype(BF16),
        wuv=_pad_heads(p["w_uv"][l], V_DIM).astype(BF16),
        w_br_a=w_br_a.reshape(N_HEADS * HP, d).astype(BF16),
        w_conv=p["w_conv"][l], w_br_b=p["w_br_b"][l].astype(BF16),
        g_v_ln=p["g_v_ln"][l].reshape(1, GM_DIM),
        w_sp=w_sp.reshape(GM_GROUPS * CHUNK, CHUNK), b_sp=b_sp,
        sp_coef=jnp.repeat(w_sp[:, 0, 0], GM_GROUP_DIM).reshape(1, GM_DIM),
        w_br_c=p["w_br_c"][l].astype(BF16), w_out=p["w_out"][l].astype(BF16),
        w_router=jnp.pad(p["w_router"][l], ((0, 0), (0, LANES - N_EXPERTS))).astype(BF16),
        b_router=jnp.pad(p["b_router"][l], (0, LANES - N_EXPERTS)).reshape(1, LANES),
        w_e_gate=p["w_e_gate"][l].astype(BF16), w_e_up=p["w_e_up"][l].astype(BF16),
        w_e_down=p["w_e_down"][l].astype(BF16),
        w_sh_gate=p["w_sh_gate"][l].astype(BF16), w_sh_up=p["w_sh_up"][l].astype(BF16),
        w_sh_down=p["w_sh_down"][l].astype(BF16),
    )


def _rope_tables(pos):
    inv_freq = ROPE_THETA ** (-jnp.arange(0, QK_ROPE, 2, dtype=F32) / QK_ROPE)
    ang = pos.astype(F32)[:, None] * inv_freq[None, :]
    c, s = jnp.cos(ang), jnp.sin(ang)
    n = pos.shape[0]
    cos = jnp.concatenate([jnp.ones((n, QK_NOPE), F32), c, c, jnp.ones((n, HP - QK_DIM), F32)], axis=1)
    sin = jnp.concatenate([jnp.zeros((n, QK_NOPE), F32), s, s, jnp.zeros((n, HP - QK_DIM), F32)], axis=1)
    return cos, sin


def _split_mod(m):
    return [m[:, i * D_MODEL:(i + 1) * D_MODEL] for i in range(6)]


def _prompt_layer(x, mod, lw, cos, sin):
    b, t, d = x.shape
    shift_m, scale_m, gate_m, shift_f, scale_f, gate_f = [a.reshape(b, 1, d) for a in _split_mod(mod)]
    q, k, v, ckv, krp, tail, cv, bg0, part = _inproj_prompt(x, scale_m, shift_m, lw, cos, sin)
    oa = _flash_attention(q, k, v)
    x1, h2 = _merge(oa, bg0, part, x, gate_m, scale_f, shift_f, lw, TM_MERGE)
    gates = _router(h2.reshape(b * t, d), lw, TM_ROUTE).reshape(b, t, LANES)
    y = _moe(h2, gates, x1, gate_f, lw, TM_MOE)
    state = (ckv, krp[:, :, QK_NOPE:QK_DIM], tail[:, 8 - (CONV_W - 1):], cv)
    return y, state


def _sample_layer(layer, x, mod, lw, cos, sin, hist, pool_c, pool_kr, page_table, e_ind):
    n, d = x.shape
    shift_m, scale_m, gate_m, shift_f, scale_f, gate_f = _split_mod(mod)
    qg, ckv, krp, z, cv, bg0, part = _inproj_sample(x, scale_m, shift_m, lw, cos, sin, hist[:, 0], hist[:, 1])
    qn = qg[:, :, :QK_NOPE].transpose(1, 0, 2).reshape(n, 1, N_HEADS * QK_NOPE)
    qrt = jnp.pad(qg[:, :, QK_NOPE:QK_DIM].transpose(1, 2, 0), ((0, 0), (0, 0), (0, LANES - N_HEADS)))
    kr = krp[:, QK_NOPE:QK_DIM]
    olat = _paged_attention(layer, page_table, qn, qrt, ckv.reshape(n, 1, KV_LORA), kr.reshape(n, 1, QK_ROPE),
                            lw["wuk_flat"], e_ind, pool_c, pool_kr)
    oa = _uv_project(olat.reshape(n, N_HEADS * KV_LORA), lw["wuv"])
    g3 = lambda a: a.reshape(1, n, -1)
    x1, h2 = _merge(g3(oa), g3(bg0), g3(part), g3(x), g3(gate_m), g3(scale_f), g3(shift_f), lw, n)
    gates = _router(h2.reshape(n, d), lw, n)
    y = _moe(h2, g3(gates), x1, g3(gate_f), lw, n).reshape(n, d)
    state = (ckv.reshape(n, 1, KV_LORA), kr.reshape(n, 1, QK_ROPE),
             jnp.stack([hist[:, 1], z], axis=1), cv.reshape(n, 1, GM_DIM))
    return y, state


def kernel(x_prompt, x_sample, cache_kv_latent, cache_k_rope, state_conv, page_table, c_prompt, c_sample,
           w_ada, b_ada, g_norm_mix, g_norm_ffn, w_in, g_q_lat, w_uq, g_kv_lat, w_uk, w_uv, g_qk_q, g_qk_k,
           w_br_a, w_conv, w_br_b, g_v_ln, w_spatial, b_spatial, w_br_c, w_out, w_router, b_router,
           w_e_gate, w_e_up, w_e_down, w_sh_gate, w_sh_up, w_sh_down):
    params = dict(w_ada=w_ada, b_ada=b_ada, g_norm_mix=g_norm_mix, g_norm_ffn=g_norm_ffn, w_in=w_in,
                  g_q_lat=g_q_lat, w_uq=w_uq, g_kv_lat=g_kv_lat, w_uk=w_uk, w_uv=w_uv, g_qk_q=g_qk_q,
                  g_qk_k=g_qk_k, w_br_a=w_br_a, w_conv=w_conv, w_br_b=w_br_b, g_v_ln=g_v_ln,
                  w_spatial=w_spatial, b_spatial=b_spatial, w_br_c=w_br_c, w_out=w_out, w_router=w_router,
                  b_router=b_router, w_e_gate=w_e_gate, w_e_up=w_e_up, w_e_down=w_e_down,
                  w_sh_gate=w_sh_gate, w_sh_up=w_sh_up, w_sh_down=w_sh_down)
    depth = w_in.shape[0]
    nb, t, d = x_prompt.shape
    ns = x_sample.shape[0]
    assert x_sample.shape[1] == 1 and t % TQ == 0 and t % TM_MOE == 0
    past_len = page_table.shape[1] * PAGE_SIZE

    cos_p, sin_p = _rope_tables(jnp.arange(t))
    cos_s, sin_s = _rope_tables(past_len + jnp.arange(1))
    c_all = jnp.concatenate([c_prompt, c_sample], axis=0)
    head_of = jnp.arange(N_HEADS * QK_NOPE) // QK_NOPE
    e_ind = (head_of[:, None] == jnp.arange(LANES)[None, :]).astype(BF16)

    hp, hs = x_prompt, x_sample.reshape(ns, d)
    st_p, st_s = [], []
    for l in range(depth):
        lw = _prep_layer(l, params)
        mod = _adaln(c_all, lw["w_ada"], lw["b_ada"])
        hp, sp = _prompt_layer(hp, mod[:nb], lw, cos_p, sin_p)
        hs, ss = _sample_layer(l, hs, mod[nb:], lw, cos_s, sin_s, state_conv[l], cache_kv_latent, cache_k_rope,
                               page_table, e_ind)
        st_p.append(sp)
        st_s.append(ss)

    stack = lambda sts, i: jnp.stack([s[i] for s in sts])
    return (hp, hs.reshape(ns, 1, d),
            stack(st_p, 0), stack(st_p, 1), stack(st_p, 2), stack(st_p, 3),
            stack(st_s, 0), stack(st_s, 1), stack(st_s, 2), stack(st_s, 3))
```

```python
import functools

import jax
import jax.numpy as jnp
from jax import lax
from jax.experimental import pallas as pl
from jax.experimental.pallas import tpu as pltpu

F32 = jnp.float32
BF16 = jnp.bfloat16

D_MODEL = 1024
N_HEADS = 8
QK_NOPE = 64
QK_ROPE = 32
QK_DIM = QK_NOPE + QK_ROPE
V_DIM = 64
Q_LORA = 384
KV_LORA = 256
ROPE_THETA = 10000.0
CONV_W = 3
CONV_DIM = 512
CHUNK = 128
GM_GROUPS = 8
GM_DIM = 512
GM_GROUP_DIM = GM_DIM // GM_GROUPS
N_EXPERTS = 64
TOP_K = 8
N_EXPERT_GROUPS = 8
GROUP_SIZE = N_EXPERTS // N_EXPERT_GROUPS
TOPK_GROUPS = 4
D_EXPERT = 256
ROUTED_SCALE = 2.5
N_BRANCH = 3
EPS = 1e-6
PAGE_SIZE = 128

LANES = 128
HP = LANES
VMEM_LIMIT = 56 * 1024 * 1024

S_Q = 0
S_KV = S_Q + Q_LORA
S_KRM = S_KV + KV_LORA
S_KRR = S_KRM + HP
S_CB = S_KRR + HP
S_CC = S_CB + CONV_DIM
S_CH = S_CC + CONV_DIM
S_U = S_CH + CONV_DIM
S_V = S_U + GM_DIM
S_G = S_V + GM_DIM
S_END = S_G + N_BRANCH * D_MODEL

TM_IN = 512
TQ = 512
TK = 512
TM_MERGE = 512
TM_ROUTE = 512
TM_MOE = 1024
EXPERTS_PER_STEP = 2
PAGES_PER_STEP = 32
PAGE_CHUNKS = 2
HEAD_ROWS = 16


def _cparams(sem):
    return pltpu.CompilerParams(dimension_semantics=sem, vmem_limit_bytes=VMEM_LIMIT)


def _const_spec(shape):
    nd = len(shape)
    return pl.BlockSpec(shape, lambda *_: (0,) * nd, pipeline_mode=pl.Buffered(1))


def _dot(a, b):
    return jnp.dot(a, b, preferred_element_type=F32)


def _rms_rows(x):
    return x * lax.rsqrt(jnp.mean(x * x, axis=-1, keepdims=True) + EPS)


def _silu(x):
    return x * jax.nn.sigmoid(x)


def _adaln_kernel(c_ref, w_ref, b_ref, o_ref):
    s = _silu(c_ref[...]).astype(BF16)
    o_ref[...] = _dot(s, w_ref[...].astype(BF16)) + b_ref[...]


def _adaln(c_all, w_ada, b_ada):
    n, d = c_all.shape
    e = w_ada.shape[1]
    tn = 1536
    return pl.pallas_call(
        _adaln_kernel,
        grid=(e // tn,),
        in_specs=[pl.BlockSpec((n, d), lambda j: (0, 0)),
                  pl.BlockSpec((d, tn), lambda j: (0, j)),
                  pl.BlockSpec((1, tn), lambda j: (0, j))],
        out_specs=pl.BlockSpec((n, tn), lambda j: (0, j)),
        out_shape=jax.ShapeDtypeStruct((n, e), F32),
        compiler_params=_cparams(("arbitrary",)),
        name="adaln",
    )(c_all, w_ada, b_ada.reshape(1, e))


def _inproj_common(x_ref, sc_ref, sh_ref, gmix_ref, w2_ref):
    h = (_rms_rows(x_ref[...]) * gmix_ref[...] * (1.0 + sc_ref[...]) + sh_ref[...]).astype(BF16)

    def seg(a, b):
        return _dot(h, w2_ref[:, a:b])

    return seg


def _heads_q(seg, gq_ref, wuq_ref, gqq_ref, cos, sin):
    cqn = (_rms_rows(seg(S_Q, S_KV)) * gq_ref[...]).astype(BF16)
    q2 = _dot(cqn, wuq_ref[...])
    out = []
    for hh in range(N_HEADS):
        qm = q2[:, hh * HP:(hh + 1) * HP]
        qr = q2[:, (N_HEADS + hh) * HP:(N_HEADS + hh + 1) * HP]
        qh = qm * cos + qr * sin
        inv = lax.rsqrt(jnp.sum(qh * qh, axis=-1, keepdims=True) * (1.0 / QK_DIM) + EPS)
        out.append(qh * inv * gqq_ref[...] * (QK_DIM ** -0.5))
    return out


def _latent_kv(seg, gkv_ref, cos, sin):
    ckv = _rms_rows(seg(S_KV, S_KRM)) * gkv_ref[...]
    krp = seg(S_KRM, S_KRR) * cos + seg(S_KRR, S_CB) * sin
    return ckv, krp


def _gates_and_partial(seg, brb_in, brc_in, wbrb_ref, wbrc_ref, bg0_ref, part_ref):
    brb = _dot(brb_in.astype(BF16), wbrb_ref[...])
    brc = _dot(brc_in.astype(BF16), wbrc_ref[...])
    bg = jax.nn.sigmoid(seg(S_G, S_END))
    bg0_ref[...] = bg[:, :D_MODEL].astype(BF16)
    part_ref[...] = bg[:, D_MODEL:2 * D_MODEL] * brb + bg[:, 2 * D_MODEL:] * brc


def _layernorm_rows(v, g):
    xc = v - jnp.mean(v, axis=-1, keepdims=True)
    return xc * lax.rsqrt(jnp.mean(xc * xc, axis=-1, keepdims=True) + EPS) * g


def _inproj_prompt_kernel(x_ref, sc_ref, sh_ref, gmix_ref, w2_ref, gq_ref, wuq_ref, gqq_ref,
                          gkv_ref, wuk_ref, gqk_ref, wuv_ref, cos_ref, sin_ref, wconv_ref,
                          wbrb_ref, gvln_ref, wsp_ref, bsp_ref, wbrc_ref,
                          q_ref, k_ref, v_ref, ckv_ref, kr_ref, tail_ref, cv_ref, bg0_ref, part_ref,
                          zbuf_ref, *, tm):
    t = pl.program_id(1)
    seg = _inproj_common(x_ref, sc_ref, sh_ref, gmix_ref, w2_ref)
    cos = cos_ref[...]
    sin = sin_ref[...]

    qs = _heads_q(seg, gq_ref, wuq_ref, gqq_ref, cos, sin)
    for hh in range(N_HEADS):
        q_ref[hh] = qs[hh].astype(BF16)
    ckv, krp = _latent_kv(seg, gkv_ref, cos, sin)
    ckv_ref[...] = ckv
    kr_ref[...] = krp
    ckv_b = ckv.astype(BF16)
    k2 = _dot(ckv_b, wuk_ref[...])
    v2 = _dot(ckv_b, wuv_ref[...])
    for hh in range(N_HEADS):
        kh = k2[:, hh * HP:(hh + 1) * HP] + krp
        inv = lax.rsqrt(jnp.sum(kh * kh, axis=-1, keepdims=True) * (1.0 / QK_DIM) + EPS)
        k_ref[hh] = (kh * inv * gqk_ref[...]).astype(BF16)
        v_ref[hh] = v2[:, hh * HP:(hh + 1) * HP].astype(BF16)

    gate_b = seg(S_CB, S_CC)
    z = seg(S_CC, S_CH) * seg(S_CH, S_U)

    @pl.when(t == 0)
    def _():
        zbuf_ref[0:8, :] = jnp.zeros((8, CONV_DIM), F32)

    zbuf_ref[8:8 + tm, :] = z
    z1 = zbuf_ref[7:7 + tm, :]
    z2 = zbuf_ref[6:6 + tm, :]
    wc = wconv_ref[...]
    y = wc[0:1, :] * z2 + wc[1:2, :] * z1 + wc[2:3, :] * z
    zbuf_ref[0:8, :] = z[tm - 8:tm, :]
    tail_ref[...] = z[tm - 8:tm, :]

    u = seg(S_U, S_V)
    vn = _layernorm_rows(seg(S_V, S_G), gvln_ref[...])
    cv_ref[...] = vn[tm - CHUNK:tm, :]
    vnb = vn.astype(BF16)
    rows = lax.broadcasted_iota(jnp.int32, (GM_GROUPS * CHUNK, CHUNK), 0) % CHUNK
    cols = lax.broadcasted_iota(jnp.int32, (GM_GROUPS * CHUNK, CHUNK), 1)
    wsp = jnp.where(cols <= rows, wsp_ref[...], 0.0).astype(BF16)
    lane_grp = lax.broadcasted_iota(jnp.int32, (CHUNK, GM_DIM), 1) // GM_GROUP_DIM
    sgs = []
    for c in range(tm // CHUNK):
        r = _dot(wsp, vnb[c * CHUNK:(c + 1) * CHUNK, :])
        s = bsp_ref[...]
        for g in range(GM_GROUPS):
            s = s + jnp.where(lane_grp == g, r[g * CHUNK:(g + 1) * CHUNK, :], 0.0)
        sgs.append(u[c * CHUNK:(c + 1) * CHUNK, :] * s)
    sg = jnp.concatenate(sgs, axis=0)

    _gates_and_partial(seg, gate_b * y, sg, wbrb_ref, wbrc_ref, bg0_ref, part_ref)


def _inproj_sample_kernel(x_ref, sc_ref, sh_ref, gmix_ref, w2_ref, gq_ref, wuq_ref, gqq_ref,
                          gkv_ref, gqk_ref, wukt_ref, cos_ref, sin_ref, wconv_ref, h0_ref, h1_ref,
                          wbrb_ref, gvln_ref, coef_ref, bias_ref, wbrc_ref,
                          qg_ref, ql_ref, ckv_ref, kr_ref, z_ref, cv_ref, bg0_ref, part_ref):
    seg = _inproj_common(x_ref, sc_ref, sh_ref, gmix_ref, w2_ref)
    cos = cos_ref[...]
    sin = sin_ref[...]
    qs = _heads_q(seg, gq_ref, wuq_ref, gqq_ref, cos, sin)
    for hh in range(N_HEADS):
        qg = qs[hh] * gqk_ref[...]
        qg_ref[hh] = qg
        ql_ref[hh] = _dot(qg.astype(BF16), wukt_ref[hh])
    ckv, krp = _latent_kv(seg, gkv_ref, cos, sin)
    ckv_ref[...] = ckv
    kr_ref[...] = krp

    gate_b = seg(S_CB, S_CC)
    z = seg(S_CC, S_CH) * seg(S_CH, S_U)
    wc = wconv_ref[...]
    y = wc[0:1, :] * h0_ref[...] + wc[1:2, :] * h1_ref[...] + wc[2:3, :] * z
    z_ref[...] = z

    u = seg(S_U, S_V)
    vn = _layernorm_rows(seg(S_V, S_G), gvln_ref[...])
    cv_ref[...] = vn
    sg = u * (vn * coef_ref[...] + bias_ref[...])

    _gates_and_partial(seg, gate_b * y, sg, wbrb_ref, wbrc_ref, bg0_ref, part_ref)


def _inproj_prompt(x, scale_m, shift_m, lw, cos, sin):
    b, t, d = x.shape
    tm = TM_IN
    nt = t // tm
    tok = lambda w: pl.BlockSpec((None, tm, w), lambda i, j: (i, j, 0))
    mod = pl.BlockSpec((None, 1, d), lambda i, j: (i, 0, 0))
    head = pl.BlockSpec((None, N_HEADS, tm, HP), lambda i, j: (i, 0, j, 0))
    in_specs = [
        tok(d), mod, mod, _const_spec((1, d)), _const_spec((d, S_END)),
        _const_spec((1, Q_LORA)), _const_spec((Q_LORA, 2 * N_HEADS * HP)), _const_spec((1, HP)),
        _const_spec((1, KV_LORA)), _const_spec((KV_LORA, N_HEADS * HP)), _const_spec((1, HP)),
        _const_spec((KV_LORA, N_HEADS * HP)),
        pl.BlockSpec((tm, HP), lambda i, j: (j, 0)), pl.BlockSpec((tm, HP), lambda i, j: (j, 0)),
        _const_spec((CONV_W, CONV_DIM)), _const_spec((CONV_DIM, d)), _const_spec((1, GM_DIM)),
        _const_spec((GM_GROUPS * CHUNK, CHUNK)), _const_spec((CHUNK, GM_DIM)), _const_spec((GM_DIM, d)),
    ]
    out_specs = [
        head, head, head, tok(KV_LORA), tok(HP),
        pl.BlockSpec((None, 8, CONV_DIM), lambda i, j: (i, 0, 0)),
        pl.BlockSpec((None, CHUNK, GM_DIM), lambda i, j: (i, 0, 0)),
        tok(d), tok(d),
    ]
    out_shape = [
        jax.ShapeDtypeStruct((b, N_HEADS, t, HP), BF16),
        jax.ShapeDtypeStruct((b, N_HEADS, t, HP), BF16),
        jax.ShapeDtypeStruct((b, N_HEADS, t, HP), BF16),
        jax.ShapeDtypeStruct((b, t, KV_LORA), F32),
        jax.ShapeDtypeStruct((b, t, HP), F32),
        jax.ShapeDtypeStruct((b, 8, CONV_DIM), F32),
        jax.ShapeDtypeStruct((b, CHUNK, GM_DIM), F32),
        jax.ShapeDtypeStruct((b, t, d), BF16),
        jax.ShapeDtypeStruct((b, t, d), F32),
    ]
    return pl.pallas_call(
        functools.partial(_inproj_prompt_kernel, tm=tm),
        grid=(b, nt),
        in_specs=in_specs, out_specs=out_specs, out_shape=out_shape,
        scratch_shapes=[pltpu.VMEM((tm + 8, CONV_DIM), F32)],
        compiler_params=_cparams(("arbitrary", "arbitrary")),
        name="inproj_prompt",
    )(x, scale_m, shift_m, lw["g_norm_mix"], lw["w2"], lw["g_q_lat"], lw["wuq2"], lw["g_qk_q"],
      lw["g_kv_lat"], lw["wuk"], lw["g_qk_k"], lw["wuv"], cos, sin, lw["w_conv"],
      lw["w_br_b"], lw["g_v_ln"], lw["w_sp"], lw["b_sp"], lw["w_br_c"])


def _inproj_sample(x, scale_m, shift_m, lw, cos, sin, hist0, hist1):
    n, d = x.shape
    full = lambda *s: pl.BlockSpec(s, lambda i: (0,) * len(s))
    in_specs = [
        full(n, d), full(n, d), full(n, d), full(1, d), full(d, S_END),
        full(1, Q_LORA), full(Q_LORA, 2 * N_HEADS * HP), full(1, HP),
        full(1, KV_LORA), full(1, HP), full(N_HEADS, HP, KV_LORA), full(1, HP), full(1, HP),
        full(CONV_W, CONV_DIM), full(n, CONV_DIM), full(n, CONV_DIM),
        full(CONV_DIM, d), full(1, GM_DIM), full(1, GM_DIM), full(1, GM_DIM), full(GM_DIM, d),
    ]
    out_specs = [full(N_HEADS, n, HP), full(N_HEADS, n, KV_LORA), full(n, KV_LORA), full(n, HP),
                 full(n, CONV_DIM), full(n, GM_DIM), full(n, d), full(n, d)]
    out_shape = [
        jax.ShapeDtypeStruct((N_HEADS, n, HP), F32),
        jax.ShapeDtypeStruct((N_HEADS, n, KV_LORA), F32),
        jax.ShapeDtypeStruct((n, KV_LORA), F32),
        jax.ShapeDtypeStruct((n, HP), F32),
        jax.ShapeDtypeStruct((n, CONV_DIM), F32),
        jax.ShapeDtypeStruct((n, GM_DIM), F32),
        jax.ShapeDtypeStruct((n, d), BF16),
        jax.ShapeDtypeStruct((n, d), F32),
    ]
    return pl.pallas_call(
        _inproj_sample_kernel,
        grid=(1,),
        in_specs=in_specs, out_specs=out_specs, out_shape=out_shape,
        compiler_params=_cparams(("arbitrary",)),
        name="inproj_sample",
    )(x, scale_m, shift_m, lw["g_norm_mix"], lw["w2"], lw["g_q_lat"], lw["wuq2"], lw["g_qk_q"],
      lw["g_kv_lat"], lw["g_qk_k"], lw["wukt_pad"], cos, sin, lw["w_conv"], hist0, hist1,
      lw["w_br_b"], lw["g_v_ln"], lw["sp_coef"], lw["b_sp"][0:1], lw["w_br_c"])


def _flash_kernel(q_ref, k_ref, v_ref, o_ref, *, tq, tk):
    assert tq == tk
    qi = pl.program_id(2)
    q = q_ref[...]

    def step(j, carry, masked):
        m, l, acc = carry
        start = pl.multiple_of(j * tk, tk)
        k = k_ref[pl.ds(start, tk), :]
        v = v_ref[pl.ds(start, tk), :]
        s = lax.dot_general(q, k, (((1,), (1,)), ((), ())), preferred_element_type=F32)
        if masked:
            row = lax.broadcasted_iota(jnp.int32, (tq, tk), 0)
            col = lax.broadcasted_iota(jnp.int32, (tq, tk), 1)
            s = jnp.where(col <= row, s, -jnp.inf)
        m_new = jnp.maximum(m, jnp.max(s, axis=-1, keepdims=True))
        alpha = jnp.exp(m - m_new)
        p = jnp.exp(s - m_new)
        l = alpha * l + jnp.sum(p, axis=-1, keepdims=True)
        acc = alpha * acc + _dot(p.astype(BF16), v)
        return m_new, l, acc

    m0 = jnp.full((tq, 1), -jnp.inf, F32)
    l0 = jnp.zeros((tq, 1), F32)
    a0 = jnp.zeros((tq, HP), F32)
    carry = lax.fori_loop(0, qi, lambda j, c: step(j, c, False), (m0, l0, a0))
    m, l, acc = step(qi, carry, True)
    o_ref[...] = (acc / l).astype(BF16)


def _flash_attention(q, k, v):
    b, h, t, _ = q.shape
    tq, tk = TQ, TK
    return pl.pallas_call(
        functools.partial(_flash_kernel, tq=tq, tk=tk),
        grid=(b, h, t // tq),
        in_specs=[pl.BlockSpec((None, None, tq, HP), lambda i, j, n: (i, j, n, 0)),
                  pl.BlockSpec((None, None, t, HP), lambda i, j, n: (i, j, 0, 0)),
                  pl.BlockSpec((None, None, t, HP), lambda i, j, n: (i, j, 0, 0))],
        out_specs=pl.BlockSpec((None, tq, HP), lambda i, j, n: (i, n, j)),
        out_shape=jax.ShapeDtypeStruct((b, t, h * HP), BF16),
        compiler_params=_cparams(("arbitrary", "arbitrary", "arbitrary")),
        name="flash_prompt",
    )(q, k, v)


def _paged_kernel(pt_ref, wkt_ref, ql_ref, qr_ref, cnew_ref, krnew_ref, poolc_ref, poolk_ref, o_ref,
                  cbuf, kbuf, csem, ksem, lhs_ref, m_ref, l_ref, acc_ref, *, layer, n_pages, n_pg, n_chunk):
    i = pl.program_id(0)
    j = pl.program_id(1)
    nj = pl.num_programs(1)
    nk = N_HEADS * QK_NOPE
    step = i * nj + j
    slot = step % 2

    def page_copies(base, buf_slot, p):
        pid = pt_ref[base + p]
        return (pltpu.make_async_copy(poolc_ref.at[layer, pid], cbuf.at[buf_slot, p], csem.at[buf_slot]),
                pltpu.make_async_copy(poolk_ref.at[layer, pid], kbuf.at[buf_slot, p], ksem.at[buf_slot]))

    def start_step(t, buf_slot):
        base = (t // nj) * n_pages + (t % nj) * n_pg
        for p in range(n_pg):
            for cp in page_copies(base, buf_slot, p):
                cp.start()

    @pl.when(step == 0)
    def _():
        start_step(step, slot)

    @pl.when(step + 1 < pl.num_programs(0) * nj)
    def _():
        start_step(step + 1, 1 - slot)

    for p in range(n_pg):
        for cp in page_copies(i * n_pages + j * n_pg, slot, p):
            cp.wait()

    @pl.when(j == 0)
    def _():
        lhs_ref[0:nk, :] = wkt_ref[...]
        lhs_ref[nk:nk + HEAD_ROWS, :] = ql_ref[...]
        m_ref[...] = jnp.full(m_ref.shape, -jnp.inf, F32)
        l_ref[...] = jnp.zeros(l_ref.shape, F32)
        acc_ref[...] = jnp.zeros(acc_ref.shape, F32)

    qr = qr_ref[...]

    def update(cb, krt, valid_keys):
        kx = lax.dot_general(lhs_ref[...], cb, (((1,), (1,)), ((), ())), preferred_element_type=F32)
        sq = kx[:nk, :] * kx[:nk, :]
        ssq = jnp.concatenate([jnp.sum(sq[hh * QK_NOPE:(hh + 1) * QK_NOPE, :], axis=0, keepdims=True)
                               for hh in range(N_HEADS)], axis=0)
        ssq = ssq + jnp.sum(krt * krt, axis=0, keepdims=True)
        inv = lax.rsqrt(ssq * (1.0 / QK_DIM) + EPS)
        st = (kx[nk:, :] + _dot(qr, krt.astype(BF16))) * jnp.concatenate([inv, inv], axis=0)
        if valid_keys is not None:
            kcol = lax.broadcasted_iota(jnp.int32, st.shape, 1)
            st = jnp.where(kcol < valid_keys, st, -jnp.inf)
        m_old = m_ref[...]
        m_new = jnp.maximum(m_old, jnp.max(st, axis=-1, keepdims=True))
        alpha = jnp.exp(m_old - m_new)
        p = jnp.exp(st - m_new)
        l_ref[...] = alpha * l_ref[...] + jnp.sum(p, axis=-1, keepdims=True)
        acc_ref[...] = alpha * acc_ref[...] + _dot(p.astype(BF16), cb)
        m_ref[...] = m_new

    per = n_pg // n_chunk
    for ch in range(n_chunk):
        cb = cbuf[slot, ch * per:(ch + 1) * per].reshape(per * PAGE_SIZE, KV_LORA).astype(BF16)
        krt = jnp.concatenate([kbuf[slot, p] for p in range(ch * per, (ch + 1) * per)], axis=1)
        update(cb, krt, None)

    @pl.when(j == nj - 1)
    def _():
        first = lax.broadcasted_iota(jnp.int32, (LANES, KV_LORA), 0) == 0
        cn = jnp.where(first, jnp.broadcast_to(cnew_ref[...], (LANES, KV_LORA)), 0.0).astype(BF16)
        update(cn, krnew_ref[...], 1)
        o_ref[...] = (acc_ref[...] / l_ref[...])[:N_HEADS, :]


def _paged_attention(layer, page_table, wkt, qlat, qr, cnew, krnew_t, pool_c, pool_krt):
    s, n_pages = page_table.shape
    n_pg = PAGES_PER_STEP
    nj = n_pages // n_pg
    pt = page_table.reshape(-1)
    nk = N_HEADS * QK_NOPE

    per_seq = lambda *shape: pl.BlockSpec((None,) + shape, lambda i, j, pt_ref: (i,) + (0,) * len(shape))
    const = lambda *shape: pl.BlockSpec(shape, lambda i, j, pt_ref: (0,) * len(shape))
    in_specs = [const(nk, KV_LORA), per_seq(HEAD_ROWS, KV_LORA), per_seq(HEAD_ROWS, QK_ROPE),
                per_seq(1, KV_LORA), per_seq(QK_ROPE, LANES),
                pl.BlockSpec(memory_space=pl.ANY), pl.BlockSpec(memory_space=pl.ANY)]
    grid_spec = pltpu.PrefetchScalarGridSpec(
        num_scalar_prefetch=1,
        grid=(s, nj),
        in_specs=in_specs,
        out_specs=per_seq(N_HEADS, KV_LORA),
        scratch_shapes=[pltpu.VMEM((2, n_pg, PAGE_SIZE, KV_LORA), F32),
                        pltpu.VMEM((2, n_pg, QK_ROPE, PAGE_SIZE), F32),
                        pltpu.SemaphoreType.DMA((2,)), pltpu.SemaphoreType.DMA((2,)),
                        pltpu.VMEM((nk + HEAD_ROWS, KV_LORA), BF16), pltpu.VMEM((HEAD_ROWS, 1), F32),
                        pltpu.VMEM((HEAD_ROWS, 1), F32), pltpu.VMEM((HEAD_ROWS, KV_LORA), F32)],
    )
    return pl.pallas_call(
        functools.partial(_paged_kernel, layer=layer, n_pages=n_pages, n_pg=n_pg, n_chunk=PAGE_CHUNKS),
        grid_spec=grid_spec,
        out_shape=jax.ShapeDtypeStruct((s, N_HEADS, KV_LORA), F32),
        compiler_params=_cparams(("arbitrary", "arbitrary")),
        name="paged_sample",
    )(pt, wkt, qlat, qr, cnew, krnew_t, pool_c, pool_krt)


def _uv_kernel(ol_ref, wuv_ref, o_ref):
    for hh in range(N_HEADS):
        ol = ol_ref[:, hh * KV_LORA:(hh + 1) * KV_LORA].astype(BF16)
        o_ref[:, hh * HP:(hh + 1) * HP] = _dot(ol, wuv_ref[:, hh * HP:(hh + 1) * HP]).astype(BF16)


def _uv_project(olat, wuv):
    n = olat.shape[0]
    full = lambda *s: pl.BlockSpec(s, lambda i: (0,) * len(s))
    return pl.pallas_call(
        _uv_kernel, grid=(1,),
        in_specs=[full(n, N_HEADS * KV_LORA), full(KV_LORA, N_HEADS * HP)],
        out_specs=full(n, N_HEADS * HP),
        out_shape=jax.ShapeDtypeStruct((n, N_HEADS * HP), BF16),
        compiler_params=_cparams(("arbitrary",)),
        name="uv_sample",
    )(olat, wuv)


def _merge_kernel(oa_ref, wbra_ref, bg0_ref, part_ref, x_ref, gm_ref, wout_ref, gffn_ref, scf_ref, shf_ref,
                  x1_ref, h2_ref):
    bra = _dot(oa_ref[...], wbra_ref[...])
    merged = bg0_ref[...].astype(F32) * bra + part_ref[...]
    x1 = x_ref[...] + gm_ref[...] * _dot(merged.astype(BF16), wout_ref[...])
    x1_ref[...] = x1
    h2_ref[...] = (_rms_rows(x1) * gffn_ref[...] * (1.0 + scf_ref[...]) + shf_ref[...]).astype(BF16)


def _merge(oa, bg0, part, x, gate_m, scale_f, shift_f, lw, tm):
    g, t, d = x.shape
    mt = gate_m.shape[1]
    tok = lambda w: pl.BlockSpec((None, tm, w), lambda i, j: (i, j, 0))
    if mt == 1:
        mod = pl.BlockSpec((None, 1, d), lambda i, j: (i, 0, 0))
    else:
        mod = pl.BlockSpec((None, tm, d), lambda i, j: (i, j, 0))
    return pl.pallas_call(
        _merge_kernel,
        grid=(g, t // tm),
        in_specs=[tok(N_HEADS * HP), _const_spec((N_HEADS * HP, d)), tok(d), tok(d), tok(d), mod,
                  _const_spec((d, d)), _const_spec((1, d)), mod, mod],
        out_specs=[tok(d), tok(d)],
        out_shape=[jax.ShapeDtypeStruct((g, t, d), F32), jax.ShapeDtypeStruct((g, t, d), BF16)],
        compiler_params=_cparams(("arbitrary", "arbitrary")),
        name="merge",
    )(oa, lw["w_br_a"], bg0, part, x, gate_m, lw["w_out"], lw["g_norm_ffn"], scale_f, shift_f)


def _router_kernel(h_ref, wrt_ref, brt_ref, g_ref):
    tm = h_ref.shape[0]
    logits = lax.dot_general(wrt_ref[...], h_ref[...], (((1,), (1,)), ((), ())), preferred_element_type=F32)
    scores = jax.nn.sigmoid(logits)
    choice = scores + brt_ref[...]
    neg = -jnp.inf

    gsc = []
    for g in range(N_EXPERT_GROUPS):
        xg = choice[g * GROUP_SIZE:(g + 1) * GROUP_SIZE, :]
        m1 = jnp.max(xg, axis=0, keepdims=True)
        cnt = jnp.sum(jnp.where(xg == m1, 1.0, 0.0), axis=0, keepdims=True)
        m2 = jnp.max(jnp.where(xg < m1, xg, neg), axis=0, keepdims=True)
        gsc.append(m1 + jnp.where(cnt >= 2.0, m1, m2))

    cands = []
    for g in range(N_EXPERT_GROUPS):
        rank = jnp.zeros((1, tm), F32)
        for o in range(N_EXPERT_GROUPS):
            if o == g:
                continue
            beats = (gsc[o] > gsc[g]) if o > g else (gsc[o] >= gsc[g])
            rank = rank + jnp.where(beats, 1.0, 0.0)
        keep = rank < float(TOPK_GROUPS)
        cands.append(jnp.where(keep, choice[g * GROUP_SIZE:(g + 1) * GROUP_SIZE, :], neg))
    cand = jnp.concatenate(cands, axis=0)

    ridx = lax.broadcasted_iota(jnp.int32, (N_EXPERTS, tm), 0)
    picked = jnp.zeros((N_EXPERTS, tm), F32)
    for _ in range(TOP_K):
        m = jnp.max(cand, axis=0, keepdims=True)
        idx = jnp.min(jnp.where(cand == m, ridx, N_EXPERTS), axis=0, keepdims=True)
        hit = ridx == idx
        picked = jnp.where(hit, 1.0, picked)
        cand = jnp.where(hit, neg, cand)

    w = jnp.where(picked > 0.0, scores, 0.0)
    gt = w / jnp.sum(w, axis=0, keepdims=True) * ROUTED_SCALE
    g_ref[...] = jnp.concatenate([gt, jnp.zeros((LANES - N_EXPERTS, tm), F32)], axis=0).T


def _router(h2, lw, tm):
    m, d = h2.shape
    return pl.pallas_call(
        _router_kernel,
        grid=(m // tm,),
        in_specs=[pl.BlockSpec((tm, d), lambda i: (i, 0)), _const_spec((N_EXPERTS, d)),
                  _const_spec((N_EXPERTS, 1))],
        out_specs=pl.BlockSpec((tm, LANES), lambda i: (i, 0)),
        out_shape=jax.ShapeDtypeStruct((m, LANES), F32),
        compiler_params=_cparams(("arbitrary",)),
        name="router",
    )(h2, lw["w_router_t"], lw["b_router_t"])


def _moe_kernel(h_ref, g_ref, wg_ref, wu_ref, wd_ref, wsg_ref, wsu_ref, wsd_ref, x1_ref, gf_ref,
                o_ref, acc_ref, *, epb):
    e = pl.program_id(2)
    h = h_ref[...]
    tm = h.shape[0]

    @pl.when(e == 0)
    def _():
        hs = (_silu(_dot(h, wsg_ref[...])) * _dot(h, wsu_ref[...])).astype(BF16)
        acc_ref[...] = _dot(hs, wsd_ref[...])

    lane = lax.broadcasted_iota(jnp.int32, (tm, LANES), 1)
    gates = g_ref[...]
    for jj in range(epb):
        a = _dot(h, wg_ref[jj])
        b = _dot(h, wu_ref[jj])
        gcol = jnp.sum(jnp.where(lane == e * epb + jj, gates, 0.0), axis=-1, keepdims=True)
        hh = (_silu(a) * b * gcol).astype(BF16)
        acc_ref[...] += _dot(hh, wd_ref[jj])

    @pl.when(e == pl.num_programs(2) - 1)
    def _():
        o_ref[...] = x1_ref[...] + gf_ref[...] * acc_ref[...]


def _moe(h2, gates, x1, gate_f, lw, tm):
    g, t, d = x1.shape
    mt = gate_f.shape[1]
    epb = EXPERTS_PER_STEP
    tok = lambda w: pl.BlockSpec((None, tm, w), lambda i, j, e: (i, j, 0))
    if mt == 1:
        mod = pl.BlockSpec((None, 1, d), lambda i, j, e: (i, 0, 0))
    else:
        mod = pl.BlockSpec((None, tm, d), lambda i, j, e: (i, j, 0))
    return pl.pallas_call(
        functools.partial(_moe_kernel, epb=epb),
        grid=(g, t // tm, N_EXPERTS // epb),
        in_specs=[tok(d), tok(LANES),
                  pl.BlockSpec((epb, d, D_EXPERT), lambda i, j, e: (e, 0, 0)),
                  pl.BlockSpec((epb, d, D_EXPERT), lambda i, j, e: (e, 0, 0)),
                  pl.BlockSpec((epb, D_EXPERT, d), lambda i, j, e: (e, 0, 0)),
                  _const_spec((d, D_EXPERT)), _const_spec((d, D_EXPERT)), _const_spec((D_EXPERT, d)),
                  tok(d), mod],
        out_specs=tok(d),
        out_shape=jax.ShapeDtypeStruct((g, t, d), F32),
        scratch_shapes=[pltpu.VMEM((tm, d), F32)],
        compiler_params=_cparams(("arbitrary", "arbitrary", "arbitrary")),
        name="moe",
    )(h2, gates, lw["w_e_gate"], lw["w_e_up"], lw["w_e_down"], lw["w_sh_gate"], lw["w_sh_up"],
      lw["w_sh_down"], x1, gate_f)


def _pad_heads(w, width):
    pad = [(0, 0)] * (w.ndim - 1) + [(0, HP - width)]
    wp = jnp.pad(w, pad)
    return wp.reshape(w.shape[:-2] + (w.shape[-2] * HP,))


def _rot_cols(w):
    half = QK_ROPE // 2
    return jnp.concatenate([-w[..., half:], w[..., :half]], axis=-1)


def _prep_layer(l, p):
    d = D_MODEL
    w_in = p["w_in"][l]
    off_kv = Q_LORA
    off_kr = off_kv + KV_LORA
    off_cb = off_kr + QK_ROPE
    off_g = off_cb + 3 * CONV_DIM + 2 * GM_DIM
    wkr = w_in[:, off_kr:off_cb]
    lead = jnp.zeros((d, QK_NOPE), F32)
    trail = jnp.zeros((d, HP - QK_DIM), F32)
    w2 = jnp.concatenate([
        w_in[:, :off_kr],
        lead, wkr, trail,
        lead, _rot_cols(wkr), trail,
        w_in[:, off_cb:],
    ], axis=1).astype(BF16)
    assert w2.shape[1] == S_END and off_g + N_BRANCH * d == w_in.shape[1]

    wuq = p["w_uq"][l]
    wuq_rot = jnp.concatenate([jnp.zeros_like(wuq[..., :QK_NOPE]), _rot_cols(wuq[..., QK_NOPE:])], axis=-1)
    wuq2 = jnp.concatenate([_pad_heads(wuq, QK_DIM), _pad_heads(wuq_rot, QK_DIM)], axis=-1).astype(BF16)

    def pad_gain(gq):
        return jnp.pad(gq, (0, HP - QK_DIM)).reshape(1, HP)

    w_br_a = jnp.pad(p["w_br_a"][l].reshape(N_HEADS, V_DIM, d), ((0, 0), (0, HP - V_DIM), (0, 0)))
    w_sp = p["w_spatial"][l]
    b_sp = jnp.repeat(p["b_spatial"][l].T, GM_GROUP_DIM, axis=1)
    return dict(
        w_ada=p["w_ada"][l], b_ada=p["b_ada"][l],
        g_norm_mix=p["g_norm_mix"][l].reshape(1, d), g_norm_ffn=p["g_norm_ffn"][l].reshape(1, d),
        w2=w2, g_q_lat=p["g_q_lat"][l].reshape(1, Q_LORA), wuq2=wuq2,
        g_qk_q=pad_gain(p["g_qk_q"][l]), g_qk_k=pad_gain(p["g_qk_k"][l]),
        g_kv_lat=p["g_kv_lat"][l].reshape(1, KV_LORA),
        wuk=_pad_heads(p["w_uk"][l], QK_NOPE).astype(BF16),
        wkt=p["w_uk"][l].reshape(KV_LORA, N_HEADS * QK_NOPE).T.astype(BF16),
        wukt_pad=jnp.pad(p["w_uk"][l].transpose(1, 2, 0), ((0, 0), (0, HP - QK_NOPE), (0, 0))).astype(BF16),
        wuv=_pad_heads(p["w_uv"][l], V_DIM).astype(BF16),
        w_br_a=w_br_a.reshape(N_HEADS * HP, d).astype(BF16),
        w_conv=p["w_conv"][l], w_br_b=p["w_br_b"][l].astype(BF16),
        g_v_ln=p["g_v_ln"][l].reshape(1, GM_DIM),
        w_sp=w_sp.reshape(GM_GROUPS * CHUNK, CHUNK), b_sp=b_sp,
        sp_coef=jnp.repeat(w_sp[:, 0, 0], GM_GROUP_DIM).reshape(1, GM_DIM),
        w_br_c=p["w_br_c"][l].astype(BF16), w_out=p["w_out"][l].astype(BF16),
        w_router_t=p["w_router"][l].T.astype(BF16),
        b_router_t=p["b_router"][l].reshape(N_EXPERTS, 1),
        w_e_gate=p["w_e_gate"][l].astype(BF16), w_e_up=p["w_e_up"][l].astype(BF16),
        w_e_down=p["w_e_down"][l].astype(BF16),
        w_sh_gate=p["w_sh_gate"][l].astype(BF16), w_sh_up=p["w_sh_up"][l].astype(BF16),
        w_sh_down=p["w_sh_down"][l].astype(BF16),
    )


def _rope_tables(pos):
    inv_freq = ROPE_THETA ** (-jnp.arange(0, QK_ROPE, 2, dtype=F32) / QK_ROPE)
    ang = pos.astype(F32)[:, None] * inv_freq[None, :]
    c, s = jnp.cos(ang), jnp.sin(ang)
    n = pos.shape[0]
    cos = jnp.concatenate([jnp.ones((n, QK_NOPE), F32), c, c, jnp.ones((n, HP - QK_DIM), F32)], axis=1)
    sin = jnp.concatenate([jnp.zeros((n, QK_NOPE), F32), s, s, jnp.zeros((n, HP - QK_DIM), F32)], axis=1)
    return cos, sin


def _split_mod(m):
    return [m[:, i * D_MODEL:(i + 1) * D_MODEL] for i in range(6)]


def _prompt_layer(x, mod, lw, cos, sin):
    b, t, d = x.shape
    shift_m, scale_m, gate_m, shift_f, scale_f, gate_f = [a.reshape(b, 1, d) for a in _split_mod(mod)]
    q, k, v, ckv, krp, tail, cv, bg0, part = _inproj_prompt(x, scale_m, shift_m, lw, cos, sin)
    oa = _flash_attention(q, k, v)
    x1, h2 = _merge(oa, bg0, part, x, gate_m, scale_f, shift_f, lw, TM_MERGE)
    gates = _router(h2.reshape(b * t, d), lw, TM_ROUTE).reshape(b, t, LANES)
    y = _moe(h2, gates, x1, gate_f, lw, TM_MOE)
    state = (ckv, krp[:, :, QK_NOPE:QK_DIM], tail[:, 8 - (CONV_W - 1):], cv)
    return y, state


def _sample_layer(layer, x, mod, lw, cos, sin, hist, pool_c, pool_krt, page_table):
    n, d = x.shape
    shift_m, scale_m, gate_m, shift_f, scale_f, gate_f = _split_mod(mod)
    qg, ql, ckv, krp, z, cv, bg0, part = _inproj_sample(x, scale_m, shift_m, lw, cos, sin, hist[:, 0], hist[:, 1])
    head_pad = ((0, 0), (0, HEAD_ROWS - N_HEADS), (0, 0))
    qlat = jnp.pad(ql.transpose(1, 0, 2), head_pad).astype(BF16)
    qr = jnp.pad(qg[:, :, QK_NOPE:QK_DIM].transpose(1, 0, 2), head_pad).astype(BF16)
    kr = krp[:, QK_NOPE:QK_DIM]
    krnew_t = jnp.pad(kr[:, :, None], ((0, 0), (0, 0), (0, LANES - 1)))
    olat = _paged_attention(layer, page_table, lw["wkt"], qlat, qr, ckv.reshape(n, 1, KV_LORA), krnew_t,
                            pool_c, pool_krt)
    oa = _uv_project(olat.reshape(n, N_HEADS * KV_LORA), lw["wuv"])
    g3 = lambda a: a.reshape(1, n, -1)
    x1, h2 = _merge(g3(oa), g3(bg0), g3(part), g3(x), g3(gate_m), g3(scale_f), g3(shift_f), lw, n)
    gates = _router(h2.reshape(n, d), lw, n)
    y = _moe(h2, g3(gates), x1, g3(gate_f), lw, n).reshape(n, d)
    state = (ckv.reshape(n, 1, KV_LORA), kr.reshape(n, 1, QK_ROPE),
             jnp.stack([hist[:, 1], z], axis=1), cv.reshape(n, 1, GM_DIM))
    return y, state


def kernel(x_prompt, x_sample, cache_kv_latent, cache_k_rope, state_conv, page_table, c_prompt, c_sample,
           w_ada, b_ada, g_norm_mix, g_norm_ffn, w_in, g_q_lat, w_uq, g_kv_lat, w_uk, w_uv, g_qk_q, g_qk_k,
           w_br_a, w_conv, w_br_b, g_v_ln, w_spatial, b_spatial, w_br_c, w_out, w_router, b_router,
           w_e_gate, w_e_up, w_e_down, w_sh_gate, w_sh_up, w_sh_down):
    params = dict(w_ada=w_ada, b_ada=b_ada, g_norm_mix=g_norm_mix, g_norm_ffn=g_norm_ffn, w_in=w_in,
                  g_q_lat=g_q_lat, w_uq=w_uq, g_kv_lat=g_kv_lat, w_uk=w_uk, w_uv=w_uv, g_qk_q=g_qk_q,
                  g_qk_k=g_qk_k, w_br_a=w_br_a, w_conv=w_conv, w_br_b=w_br_b, g_v_ln=g_v_ln,
                  w_spatial=w_spatial, b_spatial=b_spatial, w_br_c=w_br_c, w_out=w_out, w_router=w_router,
                  b_router=b_router, w_e_gate=w_e_gate, w_e_up=w_e_up, w_e_down=w_e_down,
                  w_sh_gate=w_sh_gate, w_sh_up=w_sh_up, w_sh_down=w_sh_down)
    depth = w_in.shape[0]
    nb, t, d = x_prompt.shape
    ns = x_sample.shape[0]
    assert x_sample.shape[1] == 1 and t % TQ == 0 and t % TM_MOE == 0
    past_len = page_table.shape[1] * PAGE_SIZE

    cos_p, sin_p = _rope_tables(jnp.arange(t))
    cos_s, sin_s = _rope_tables(past_len + jnp.arange(1))
    c_all = jnp.concatenate([c_prompt, c_sample], axis=0)
    cache_krt = jnp.swapaxes(cache_k_rope, 2, 3)

    hp, hs = x_prompt, x_sample.reshape(ns, d)
    st_p, st_s = [], []
    for l in range(depth):
        lw = _prep_layer(l, params)
        mod = _adaln(c_all, lw["w_ada"], lw["b_ada"])
        hp, sp = _prompt_layer(hp, mod[:nb], lw, cos_p, sin_p)
        hs, ss = _sample_layer(l, hs, mod[nb:], lw, cos_s, sin_s, state_conv[l], cache_kv_latent, cache_krt,
                               page_table)
        st_p.append(sp)
        st_s.append(ss)

    stack = lambda sts, i: jnp.stack([s[i] for s in sts])
    return (hp, hs.reshape(ns, 1, d),
            stack(st_p, 0), stack(st_p, 1), stack(st_p, 2), stack(st_p, 3),
            stack(st_s, 0), stack(st_s, 1), stack(st_s, 2), stack(st_s, 3))
```

```python
import functools

import jax
import jax.numpy as jnp
from jax import lax
from jax.experimental import pallas as pl
from jax.experimental.pallas import tpu as pltpu

F32 = jnp.float32
BF16 = jnp.bfloat16

D_MODEL = 1024
N_HEADS = 8
QK_NOPE = 64
QK_ROPE = 32
QK_DIM = QK_NOPE + QK_ROPE
V_DIM = 64
Q_LORA = 384
KV_LORA = 256
ROPE_THETA = 10000.0
CONV_W = 3
CONV_DIM = 512
CHUNK = 128
GM_GROUPS = 8
GM_DIM = 512
GM_GROUP_DIM = GM_DIM // GM_GROUPS
N_EXPERTS = 64
TOP_K = 8
N_EXPERT_GROUPS = 8
GROUP_SIZE = N_EXPERTS // N_EXPERT_GROUPS
TOPK_GROUPS = 4
D_EXPERT = 256
ROUTED_SCALE = 2.5
N_BRANCH = 3
EPS = 1e-6
PAGE_SIZE = 128

LANES = 128
HP = LANES
VMEM_LIMIT = 56 * 1024 * 1024

S_Q = 0
S_KV = S_Q + Q_LORA
S_KRM = S_KV + KV_LORA
S_KRR = S_KRM + HP
S_CB = S_KRR + HP
S_CC = S_CB + CONV_DIM
S_CH = S_CC + CONV_DIM
S_U = S_CH + CONV_DIM
S_V = S_U + GM_DIM
S_G = S_V + GM_DIM
S_END = S_G + N_BRANCH * D_MODEL

TM_IN = 512
TQ = 512
TK = 512
TM_MERGE = 512
TM_ROUTE = 512
TM_MOE = 1024
EXPERTS_PER_STEP = 2
MOE_BLOCK = 256
MOE_CHUNK = 16
EXP_CHUNKS = 16
MOE_ROWS = -(-(MOE_BLOCK * TOP_K + N_EXPERTS * (MOE_CHUNK - 1)) // 512) * 512
PAGES_PER_STEP = 32
PAGE_CHUNKS = 2
HEAD_ROWS = 16


def _cparams(sem):
    return pltpu.CompilerParams(dimension_semantics=sem, vmem_limit_bytes=VMEM_LIMIT)


def _const_spec(shape):
    nd = len(shape)
    return pl.BlockSpec(shape, lambda *_: (0,) * nd, pipeline_mode=pl.Buffered(1))


def _dot(a, b):
    return jnp.dot(a, b, preferred_element_type=F32)


def _rms_rows(x):
    return x * lax.rsqrt(jnp.mean(x * x, axis=-1, keepdims=True) + EPS)


def _silu(x):
    return x * jax.nn.sigmoid(x)


def _adaln_kernel(c_ref, w_ref, b_ref, o_ref):
    s = _silu(c_ref[...]).astype(BF16)
    o_ref[...] = _dot(s, w_ref[...].astype(BF16)) + b_ref[...]


def _adaln(c_all, w_ada, b_ada):
    n, d = c_all.shape
    e = w_ada.shape[1]
    tn = 1536
    return pl.pallas_call(
        _adaln_kernel,
        grid=(e // tn,),
        in_specs=[pl.BlockSpec((n, d), lambda j: (0, 0)),
                  pl.BlockSpec((d, tn), lambda j: (0, j)),
                  pl.BlockSpec((1, tn), lambda j: (0, j))],
        out_specs=pl.BlockSpec((n, tn), lambda j: (0, j)),
        out_shape=jax.ShapeDtypeStruct((n, e), F32),
        compiler_params=_cparams(("arbitrary",)),
        name="adaln",
    )(c_all, w_ada, b_ada.reshape(1, e))


def _inproj_common(x_ref, sc_ref, sh_ref, gmix_ref, w2_ref):
    h = (_rms_rows(x_ref[...]) * gmix_ref[...] * (1.0 + sc_ref[...]) + sh_ref[...]).astype(BF16)

    def seg(a, b):
        return _dot(h, w2_ref[:, a:b])

    return seg


def _heads_q(seg, gq_ref, wuq_ref, gqq_ref, cos, sin):
    cqn = (_rms_rows(seg(S_Q, S_KV)) * gq_ref[...]).astype(BF16)
    q2 = _dot(cqn, wuq_ref[...])
    out = []
    for hh in range(N_HEADS):
        qm = q2[:, hh * HP:(hh + 1) * HP]
        qr = q2[:, (N_HEADS + hh) * HP:(N_HEADS + hh + 1) * HP]
        qh = qm * cos + qr * sin
        inv = lax.rsqrt(jnp.sum(qh * qh, axis=-1, keepdims=True) * (1.0 / QK_DIM) + EPS)
        out.append(qh * inv * gqq_ref[...] * (QK_DIM ** -0.5))
    return out


def _latent_kv(seg, gkv_ref, cos, sin):
    ckv = _rms_rows(seg(S_KV, S_KRM)) * gkv_ref[...]
    krp = seg(S_KRM, S_KRR) * cos + seg(S_KRR, S_CB) * sin
    return ckv, krp


def _gates_and_partial(seg, brb_in, brc_in, wbrb_ref, wbrc_ref, bg0_ref, part_ref):
    brb = _dot(brb_in.astype(BF16), wbrb_ref[...])
    brc = _dot(brc_in.astype(BF16), wbrc_ref[...])
    bg = jax.nn.sigmoid(seg(S_G, S_END))
    bg0_ref[...] = bg[:, :D_MODEL].astype(BF16)
    part_ref[...] = bg[:, D_MODEL:2 * D_MODEL] * brb + bg[:, 2 * D_MODEL:] * brc


def _layernorm_rows(v, g):
    xc = v - jnp.mean(v, axis=-1, keepdims=True)
    return xc * lax.rsqrt(jnp.mean(xc * xc, axis=-1, keepdims=True) + EPS) * g


def _inproj_prompt_kernel(x_ref, sc_ref, sh_ref, gmix_ref, w2_ref, gq_ref, wuq_ref, gqq_ref,
                          gkv_ref, wuk_ref, gqk_ref, wuv_ref, cos_ref, sin_ref, wconv_ref,
                          wbrb_ref, gvln_ref, wsp_ref, bsp_ref, wbrc_ref,
                          q_ref, k_ref, v_ref, ckv_ref, kr_ref, tail_ref, cv_ref, bg0_ref, part_ref,
                          zbuf_ref, *, tm):
    t = pl.program_id(1)
    seg = _inproj_common(x_ref, sc_ref, sh_ref, gmix_ref, w2_ref)
    cos = cos_ref[...]
    sin = sin_ref[...]

    qs = _heads_q(seg, gq_ref, wuq_ref, gqq_ref, cos, sin)
    for hh in range(N_HEADS):
        q_ref[hh] = qs[hh].astype(BF16)
    ckv, krp = _latent_kv(seg, gkv_ref, cos, sin)
    ckv_ref[...] = ckv
    kr_ref[...] = krp
    ckv_b = ckv.astype(BF16)
    k2 = _dot(ckv_b, wuk_ref[...])
    v2 = _dot(ckv_b, wuv_ref[...])
    for hh in range(N_HEADS):
        kh = k2[:, hh * HP:(hh + 1) * HP] + krp
        inv = lax.rsqrt(jnp.sum(kh * kh, axis=-1, keepdims=True) * (1.0 / QK_DIM) + EPS)
        k_ref[hh] = (kh * inv * gqk_ref[...]).astype(BF16)
        v_ref[hh] = v2[:, hh * HP:(hh + 1) * HP].astype(BF16)

    gate_b = seg(S_CB, S_CC)
    z = seg(S_CC, S_CH) * seg(S_CH, S_U)

    @pl.when(t == 0)
    def _():
        zbuf_ref[0:8, :] = jnp.zeros((8, CONV_DIM), F32)

    zbuf_ref[8:8 + tm, :] = z
    z1 = zbuf_ref[7:7 + tm, :]
    z2 = zbuf_ref[6:6 + tm, :]
    wc = wconv_ref[...]
    y = wc[0:1, :] * z2 + wc[1:2, :] * z1 + wc[2:3, :] * z
    zbuf_ref[0:8, :] = z[tm - 8:tm, :]
    tail_ref[...] = z[tm - 8:tm, :]

    u = seg(S_U, S_V)
    vn = _layernorm_rows(seg(S_V, S_G), gvln_ref[...])
    cv_ref[...] = vn[tm - CHUNK:tm, :]
    vnb = vn.astype(BF16)
    rows = lax.broadcasted_iota(jnp.int32, (GM_GROUPS * CHUNK, CHUNK), 0) % CHUNK
    cols = lax.broadcasted_iota(jnp.int32, (GM_GROUPS * CHUNK, CHUNK), 1)
    wsp = jnp.where(cols <= rows, wsp_ref[...], 0.0).astype(BF16)
    lane_grp = lax.broadcasted_iota(jnp.int32, (CHUNK, GM_DIM), 1) // GM_GROUP_DIM
    sgs = []
    for c in range(tm // CHUNK):
        r = _dot(wsp, vnb[c * CHUNK:(c + 1) * CHUNK, :])
        s = bsp_ref[...]
        for g in range(GM_GROUPS):
            s = s + jnp.where(lane_grp == g, r[g * CHUNK:(g + 1) * CHUNK, :], 0.0)
        sgs.append(u[c * CHUNK:(c + 1) * CHUNK, :] * s)
    sg = jnp.concatenate(sgs, axis=0)

    _gates_and_partial(seg, gate_b * y, sg, wbrb_ref, wbrc_ref, bg0_ref, part_ref)


def _inproj_sample_kernel(x_ref, sc_ref, sh_ref, gmix_ref, w2_ref, gq_ref, wuq_ref, gqq_ref,
                          gkv_ref, gqk_ref, wukt_ref, cos_ref, sin_ref, wconv_ref, h0_ref, h1_ref,
                          wbrb_ref, gvln_ref, coef_ref, bias_ref, wbrc_ref,
                          qg_ref, ql_ref, ckv_ref, kr_ref, z_ref, cv_ref, bg0_ref, part_ref):
    seg = _inproj_common(x_ref, sc_ref, sh_ref, gmix_ref, w2_ref)
    cos = cos_ref[...]
    sin = sin_ref[...]
    qs = _heads_q(seg, gq_ref, wuq_ref, gqq_ref, cos, sin)
    for hh in range(N_HEADS):
        qg = qs[hh] * gqk_ref[...]
        qg_ref[hh] = qg
        ql_ref[hh] = _dot(qg.astype(BF16), wukt_ref[hh])
    ckv, krp = _latent_kv(seg, gkv_ref, cos, sin)
    ckv_ref[...] = ckv
    kr_ref[...] = krp

    gate_b = seg(S_CB, S_CC)
    z = seg(S_CC, S_CH) * seg(S_CH, S_U)
    wc = wconv_ref[...]
    y = wc[0:1, :] * h0_ref[...] + wc[1:2, :] * h1_ref[...] + wc[2:3, :] * z
    z_ref[...] = z

    u = seg(S_U, S_V)
    vn = _layernorm_rows(seg(S_V, S_G), gvln_ref[...])
    cv_ref[...] = vn
    sg = u * (vn * coef_ref[...] + bias_ref[...])

    _gates_and_partial(seg, gate_b * y, sg, wbrb_ref, wbrc_ref, bg0_ref, part_ref)


def _inproj_prompt(x, scale_m, shift_m, lw, cos, sin):
    b, t, d = x.shape
    tm = TM_IN
    nt = t // tm
    tok = lambda w: pl.BlockSpec((None, tm, w), lambda i, j: (i, j, 0))
    mod = pl.BlockSpec((None, 1, d), lambda i, j: (i, 0, 0))
    head = pl.BlockSpec((None, N_HEADS, tm, HP), lambda i, j: (i, 0, j, 0))
    in_specs = [
        tok(d), mod, mod, _const_spec((1, d)), _const_spec((d, S_END)),
        _const_spec((1, Q_LORA)), _const_spec((Q_LORA, 2 * N_HEADS * HP)), _const_spec((1, HP)),
        _const_spec((1, KV_LORA)), _const_spec((KV_LORA, N_HEADS * HP)), _const_spec((1, HP)),
        _const_spec((KV_LORA, N_HEADS * HP)),
        pl.BlockSpec((tm, HP), lambda i, j: (j, 0)), pl.BlockSpec((tm, HP), lambda i, j: (j, 0)),
        _const_spec((CONV_W, CONV_DIM)), _const_spec((CONV_DIM, d)), _const_spec((1, GM_DIM)),
        _const_spec((GM_GROUPS * CHUNK, CHUNK)), _const_spec((CHUNK, GM_DIM)), _const_spec((GM_DIM, d)),
    ]
    out_specs = [
        head, head, head, tok(KV_LORA), tok(HP),
        pl.BlockSpec((None, 8, CONV_DIM), lambda i, j: (i, 0, 0)),
        pl.BlockSpec((None, CHUNK, GM_DIM), lambda i, j: (i, 0, 0)),
        tok(d), tok(d),
    ]
    out_shape = [
        jax.ShapeDtypeStruct((b, N_HEADS, t, HP), BF16),
        jax.ShapeDtypeStruct((b, N_HEADS, t, HP), BF16),
        jax.ShapeDtypeStruct((b, N_HEADS, t, HP), BF16),
        jax.ShapeDtypeStruct((b, t, KV_LORA), F32),
        jax.ShapeDtypeStruct((b, t, HP), F32),
        jax.ShapeDtypeStruct((b, 8, CONV_DIM), F32),
        jax.ShapeDtypeStruct((b, CHUNK, GM_DIM), F32),
        jax.ShapeDtypeStruct((b, t, d), BF16),
        jax.ShapeDtypeStruct((b, t, d), F32),
    ]
    return pl.pallas_call(
        functools.partial(_inproj_prompt_kernel, tm=tm),
        grid=(b, nt),
        in_specs=in_specs, out_specs=out_specs, out_shape=out_shape,
        scratch_shapes=[pltpu.VMEM((tm + 8, CONV_DIM), F32)],
        compiler_params=_cparams(("arbitrary", "arbitrary")),
        name="inproj_prompt",
    )(x, scale_m, shift_m, lw["g_norm_mix"], lw["w2"], lw["g_q_lat"], lw["wuq2"], lw["g_qk_q"],
      lw["g_kv_lat"], lw["wuk"], lw["g_qk_k"], lw["wuv"], cos, sin, lw["w_conv"],
      lw["w_br_b"], lw["g_v_ln"], lw["w_sp"], lw["b_sp"], lw["w_br_c"])


def _inproj_sample(x, scale_m, shift_m, lw, cos, sin, hist0, hist1):
    n, d = x.shape
    full = lambda *s: pl.BlockSpec(s, lambda i: (0,) * len(s))
    in_specs = [
        full(n, d), full(n, d), full(n, d), full(1, d), full(d, S_END),
        full(1, Q_LORA), full(Q_LORA, 2 * N_HEADS * HP), full(1, HP),
        full(1, KV_LORA), full(1, HP), full(N_HEADS, HP, KV_LORA), full(1, HP), full(1, HP),
        full(CONV_W, CONV_DIM), full(n, CONV_DIM), full(n, CONV_DIM),
        full(CONV_DIM, d), full(1, GM_DIM), full(1, GM_DIM), full(1, GM_DIM), full(GM_DIM, d),
    ]
    out_specs = [full(N_HEADS, n, HP), full(N_HEADS, n, KV_LORA), full(n, KV_LORA), full(n, HP),
                 full(n, CONV_DIM), full(n, GM_DIM), full(n, d), full(n, d)]
    out_shape = [
        jax.ShapeDtypeStruct((N_HEADS, n, HP), F32),
        jax.ShapeDtypeStruct((N_HEADS, n, KV_LORA), F32),
        jax.ShapeDtypeStruct((n, KV_LORA), F32),
        jax.ShapeDtypeStruct((n, HP), F32),
        jax.ShapeDtypeStruct((n, CONV_DIM), F32),
        jax.ShapeDtypeStruct((n, GM_DIM), F32),
        jax.ShapeDtypeStruct((n, d), BF16),
        jax.ShapeDtypeStruct((n, d), F32),
    ]
    return pl.pallas_call(
        _inproj_sample_kernel,
        grid=(1,),
        in_specs=in_specs, out_specs=out_specs, out_shape=out_shape,
        compiler_params=_cparams(("arbitrary",)),
        name="inproj_sample",
    )(x, scale_m, shift_m, lw["g_norm_mix"], lw["w2"], lw["g_q_lat"], lw["wuq2"], lw["g_qk_q"],
      lw["g_kv_lat"], lw["g_qk_k"], lw["wukt_pad"], cos, sin, lw["w_conv"], hist0, hist1,
      lw["w_br_b"], lw["g_v_ln"], lw["sp_coef"], lw["b_sp"][0:1], lw["w_br_c"])


def _flash_kernel(q_ref, k_ref, v_ref, o_ref, *, tq, tk):
    assert tq == tk
    qi = pl.program_id(2)
    q = q_ref[...]

    def step(j, carry, masked):
        m, l, acc = carry
        start = pl.multiple_of(j * tk, tk)
        k = k_ref[pl.ds(start, tk), :]
        v = v_ref[pl.ds(start, tk), :]
        s = lax.dot_general(q, k, (((1,), (1,)), ((), ())), preferred_element_type=F32)
        if masked:
            row = lax.broadcasted_iota(jnp.int32, (tq, tk), 0)
            col = lax.broadcasted_iota(jnp.int32, (tq, tk), 1)
            s = jnp.where(col <= row, s, -jnp.inf)
        m_new = jnp.maximum(m, jnp.max(s, axis=-1, keepdims=True))
        alpha = jnp.exp(m - m_new)
        p = jnp.exp(s - m_new)
        l = alpha * l + jnp.sum(p, axis=-1, keepdims=True)
        acc = alpha * acc + _dot(p.astype(BF16), v)
        return m_new, l, acc

    m0 = jnp.full((tq, 1), -jnp.inf, F32)
    l0 = jnp.zeros((tq, 1), F32)
    a0 = jnp.zeros((tq, HP), F32)
    carry = lax.fori_loop(0, qi, lambda j, c: step(j, c, False), (m0, l0, a0))
    m, l, acc = step(qi, carry, True)
    o_ref[...] = (acc / l).astype(BF16)


def _flash_attention(q, k, v):
    b, h, t, _ = q.shape
    tq, tk = TQ, TK
    return pl.pallas_call(
        functools.partial(_flash_kernel, tq=tq, tk=tk),
        grid=(b, h, t // tq),
        in_specs=[pl.BlockSpec((None, None, tq, HP), lambda i, j, n: (i, j, n, 0)),
                  pl.BlockSpec((None, None, t, HP), lambda i, j, n: (i, j, 0, 0)),
                  pl.BlockSpec((None, None, t, HP), lambda i, j, n: (i, j, 0, 0))],
        out_specs=pl.BlockSpec((None, tq, HP), lambda i, j, n: (i, n, j)),
        out_shape=jax.ShapeDtypeStruct((b, t, h * HP), BF16),
        compiler_params=_cparams(("arbitrary", "arbitrary", "arbitrary")),
        name="flash_prompt",
    )(q, k, v)


def _paged_kernel(pt_ref, wkt_ref, ql_ref, qr_ref, cnew_ref, krnew_ref, poolc_ref, poolk_ref, o_ref,
                  cbuf, kbuf, csem, ksem, lhs_ref, m_ref, l_ref, acc_ref, *, layer, n_pages, n_pg, n_chunk):
    i = pl.program_id(0)
    j = pl.program_id(1)
    nj = pl.num_programs(1)
    nk = N_HEADS * QK_NOPE
    step = i * nj + j
    slot = step % 2

    def page_copies(base, buf_slot, p):
        pid = pt_ref[base + p]
        return (pltpu.make_async_copy(poolc_ref.at[layer, pid], cbuf.at[buf_slot, p], csem.at[buf_slot]),
                pltpu.make_async_copy(poolk_ref.at[layer, pid], kbuf.at[buf_slot, p], ksem.at[buf_slot]))

    def start_step(t, buf_slot):
        base = (t // nj) * n_pages + (t % nj) * n_pg
        for p in range(n_pg):
            for cp in page_copies(base, buf_slot, p):
                cp.start()

    @pl.when(step == 0)
    def _():
        start_step(step, slot)

    @pl.when(step + 1 < pl.num_programs(0) * nj)
    def _():
        start_step(step + 1, 1 - slot)

    for p in range(n_pg):
        for cp in page_copies(i * n_pages + j * n_pg, slot, p):
            cp.wait()

    @pl.when(j == 0)
    def _():
        lhs_ref[0:nk, :] = wkt_ref[...]
        lhs_ref[nk:nk + HEAD_ROWS, :] = ql_ref[...]
        m_ref[...] = jnp.full(m_ref.shape, -jnp.inf, F32)
        l_ref[...] = jnp.zeros(l_ref.shape, F32)
        acc_ref[...] = jnp.zeros(acc_ref.shape, F32)

    qr = qr_ref[...]

    def update(cb, krt, valid_keys):
        kx = lax.dot_general(lhs_ref[...], cb, (((1,), (1,)), ((), ())), preferred_element_type=F32)
        sq = kx[:nk, :] * kx[:nk, :]
        ssq = jnp.concatenate([jnp.sum(sq[hh * QK_NOPE:(hh + 1) * QK_NOPE, :], axis=0, keepdims=True)
                               for hh in range(N_HEADS)], axis=0)
        ssq = ssq + jnp.sum(krt * krt, axis=0, keepdims=True)
        inv = lax.rsqrt(ssq * (1.0 / QK_DIM) + EPS)
        st = (kx[nk:, :] + _dot(qr, krt.astype(BF16))) * jnp.concatenate([inv, inv], axis=0)
        if valid_keys is not None:
            kcol = lax.broadcasted_iota(jnp.int32, st.shape, 1)
            st = jnp.where(kcol < valid_keys, st, -jnp.inf)
        m_old = m_ref[...]
        m_new = jnp.maximum(m_old, jnp.max(st, axis=-1, keepdims=True))
        alpha = jnp.exp(m_old - m_new)
        p = jnp.exp(st - m_new)
        l_ref[...] = alpha * l_ref[...] + jnp.sum(p, axis=-1, keepdims=True)
        acc_ref[...] = alpha * acc_ref[...] + _dot(p.astype(BF16), cb)
        m_ref[...] = m_new

    per = n_pg // n_chunk
    for ch in range(n_chunk):
        cb = cbuf[slot, ch * per:(ch + 1) * per].reshape(per * PAGE_SIZE, KV_LORA).astype(BF16)
        krt = jnp.concatenate([kbuf[slot, p] for p in range(ch * per, (ch + 1) * per)], axis=1)
        update(cb, krt, None)

    @pl.when(j == nj - 1)
    def _():
        first = lax.broadcasted_iota(jnp.int32, (LANES, KV_LORA), 0) == 0
        cn = jnp.where(first, jnp.broadcast_to(cnew_ref[...], (LANES, KV_LORA)), 0.0).astype(BF16)
        update(cn, krnew_ref[...], 1)
        o_ref[...] = (acc_ref[...] / l_ref[...])[:N_HEADS, :]


def _paged_attention(layer, page_table, wkt, qlat, qr, cnew, krnew_t, pool_c, pool_krt):
    s, n_pages = page_table.shape
    n_pg = PAGES_PER_STEP
    nj = n_pages // n_pg
    pt = page_table.reshape(-1)
    nk = N_HEADS * QK_NOPE

    per_seq = lambda *shape: pl.BlockSpec((None,) + shape, lambda i, j, pt_ref: (i,) + (0,) * len(shape))
    const = lambda *shape: pl.BlockSpec(shape, lambda i, j, pt_ref: (0,) * len(shape))
    in_specs = [const(nk, KV_LORA), per_seq(HEAD_ROWS, KV_LORA), per_seq(HEAD_ROWS, QK_ROPE),
                per_seq(1, KV_LORA), per_seq(QK_ROPE, LANES),
                pl.BlockSpec(memory_space=pl.ANY), pl.BlockSpec(memory_space=pl.ANY)]
    grid_spec = pltpu.PrefetchScalarGridSpec(
        num_scalar_prefetch=1,
        grid=(s, nj),
        in_specs=in_specs,
        out_specs=per_seq(N_HEADS, KV_LORA),
        scratch_shapes=[pltpu.VMEM((2, n_pg, PAGE_SIZE, KV_LORA), F32),
                        pltpu.VMEM((2, n_pg, QK_ROPE, PAGE_SIZE), F32),
                        pltpu.SemaphoreType.DMA((2,)), pltpu.SemaphoreType.DMA((2,)),
                        pltpu.VMEM((nk + HEAD_ROWS, KV_LORA), BF16), pltpu.VMEM((HEAD_ROWS, 1), F32),
                        pltpu.VMEM((HEAD_ROWS, 1), F32), pltpu.VMEM((HEAD_ROWS, KV_LORA), F32)],
    )
    return pl.pallas_call(
        functools.partial(_paged_kernel, layer=layer, n_pages=n_pages, n_pg=n_pg, n_chunk=PAGE_CHUNKS),
        grid_spec=grid_spec,
        out_shape=jax.ShapeDtypeStruct((s, N_HEADS, KV_LORA), F32),
        compiler_params=_cparams(("arbitrary", "arbitrary")),
        name="paged_sample",
    )(pt, wkt, qlat, qr, cnew, krnew_t, pool_c, pool_krt)


def _uv_kernel(ol_ref, wuv_ref, o_ref):
    for hh in range(N_HEADS):
        ol = ol_ref[:, hh * KV_LORA:(hh + 1) * KV_LORA].astype(BF16)
        o_ref[:, hh * HP:(hh + 1) * HP] = _dot(ol, wuv_ref[:, hh * HP:(hh + 1) * HP]).astype(BF16)


def _uv_project(olat, wuv):
    n = olat.shape[0]
    full = lambda *s: pl.BlockSpec(s, lambda i: (0,) * len(s))
    return pl.pallas_call(
        _uv_kernel, grid=(1,),
        in_specs=[full(n, N_HEADS * KV_LORA), full(KV_LORA, N_HEADS * HP)],
        out_specs=full(n, N_HEADS * HP),
        out_shape=jax.ShapeDtypeStruct((n, N_HEADS * HP), BF16),
        compiler_params=_cparams(("arbitrary",)),
        name="uv_sample",
    )(olat, wuv)


def _merge_kernel(oa_ref, wbra_ref, bg0_ref, part_ref, x_ref, gm_ref, wout_ref, gffn_ref, scf_ref, shf_ref,
                  x1_ref, h2_ref):
    bra = _dot(oa_ref[...], wbra_ref[...])
    merged = bg0_ref[...].astype(F32) * bra + part_ref[...]
    x1 = x_ref[...] + gm_ref[...] * _dot(merged.astype(BF16), wout_ref[...])
    x1_ref[...] = x1
    h2_ref[...] = (_rms_rows(x1) * gffn_ref[...] * (1.0 + scf_ref[...]) + shf_ref[...]).astype(BF16)


def _merge(oa, bg0, part, x, gate_m, scale_f, shift_f, lw, tm):
    g, t, d = x.shape
    mt = gate_m.shape[1]
    tok = lambda w: pl.BlockSpec((None, tm, w), lambda i, j: (i, j, 0))
    if mt == 1:
        mod = pl.BlockSpec((None, 1, d), lambda i, j: (i, 0, 0))
    else:
        mod = pl.BlockSpec((None, tm, d), lambda i, j: (i, j, 0))
    return pl.pallas_call(
        _merge_kernel,
        grid=(g, t // tm),
        in_specs=[tok(N_HEADS * HP), _const_spec((N_HEADS * HP, d)), tok(d), tok(d), tok(d), mod,
                  _const_spec((d, d)), _const_spec((1, d)), mod, mod],
        out_specs=[tok(d), tok(d)],
        out_shape=[jax.ShapeDtypeStruct((g, t, d), F32), jax.ShapeDtypeStruct((g, t, d), BF16)],
        compiler_params=_cparams(("arbitrary", "arbitrary")),
        name="merge",
    )(oa, lw["w_br_a"], bg0, part, x, gate_m, lw["w_out"], lw["g_norm_ffn"], scale_f, shift_f)


def _router_kernel(h_ref, wrt_ref, brt_ref, g_ref, idx_ref, w_ref):
    tm = h_ref.shape[0]
    logits = lax.dot_general(wrt_ref[...], h_ref[...], (((1,), (1,)), ((), ())), preferred_element_type=F32)
    scores = jax.nn.sigmoid(logits)
    choice = scores + brt_ref[...]
    neg = -jnp.inf

    gsc = []
    for g in range(N_EXPERT_GROUPS):
        xg = choice[g * GROUP_SIZE:(g + 1) * GROUP_SIZE, :]
        m1 = jnp.max(xg, axis=0, keepdims=True)
        cnt = jnp.sum(jnp.where(xg == m1, 1.0, 0.0), axis=0, keepdims=True)
        m2 = jnp.max(jnp.where(xg < m1, xg, neg), axis=0, keepdims=True)
        gsc.append(m1 + jnp.where(cnt >= 2.0, m1, m2))

    cands = []
    for g in range(N_EXPERT_GROUPS):
        rank = jnp.zeros((1, tm), F32)
        for o in range(N_EXPERT_GROUPS):
            if o == g:
                continue
            beats = (gsc[o] > gsc[g]) if o > g else (gsc[o] >= gsc[g])
            rank = rank + jnp.where(beats, 1.0, 0.0)
        keep = rank < float(TOPK_GROUPS)
        cands.append(jnp.where(keep, choice[g * GROUP_SIZE:(g + 1) * GROUP_SIZE, :], neg))
    cand = jnp.concatenate(cands, axis=0)

    ridx = lax.broadcasted_iota(jnp.int32, (N_EXPERTS, tm), 0)
    picked = jnp.zeros((N_EXPERTS, tm), F32)
    idxs, ws = [], []
    for _ in range(TOP_K):
        m = jnp.max(cand, axis=0, keepdims=True)
        idx = jnp.min(jnp.where(cand == m, ridx, N_EXPERTS), axis=0, keepdims=True)
        hit = ridx == idx
        picked = jnp.where(hit, 1.0, picked)
        cand = jnp.where(hit, neg, cand)
        idxs.append(idx)
        ws.append(jnp.sum(jnp.where(hit, scores, 0.0), axis=0, keepdims=True))

    w = jnp.where(picked > 0.0, scores, 0.0)
    norm = ROUTED_SCALE / jnp.sum(w, axis=0, keepdims=True)
    gt = w * norm
    g_ref[...] = jnp.concatenate([gt, jnp.zeros((LANES - N_EXPERTS, tm), F32)], axis=0).T
    idx_ref[...] = jnp.concatenate(idxs, axis=0)
    w_ref[...] = jnp.concatenate(ws, axis=0) * norm


def _router(h2, lw, tm):
    m, d = h2.shape
    return pl.pallas_call(
        _router_kernel,
        grid=(m // tm,),
        in_specs=[pl.BlockSpec((tm, d), lambda i: (i, 0)), _const_spec((N_EXPERTS, d)),
                  _const_spec((N_EXPERTS, 1))],
        out_specs=[pl.BlockSpec((tm, LANES), lambda i: (i, 0)), pl.BlockSpec((TOP_K, tm), lambda i: (0, i)),
                   pl.BlockSpec((TOP_K, tm), lambda i: (0, i))],
        out_shape=[jax.ShapeDtypeStruct((m, LANES), F32), jax.ShapeDtypeStruct((TOP_K, m), jnp.int32),
                   jax.ShapeDtypeStruct((TOP_K, m), F32)],
        compiler_params=_cparams(("arbitrary",)),
        name="router",
    )(h2, lw["w_router_t"], lw["b_router_t"])


def _moe_kernel(h_ref, g_ref, wg_ref, wu_ref, wd_ref, wsg_ref, wsu_ref, wsd_ref, x1_ref, gf_ref,
                o_ref, acc_ref, *, epb):
    e = pl.program_id(2)
    h = h_ref[...]
    tm = h.shape[0]

    @pl.when(e == 0)
    def _():
        hs = (_silu(_dot(h, wsg_ref[...])) * _dot(h, wsu_ref[...])).astype(BF16)
        acc_ref[...] = _dot(hs, wsd_ref[...])

    lane = lax.broadcasted_iota(jnp.int32, (tm, LANES), 1)
    gates = g_ref[...]
    for jj in range(epb):
        a = _dot(h, wg_ref[jj])
        b = _dot(h, wu_ref[jj])
        gcol = jnp.sum(jnp.where(lane == e * epb + jj, gates, 0.0), axis=-1, keepdims=True)
        hh = (_silu(a) * b * gcol).astype(BF16)
        acc_ref[...] += _dot(hh, wd_ref[jj])

    @pl.when(e == pl.num_programs(2) - 1)
    def _():
        o_ref[...] = x1_ref[...] + gf_ref[...] * acc_ref[...]


def _moe(h2, gates, x1, gate_f, lw, tm):
    g, t, d = x1.shape
    mt = gate_f.shape[1]
    epb = EXPERTS_PER_STEP
    tok = lambda w: pl.BlockSpec((None, tm, w), lambda i, j, e: (i, j, 0))
    if mt == 1:
        mod = pl.BlockSpec((None, 1, d), lambda i, j, e: (i, 0, 0))
    else:
        mod = pl.BlockSpec((None, tm, d), lambda i, j, e: (i, j, 0))
    return pl.pallas_call(
        functools.partial(_moe_kernel, epb=epb),
        grid=(g, t // tm, N_EXPERTS // epb),
        in_specs=[tok(d), tok(LANES),
                  pl.BlockSpec((epb, d, D_EXPERT), lambda i, j, e: (e, 0, 0)),
                  pl.BlockSpec((epb, d, D_EXPERT), lambda i, j, e: (e, 0, 0)),
                  pl.BlockSpec((epb, D_EXPERT, d), lambda i, j, e: (e, 0, 0)),
                  _const_spec((d, D_EXPERT)), _const_spec((d, D_EXPERT)), _const_spec((D_EXPERT, d)),
                  tok(d), mod],
        out_specs=tok(d),
        out_shape=jax.ShapeDtypeStruct((g, t, d), F32),
        scratch_shapes=[pltpu.VMEM((tm, d), F32)],
        compiler_params=_cparams(("arbitrary", "arbitrary", "arbitrary")),
        name="moe",
    )(h2, gates, lw["w_e_gate"], lw["w_e_up"], lw["w_e_down"], lw["w_sh_gate"], lw["w_sh_up"],
      lw["w_sh_down"], x1, gate_f)


def _dispatch_kernel(x_ref, posk_ref, xs_ref, *, rc):
    x = x_ref[...]
    posk = posk_ref[...]
    nb = x.shape[0]
    for c in range(MOE_ROWS // rc):
        riota = c * rc + lax.broadcasted_iota(jnp.int32, (rc, nb), 0)
        p = jnp.zeros((rc, nb), F32)
        for k in range(TOP_K):
            p = jnp.where(riota == posk[k:k + 1, :], 1.0, p)
        xs_ref[c * rc:(c + 1) * rc, :] = _dot(p.astype(BF16), x).astype(BF16)


def _dispatch(h2, posk_t):
    nblk, nb, d = h2.shape
    return pl.pallas_call(
        functools.partial(_dispatch_kernel, rc=512),
        grid=(nblk,),
        in_specs=[pl.BlockSpec((None, nb, d), lambda i: (i, 0, 0)),
                  pl.BlockSpec((None, TOP_K, nb), lambda i: (i, 0, 0))],
        out_specs=pl.BlockSpec((None, MOE_ROWS, d), lambda i: (i, 0, 0)),
        out_shape=jax.ShapeDtypeStruct((nblk, MOE_ROWS, d), BF16),
        compiler_params=_cparams(("arbitrary",)),
        name="moe_dispatch",
    )(h2, posk_t)


def _expert_kernel(te_ref, src_ref, nact_ref, xs_ref, wg_ref, wu_ref, wd_ref, ys_ref, xbuf, sem):
    t = pl.program_id(0)
    nact = nact_ref[0]
    slot = t % 2

    def copies(tile, buf_slot):
        return [pltpu.make_async_copy(xs_ref.at[src_ref[tile * EXP_CHUNKS + c]], xbuf.at[buf_slot, c],
                                      sem.at[buf_slot]) for c in range(EXP_CHUNKS)]

    @pl.when(jnp.logical_and(t == 0, nact > 0))
    def _():
        for cp in copies(t, slot):
            cp.start()

    @pl.when(t + 1 < nact)
    def _():
        for cp in copies(t + 1, 1 - slot):
            cp.start()

    @pl.when(t < nact)
    def _():
        for cp in copies(t, slot):
            cp.wait()
        x = xbuf[slot].reshape(EXP_CHUNKS * MOE_CHUNK, xbuf.shape[-1])
        h = (_silu(_dot(x, wg_ref[...])) * _dot(x, wu_ref[...])).astype(BF16)
        ys_ref[...] = _dot(h, wd_ref[...]).astype(BF16)

    @pl.when(t >= nact)
    def _():
        ys_ref[...] = jnp.zeros(ys_ref.shape, BF16)


def _experts(xs_chunks, tile_expert, src_chunk, n_active, lw):
    d = xs_chunks.shape[-1]
    n_tiles = tile_expert.shape[0]
    tm = EXP_CHUNKS * MOE_CHUNK
    grid_spec = pltpu.PrefetchScalarGridSpec(
        num_scalar_prefetch=3,
        grid=(n_tiles,),
        in_specs=[pl.BlockSpec(memory_space=pl.ANY),
                  pl.BlockSpec((None, d, D_EXPERT), lambda t, te, src, na: (te[t], 0, 0)),
                  pl.BlockSpec((None, d, D_EXPERT), lambda t, te, src, na: (te[t], 0, 0)),
                  pl.BlockSpec((None, D_EXPERT, d), lambda t, te, src, na: (te[t], 0, 0))],
        out_specs=pl.BlockSpec((tm, d), lambda t, te, src, na: (t, 0)),
        scratch_shapes=[pltpu.VMEM((2, EXP_CHUNKS, MOE_CHUNK, d), BF16), pltpu.SemaphoreType.DMA((2,))],
    )
    return pl.pallas_call(
        _expert_kernel,
        grid_spec=grid_spec,
        out_shape=jax.ShapeDtypeStruct((n_tiles * tm, d), BF16),
        compiler_params=_cparams(("arbitrary",)),
        name="moe_experts",
    )(tile_expert, src_chunk, n_active, xs_chunks, lw["w_e_gate"], lw["w_e_up"], lw["w_e_down"])


def _combine_kernel(src_ref, ys_ref, pos_ref, w_ref, h_ref, x1_ref, gf_ref, wsg_ref, wsu_ref, wsd_ref,
                    o_ref, ybuf, sem, *, cc):
    b = pl.program_id(0)
    slot = b % 2
    nch = MOE_ROWS // MOE_CHUNK

    def copies(blk, buf_slot):
        return [pltpu.make_async_copy(ys_ref.at[src_ref[blk * nch + c]], ybuf.at[buf_slot, c], sem.at[buf_slot])
                for c in range(nch)]

    @pl.when(b == 0)
    def _():
        for cp in copies(b, slot):
            cp.start()

    @pl.when(b + 1 < pl.num_programs(0))
    def _():
        for cp in copies(b + 1, 1 - slot):
            cp.start()

    h = h_ref[...]
    hs = (_silu(_dot(h, wsg_ref[...])) * _dot(h, wsu_ref[...])).astype(BF16)
    acc = _dot(hs, wsd_ref[...])

    for cp in copies(b, slot):
        cp.wait()
    pos = pos_ref[...]
    w = w_ref[...]
    nb = pos.shape[0]
    per = cc // MOE_CHUNK
    for c in range(MOE_ROWS // cc):
        liota = c * cc + lax.broadcasted_iota(jnp.int32, (nb, cc), 1)
        pw = jnp.zeros((nb, cc), F32)
        for k in range(TOP_K):
            pw = jnp.where(liota == pos[:, k:k + 1], w[:, k:k + 1], pw)
        y = ybuf[slot, c * per:(c + 1) * per].reshape(cc, ybuf.shape[-1])
        acc = acc + _dot(pw.astype(BF16), y)
    o_ref[...] = x1_ref[...] + gf_ref[...] * acc


def _combine(ys_chunks, src_chunk, posk, wk, h2, x1, gate_f, lw, blocks_per_seq):
    nblk, nb, d = h2.shape
    nch = MOE_ROWS // MOE_CHUNK
    blk = lambda w: pl.BlockSpec((None, nb, w), lambda i, src: (i, 0, 0))
    const = lambda *s: pl.BlockSpec(s, lambda i, src: (0,) * len(s))
    grid_spec = pltpu.PrefetchScalarGridSpec(
        num_scalar_prefetch=1,
        grid=(nblk,),
        in_specs=[pl.BlockSpec(memory_space=pl.ANY), blk(TOP_K), blk(TOP_K), blk(d), blk(d),
                  pl.BlockSpec((None, 1, d), lambda i, src: (i // blocks_per_seq, 0, 0)),
                  const(d, D_EXPERT), const(d, D_EXPERT), const(D_EXPERT, d)],
        out_specs=blk(d),
        scratch_shapes=[pltpu.VMEM((2, nch, MOE_CHUNK, d), BF16), pltpu.SemaphoreType.DMA((2,))],
    )
    return pl.pallas_call(
        functools.partial(_combine_kernel, cc=512),
        grid_spec=grid_spec,
        out_shape=jax.ShapeDtypeStruct((nblk, nb, d), F32),
        compiler_params=_cparams(("arbitrary",)),
        name="moe_combine",
    )(src_chunk, ys_chunks, posk, wk, h2, x1, gate_f, lw["w_sh_gate"], lw["w_sh_up"], lw["w_sh_down"])


def _moe_plan(idx, nblk):
    t = idx.shape[0]
    nb = t // nblk
    i32 = jnp.int32
    cpb = MOE_ROWS // MOE_CHUNK
    sel = jnp.sum((idx[:, :, None] == jnp.arange(N_EXPERTS, dtype=i32)).astype(i32), axis=1)
    selb = sel.reshape(nblk, nb, N_EXPERTS)
    nch = (jnp.sum(selb, axis=1) + MOE_CHUNK - 1) // MOE_CHUNK
    off = jnp.cumsum(nch, axis=1) - nch
    tot = jnp.sum(nch, axis=1)
    rank = jnp.cumsum(selb, axis=1) - selb
    pos = (off[:, None, :] * MOE_CHUNK + rank).reshape(t, N_EXPERTS)
    posk = jnp.take_along_axis(pos, idx, axis=1)

    ech = jnp.sum(nch, axis=0)
    tiles_e = (ech + EXP_CHUNKS - 1) // EXP_CHUNKS
    tile_end = jnp.cumsum(tiles_e)
    tile_start = tile_end - tiles_e
    n_active = tile_end[-1]
    cumb = jnp.cumsum(nch, axis=0) - nch

    n_tiles = _max_expert_tiles(t, nblk)
    tid = jnp.arange(n_tiles, dtype=i32)
    te = jnp.sum((tid[:, None] >= tile_end[None, :]).astype(i32), axis=1)
    te_last = jnp.sum((n_active - 1 >= tile_end).astype(i32))
    te = jnp.where(tid < n_active, te, te_last)

    slot = jnp.arange(n_tiles * EXP_CHUNKS, dtype=i32)
    st = slot // EXP_CHUNKS
    se = te[st]
    c = slot - tile_start[se] * EXP_CHUNKS
    valid = jnp.logical_and(c < ech[se], st < n_active)
    ends = (cumb + nch).T[se]
    sb = jnp.minimum(jnp.sum((c[:, None] >= ends).astype(i32), axis=1), nblk - 1)
    src_e = jnp.where(valid, sb * cpb + off[sb, se] + (c - cumb[sb, se]), 0)

    j = jnp.arange(cpb, dtype=i32)
    ce = jnp.minimum(jnp.sum((j[None, :, None] >= (off + nch)[:, None, :]).astype(i32), axis=2), N_EXPERTS - 1)
    cvalid = j[None, :] < tot[:, None]
    g = tile_start[ce] * EXP_CHUNKS + jnp.take_along_axis(cumb, ce, axis=1) + (j[None, :] - jnp.take_along_axis(off, ce, axis=1))
    src_c = jnp.where(cvalid, g, 0).reshape(-1)
    return posk, te, src_e, n_active.reshape(1), src_c


def _max_expert_tiles(t, nblk):
    max_chunks = t * TOP_K // MOE_CHUNK + nblk * N_EXPERTS
    return max_chunks // EXP_CHUNKS + N_EXPERTS


def _moe_sparse(h2, idx_t, w_t, x1, gate_f, lw):
    b, t, d = x1.shape
    nblk = b * t // MOE_BLOCK
    idx = idx_t.T
    posk, te, src_e, n_active, src_c = _moe_plan(idx, nblk)
    h2b = h2.reshape(nblk, MOE_BLOCK, d)
    posk_b = posk.reshape(nblk, MOE_BLOCK, TOP_K)
    xs = _dispatch(h2b, posk_b.transpose(0, 2, 1))
    ys = _experts(xs.reshape(nblk * MOE_ROWS // MOE_CHUNK, MOE_CHUNK, d), te, src_e, n_active, lw)
    out = _combine(ys.reshape(-1, MOE_CHUNK, d), src_c, posk_b, w_t.T.reshape(nblk, MOE_BLOCK, TOP_K),
                   h2b, x1.reshape(nblk, MOE_BLOCK, d), gate_f, lw, t // MOE_BLOCK)
    return out.reshape(b, t, d)


def _pad_heads(w, width):
    pad = [(0, 0)] * (w.ndim - 1) + [(0, HP - width)]
    wp = jnp.pad(w, pad)
    return wp.reshape(w.shape[:-2] + (w.shape[-2] * HP,))


def _rot_cols(w):
    half = QK_ROPE // 2
    return jnp.concatenate([-w[..., half:], w[..., :half]], axis=-1)


def _prep_layer(l, p):
    d = D_MODEL
    w_in = p["w_in"][l]
    off_kv = Q_LORA
    off_kr = off_kv + KV_LORA
    off_cb = off_kr + QK_ROPE
    off_g = off_cb + 3 * CONV_DIM + 2 * GM_DIM
    wkr = w_in[:, off_kr:off_cb]
    lead = jnp.zeros((d, QK_NOPE), F32)
    trail = jnp.zeros((d, HP - QK_DIM), F32)
    w2 = jnp.concatenate([
        w_in[:, :off_kr],
        lead, wkr, trail,
        lead, _rot_cols(wkr), trail,
        w_in[:, off_cb:],
    ], axis=1).astype(BF16)
    assert w2.shape[1] == S_END and off_g + N_BRANCH * d == w_in.shape[1]

    wuq = p["w_uq"][l]
    wuq_rot = jnp.concatenate([jnp.zeros_like(wuq[..., :QK_NOPE]), _rot_cols(wuq[..., QK_NOPE:])], axis=-1)
    wuq2 = jnp.concatenate([_pad_heads(wuq, QK_DIM), _pad_heads(wuq_rot, QK_DIM)], axis=-1).astype(BF16)

    def pad_gain(gq):
        return jnp.pad(gq, (0, HP - QK_DIM)).reshape(1, HP)

    w_br_a = jnp.pad(p["w_br_a"][l].reshape(N_HEADS, V_DIM, d), ((0, 0), (0, HP - V_DIM), (0, 0)))
    w_sp = p["w_spatial"][l]
    b_sp = jnp.repeat(p["b_spatial"][l].T, GM_GROUP_DIM, axis=1)
    return dict(
        w_ada=p["w_ada"][l], b_ada=p["b_ada"][l],
        g_norm_mix=p["g_norm_mix"][l].reshape(1, d), g_norm_ffn=p["g_norm_ffn"][l].reshape(1, d),
        w2=w2, g_q_lat=p["g_q_lat"][l].reshape(1, Q_LORA), wuq2=wuq2,
        g_qk_q=pad_gain(p["g_qk_q"][l]), g_qk_k=pad_gain(p["g_qk_k"][l]),
        g_kv_lat=p["g_kv_lat"][l].reshape(1, KV_LORA),
        wuk=_pad_heads(p["w_uk"][l], QK_NOPE).astype(BF16),
        wkt=p["w_uk"][l].reshape(KV_LORA, N_HEADS * QK_NOPE).T.astype(BF16),
        wukt_pad=jnp.pad(p["w_uk"][l].transpose(1, 2, 0), ((0, 0), (0, HP - QK_NOPE), (0, 0))).astype(BF16),
        wuv=_pad_heads(p["w_uv"][l], V_DIM).astype(BF16),
        w_br_a=w_br_a.reshape(N_HEADS * HP, d).astype(BF16),
        w_conv=p["w_conv"][l], w_br_b=p["w_br_b"][l].astype(BF16),
        g_v_ln=p["g_v_ln"][l].reshape(1, GM_DIM),
        w_sp=w_sp.reshape(GM_GROUPS * CHUNK, CHUNK), b_sp=b_sp,
        sp_coef=jnp.repeat(w_sp[:, 0, 0], GM_GROUP_DIM).reshape(1, GM_DIM),
        w_br_c=p["w_br_c"][l].astype(BF16), w_out=p["w_out"][l].astype(BF16),
        w_router_t=p["w_router"][l].T.astype(BF16),
        b_router_t=p["b_router"][l].reshape(N_EXPERTS, 1),
        w_e_gate=p["w_e_gate"][l].astype(BF16), w_e_up=p["w_e_up"][l].astype(BF16),
        w_e_down=p["w_e_down"][l].astype(BF16),
        w_sh_gate=p["w_sh_gate"][l].astype(BF16), w_sh_up=p["w_sh_up"][l].astype(BF16),
        w_sh_down=p["w_sh_down"][l].astype(BF16),
    )


def _rope_tables(pos):
    inv_freq = ROPE_THETA ** (-jnp.arange(0, QK_ROPE, 2, dtype=F32) / QK_ROPE)
    ang = pos.astype(F32)[:, None] * inv_freq[None, :]
    c, s = jnp.cos(ang), jnp.sin(ang)
    n = pos.shape[0]
    cos = jnp.concatenate([jnp.ones((n, QK_NOPE), F32), c, c, jnp.ones((n, HP - QK_DIM), F32)], axis=1)
    sin = jnp.concatenate([jnp.zeros((n, QK_NOPE), F32), s, s, jnp.zeros((n, HP - QK_DIM), F32)], axis=1)
    return cos, sin


def _split_mod(m):
    return [m[:, i * D_MODEL:(i + 1) * D_MODEL] for i in range(6)]


def _prompt_layer(x, mod, lw, cos, sin):
    b, t, d = x.shape
    shift_m, scale_m, gate_m, shift_f, scale_f, gate_f = [a.reshape(b, 1, d) for a in _split_mod(mod)]
    q, k, v, ckv, krp, tail, cv, bg0, part = _inproj_prompt(x, scale_m, shift_m, lw, cos, sin)
    oa = _flash_attention(q, k, v)
    x1, h2 = _merge(oa, bg0, part, x, gate_m, scale_f, shift_f, lw, TM_MERGE)
    _, idx_t, w_t = _router(h2.reshape(b * t, d), lw, TM_ROUTE)
    y = _moe_sparse(h2, idx_t, w_t, x1, gate_f, lw)
    state = (ckv, krp[:, :, QK_NOPE:QK_DIM], tail[:, 8 - (CONV_W - 1):], cv)
    return y, state


def _sample_layer(layer, x, mod, lw, cos, sin, hist, pool_c, pool_krt, page_table):
    n, d = x.shape
    shift_m, scale_m, gate_m, shift_f, scale_f, gate_f = _split_mod(mod)
    qg, ql, ckv, krp, z, cv, bg0, part = _inproj_sample(x, scale_m, shift_m, lw, cos, sin, hist[:, 0], hist[:, 1])
    head_pad = ((0, 0), (0, HEAD_ROWS - N_HEADS), (0, 0))
    qlat = jnp.pad(ql.transpose(1, 0, 2), head_pad).astype(BF16)
    qr = jnp.pad(qg[:, :, QK_NOPE:QK_DIM].transpose(1, 0, 2), head_pad).astype(BF16)
    kr = krp[:, QK_NOPE:QK_DIM]
    krnew_t = jnp.pad(kr[:, :, None], ((0, 0), (0, 0), (0, LANES - 1)))
    olat = _paged_attention(layer, page_table, lw["wkt"], qlat, qr, ckv.reshape(n, 1, KV_LORA), krnew_t,
                            pool_c, pool_krt)
    oa = _uv_project(olat.reshape(n, N_HEADS * KV_LORA), lw["wuv"])
    g3 = lambda a: a.reshape(1, n, -1)
    x1, h2 = _merge(g3(oa), g3(bg0), g3(part), g3(x), g3(gate_m), g3(scale_f), g3(shift_f), lw, n)
    gates, _, _ = _router(h2.reshape(n, d), lw, n)
    y = _moe(h2, g3(gates), x1, g3(gate_f), lw, n).reshape(n, d)
    state = (ckv.reshape(n, 1, KV_LORA), kr.reshape(n, 1, QK_ROPE),
             jnp.stack([hist[:, 1], z], axis=1), cv.reshape(n, 1, GM_DIM))
    return y, state


def kernel(x_prompt, x_sample, cache_kv_latent, cache_k_rope, state_conv, page_table, c_prompt, c_sample,
           w_ada, b_ada, g_norm_mix, g_norm_ffn, w_in, g_q_lat, w_uq, g_kv_lat, w_uk, w_uv, g_qk_q, g_qk_k,
           w_br_a, w_conv, w_br_b, g_v_ln, w_spatial, b_spatial, w_br_c, w_out, w_router, b_router,
           w_e_gate, w_e_up, w_e_down, w_sh_gate, w_sh_up, w_sh_down):
    params = dict(w_ada=w_ada, b_ada=b_ada, g_norm_mix=g_norm_mix, g_norm_ffn=g_norm_ffn, w_in=w_in,
                  g_q_lat=g_q_lat, w_uq=w_uq, g_kv_lat=g_kv_lat, w_uk=w_uk, w_uv=w_uv, g_qk_q=g_qk_q,
                  g_qk_k=g_qk_k, w_br_a=w_br_a, w_conv=w_conv, w_br_b=w_br_b, g_v_ln=g_v_ln,
                  w_spatial=w_spatial, b_spatial=b_spatial, w_br_c=w_br_c, w_out=w_out, w_router=w_router,
                  b_router=b_router, w_e_gate=w_e_gate, w_e_up=w_e_up, w_e_down=w_e_down,
                  w_sh_gate=w_sh_gate, w_sh_up=w_sh_up, w_sh_down=w_sh_down)
    depth = w_in.shape[0]
    nb, t, d = x_prompt.shape
    ns = x_sample.shape[0]
    assert x_sample.shape[1] == 1 and t % TQ == 0 and t % MOE_BLOCK == 0
    past_len = page_table.shape[1] * PAGE_SIZE

    cos_p, sin_p = _rope_tables(jnp.arange(t))
    cos_s, sin_s = _rope_tables(past_len + jnp.arange(1))
    c_all = jnp.concatenate([c_prompt, c_sample], axis=0)
    cache_krt = jnp.swapaxes(cache_k_rope, 2, 3)

    hp, hs = x_prompt, x_sample.reshape(ns, d)
    st_p, st_s = [], []
    for l in range(depth):
        lw = _prep_layer(l, params)
        mod = _adaln(c_all, lw["w_ada"], lw["b_ada"])
        hp, sp = _prompt_layer(hp, mod[:nb], lw, cos_p, sin_p)
        hs, ss = _sample_layer(l, hs, mod[nb:], lw, cos_s, sin_s, state_conv[l], cache_kv_latent, cache_krt,
                               page_table)
        st_p.append(sp)
        st_s.append(ss)

    stack = lambda sts, i: jnp.stack([s[i] for s in sts])
    return (hp, hs.reshape(ns, 1, d),
            stack(st_p, 0), stack(st_p, 1), stack(st_p, 2), stack(st_p, 3),
            stack(st_s, 0), stack(st_s, 1), stack(st_s, 2), stack(st_s, 3))
```

```python
import functools

import jax
import jax.numpy as jnp
from jax import lax
from jax.experimental import pallas as pl
from jax.experimental.pallas import tpu as pltpu

F32 = jnp.float32
BF16 = jnp.bfloat16

D_MODEL = 1024
N_HEADS = 8
QK_NOPE = 64
QK_ROPE = 32
QK_DIM = QK_NOPE + QK_ROPE
V_DIM = 64
Q_LORA = 384
KV_LORA = 256
ROPE_THETA = 10000.0
CONV_W = 3
CONV_DIM = 512
CHUNK = 128
GM_GROUPS = 8
GM_DIM = 512
GM_GROUP_DIM = GM_DIM // GM_GROUPS
N_EXPERTS = 64
TOP_K = 8
N_EXPERT_GROUPS = 8
GROUP_SIZE = N_EXPERTS // N_EXPERT_GROUPS
TOPK_GROUPS = 4
D_EXPERT = 256
ROUTED_SCALE = 2.5
N_BRANCH = 3
EPS = 1e-6
PAGE_SIZE = 128

LANES = 128
HP = LANES
VMEM_LIMIT = 56 * 1024 * 1024

S_Q = 0
S_KV = S_Q + Q_LORA
S_KRM = S_KV + KV_LORA
S_KRR = S_KRM + HP
S_CB = S_KRR + HP
S_CC = S_CB + CONV_DIM
S_CH = S_CC + CONV_DIM
S_U = S_CH + CONV_DIM
S_V = S_U + GM_DIM
S_G = S_V + GM_DIM
S_END = S_G + N_BRANCH * D_MODEL

TM_IN = 512
TQ = 512
TK = 512
TM_MERGE = 512
TM_ROUTE = 512
TM_MOE = 1024
EXPERTS_PER_STEP = 2
MOE_BLOCK = 256
MOE_CHUNK = 16
EXP_CHUNKS = 32
EXP_BUFFERS = 3
COMBINE_BUFFERS = 3
MOE_ROWS = -(-(MOE_BLOCK * TOP_K + N_EXPERTS * (MOE_CHUNK - 1)) // 512) * 512
PAGES_PER_STEP = 32
PAGE_CHUNKS = 2
PAGE_BUFFERS = 3
HEAD_ROWS = 16


def _cparams(sem):
    return pltpu.CompilerParams(dimension_semantics=sem, vmem_limit_bytes=VMEM_LIMIT)


def _const_spec(shape):
    nd = len(shape)
    return pl.BlockSpec(shape, lambda *_: (0,) * nd, pipeline_mode=pl.Buffered(1))


def _dot(a, b):
    return jnp.dot(a, b, preferred_element_type=F32)


def _rms_rows(x):
    return x * lax.rsqrt(jnp.mean(x * x, axis=-1, keepdims=True) + EPS)


def _silu(x):
    return x * jax.nn.sigmoid(x)


def _adaln_kernel(c_ref, w_ref, b_ref, o_ref):
    s = _silu(c_ref[...]).astype(BF16)
    o_ref[...] = _dot(s, w_ref[...].astype(BF16)) + b_ref[...]


def _adaln(c_all, w_ada, b_ada):
    n, d = c_all.shape
    e = w_ada.shape[1]
    tn = 1536
    return pl.pallas_call(
        _adaln_kernel,
        grid=(e // tn,),
        in_specs=[pl.BlockSpec((n, d), lambda j: (0, 0)),
                  pl.BlockSpec((d, tn), lambda j: (0, j)),
                  pl.BlockSpec((1, tn), lambda j: (0, j))],
        out_specs=pl.BlockSpec((n, tn), lambda j: (0, j)),
        out_shape=jax.ShapeDtypeStruct((n, e), F32),
        compiler_params=_cparams(("arbitrary",)),
        name="adaln",
    )(c_all, w_ada, b_ada.reshape(1, e))


def _inproj_common(x_ref, sc_ref, sh_ref, gmix_ref, w2_ref):
    h = (_rms_rows(x_ref[...]) * gmix_ref[...] * (1.0 + sc_ref[...]) + sh_ref[...]).astype(BF16)

    def seg(a, b):
        return _dot(h, w2_ref[:, a:b])

    return seg


def _heads_q(seg, gq_ref, wuq_ref, gqq_ref, cos, sin):
    cqn = (_rms_rows(seg(S_Q, S_KV)) * gq_ref[...]).astype(BF16)
    q2 = _dot(cqn, wuq_ref[...])
    out = []
    for hh in range(N_HEADS):
        qm = q2[:, hh * HP:(hh + 1) * HP]
        qr = q2[:, (N_HEADS + hh) * HP:(N_HEADS + hh + 1) * HP]
        qh = qm * cos + qr * sin
        inv = lax.rsqrt(jnp.sum(qh * qh, axis=-1, keepdims=True) * (1.0 / QK_DIM) + EPS)
        out.append(qh * inv * gqq_ref[...] * (QK_DIM ** -0.5))
    return out


def _latent_kv(seg, gkv_ref, cos, sin):
    ckv = _rms_rows(seg(S_KV, S_KRM)) * gkv_ref[...]
    krp = seg(S_KRM, S_KRR) * cos + seg(S_KRR, S_CB) * sin
    return ckv, krp


def _gates_and_partial(seg, brb_in, brc_in, wbrb_ref, wbrc_ref, bg0_ref, part_ref):
    brb = _dot(brb_in.astype(BF16), wbrb_ref[...])
    brc = _dot(brc_in.astype(BF16), wbrc_ref[...])
    bg = jax.nn.sigmoid(seg(S_G, S_END))
    bg0_ref[...] = bg[:, :D_MODEL].astype(BF16)
    part_ref[...] = bg[:, D_MODEL:2 * D_MODEL] * brb + bg[:, 2 * D_MODEL:] * brc


def _layernorm_rows(v, g):
    xc = v - jnp.mean(v, axis=-1, keepdims=True)
    return xc * lax.rsqrt(jnp.mean(xc * xc, axis=-1, keepdims=True) + EPS) * g


def _inproj_prompt_kernel(x_ref, sc_ref, sh_ref, gmix_ref, w2_ref, gq_ref, wuq_ref, gqq_ref,
                          gkv_ref, wuk_ref, gqk_ref, wuv_ref, cos_ref, sin_ref, wconv_ref,
                          wbrb_ref, gvln_ref, wsp_ref, bsp_ref, wbrc_ref,
                          q_ref, k_ref, v_ref, ckv_ref, kr_ref, tail_ref, cv_ref, bg0_ref, part_ref,
                          zbuf_ref, *, tm):
    t = pl.program_id(1)
    seg = _inproj_common(x_ref, sc_ref, sh_ref, gmix_ref, w2_ref)
    cos = cos_ref[...]
    sin = sin_ref[...]

    qs = _heads_q(seg, gq_ref, wuq_ref, gqq_ref, cos, sin)
    for hh in range(N_HEADS):
        q_ref[hh] = qs[hh].astype(BF16)
    ckv, krp = _latent_kv(seg, gkv_ref, cos, sin)
    ckv_ref[...] = ckv
    kr_ref[...] = krp
    ckv_b = ckv.astype(BF16)
    k2 = _dot(ckv_b, wuk_ref[...])
    v2 = _dot(ckv_b, wuv_ref[...])
    for hh in range(N_HEADS):
        kh = k2[:, hh * HP:(hh + 1) * HP] + krp
        inv = lax.rsqrt(jnp.sum(kh * kh, axis=-1, keepdims=True) * (1.0 / QK_DIM) + EPS)
        k_ref[hh] = (kh * inv * gqk_ref[...]).astype(BF16)
        v_ref[hh] = v2[:, hh * HP:(hh + 1) * HP].astype(BF16)

    gate_b = seg(S_CB, S_CC)
    z = seg(S_CC, S_CH) * seg(S_CH, S_U)

    @pl.when(t == 0)
    def _():
        zbuf_ref[0:8, :] = jnp.zeros((8, CONV_DIM), F32)

    zbuf_ref[8:8 + tm, :] = z
    z1 = zbuf_ref[7:7 + tm, :]
    z2 = zbuf_ref[6:6 + tm, :]
    wc = wconv_ref[...]
    y = wc[0:1, :] * z2 + wc[1:2, :] * z1 + wc[2:3, :] * z
    zbuf_ref[0:8, :] = z[tm - 8:tm, :]
    tail_ref[...] = z[tm - 8:tm, :]

    u = seg(S_U, S_V)
    vn = _layernorm_rows(seg(S_V, S_G), gvln_ref[...])
    cv_ref[...] = vn[tm - CHUNK:tm, :]
    vnb = vn.astype(BF16)
    rows = lax.broadcasted_iota(jnp.int32, (GM_GROUPS * CHUNK, CHUNK), 0) % CHUNK
    cols = lax.broadcasted_iota(jnp.int32, (GM_GROUPS * CHUNK, CHUNK), 1)
    wsp = jnp.where(cols <= rows, wsp_ref[...], 0.0).astype(BF16)
    lane_grp = lax.broadcasted_iota(jnp.int32, (CHUNK, GM_DIM), 1) // GM_GROUP_DIM
    sgs = []
    for c in range(tm // CHUNK):
        r = _dot(wsp, vnb[c * CHUNK:(c + 1) * CHUNK, :])
        s = bsp_ref[...]
        for g in range(GM_GROUPS):
            s = s + jnp.where(lane_grp == g, r[g * CHUNK:(g + 1) * CHUNK, :], 0.0)
        sgs.append(u[c * CHUNK:(c + 1) * CHUNK, :] * s)
    sg = jnp.concatenate(sgs, axis=0)

    _gates_and_partial(seg, gate_b * y, sg, wbrb_ref, wbrc_ref, bg0_ref, part_ref)


def _inproj_sample_kernel(x_ref, sc_ref, sh_ref, gmix_ref, w2_ref, gq_ref, wuq_ref, gqq_ref,
                          gkv_ref, gqk_ref, wukt_ref, cos_ref, sin_ref, wconv_ref, h0_ref, h1_ref,
                          wbrb_ref, gvln_ref, coef_ref, bias_ref, wbrc_ref,
                          qg_ref, ql_ref, ckv_ref, kr_ref, z_ref, cv_ref, bg0_ref, part_ref):
    seg = _inproj_common(x_ref, sc_ref, sh_ref, gmix_ref, w2_ref)
    cos = cos_ref[...]
    sin = sin_ref[...]
    qs = _heads_q(seg, gq_ref, wuq_ref, gqq_ref, cos, sin)
    for hh in range(N_HEADS):
        qg = qs[hh] * gqk_ref[...]
        qg_ref[hh] = qg
        ql_ref[hh] = _dot(qg.astype(BF16), wukt_ref[hh])
    ckv, krp = _latent_kv(seg, gkv_ref, cos, sin)
    ckv_ref[...] = ckv
    kr_ref[...] = krp

    gate_b = seg(S_CB, S_CC)
    z = seg(S_CC, S_CH) * seg(S_CH, S_U)
    wc = wconv_ref[...]
    y = wc[0:1, :] * h0_ref[...] + wc[1:2, :] * h1_ref[...] + wc[2:3, :] * z
    z_ref[...] = z

    u = seg(S_U, S_V)
    vn = _layernorm_rows(seg(S_V, S_G), gvln_ref[...])
    cv_ref[...] = vn
    sg = u * (vn * coef_ref[...] + bias_ref[...])

    _gates_and_partial(seg, gate_b * y, sg, wbrb_ref, wbrc_ref, bg0_ref, part_ref)


def _inproj_prompt(x, scale_m, shift_m, lw, cos, sin):
    b, t, d = x.shape
    tm = TM_IN
    nt = t // tm
    tok = lambda w: pl.BlockSpec((None, tm, w), lambda i, j: (i, j, 0))
    mod = pl.BlockSpec((None, 1, d), lambda i, j: (i, 0, 0))
    head = pl.BlockSpec((None, N_HEADS, tm, HP), lambda i, j: (i, 0, j, 0))
    in_specs = [
        tok(d), mod, mod, _const_spec((1, d)), _const_spec((d, S_END)),
        _const_spec((1, Q_LORA)), _const_spec((Q_LORA, 2 * N_HEADS * HP)), _const_spec((1, HP)),
        _const_spec((1, KV_LORA)), _const_spec((KV_LORA, N_HEADS * HP)), _const_spec((1, HP)),
        _const_spec((KV_LORA, N_HEADS * HP)),
        pl.BlockSpec((tm, HP), lambda i, j: (j, 0)), pl.BlockSpec((tm, HP), lambda i, j: (j, 0)),
        _const_spec((CONV_W, CONV_DIM)), _const_spec((CONV_DIM, d)), _const_spec((1, GM_DIM)),
        _const_spec((GM_GROUPS * CHUNK, CHUNK)), _const_spec((CHUNK, GM_DIM)), _const_spec((GM_DIM, d)),
    ]
    out_specs = [
        head, head, head, tok(KV_LORA), tok(HP),
        pl.BlockSpec((None, 8, CONV_DIM), lambda i, j: (i, 0, 0)),
        pl.BlockSpec((None, CHUNK, GM_DIM), lambda i, j: (i, 0, 0)),
        tok(d), tok(d),
    ]
    out_shape = [
        jax.ShapeDtypeStruct((b, N_HEADS, t, HP), BF16),
        jax.ShapeDtypeStruct((b, N_HEADS, t, HP), BF16),
        jax.ShapeDtypeStruct((b, N_HEADS, t, HP), BF16),
        jax.ShapeDtypeStruct((b, t, KV_LORA), F32),
        jax.ShapeDtypeStruct((b, t, HP), F32),
        jax.ShapeDtypeStruct((b, 8, CONV_DIM), F32),
        jax.ShapeDtypeStruct((b, CHUNK, GM_DIM), F32),
        jax.ShapeDtypeStruct((b, t, d), BF16),
        jax.ShapeDtypeStruct((b, t, d), F32),
    ]
    return pl.pallas_call(
        functools.partial(_inproj_prompt_kernel, tm=tm),
        grid=(b, nt),
        in_specs=in_specs, out_specs=out_specs, out_shape=out_shape,
        scratch_shapes=[pltpu.VMEM((tm + 8, CONV_DIM), F32)],
        compiler_params=_cparams(("arbitrary", "arbitrary")),
        name="inproj_prompt",
    )(x, scale_m, shift_m, lw["g_norm_mix"], lw["w2"], lw["g_q_lat"], lw["wuq2"], lw["g_qk_q"],
      lw["g_kv_lat"], lw["wuk"], lw["g_qk_k"], lw["wuv"], cos, sin, lw["w_conv"],
      lw["w_br_b"], lw["g_v_ln"], lw["w_sp"], lw["b_sp"], lw["w_br_c"])


def _inproj_sample(x, scale_m, shift_m, lw, cos, sin, hist0, hist1):
    n, d = x.shape
    full = lambda *s: pl.BlockSpec(s, lambda i: (0,) * len(s))
    in_specs = [
        full(n, d), full(n, d), full(n, d), full(1, d), full(d, S_END),
        full(1, Q_LORA), full(Q_LORA, 2 * N_HEADS * HP), full(1, HP),
        full(1, KV_LORA), full(1, HP), full(N_HEADS, HP, KV_LORA), full(1, HP), full(1, HP),
        full(CONV_W, CONV_DIM), full(n, CONV_DIM), full(n, CONV_DIM),
        full(CONV_DIM, d), full(1, GM_DIM), full(1, GM_DIM), full(1, GM_DIM), full(GM_DIM, d),
    ]
    out_specs = [full(N_HEADS, n, HP), full(N_HEADS, n, KV_LORA), full(n, KV_LORA), full(n, HP),
                 full(n, CONV_DIM), full(n, GM_DIM), full(n, d), full(n, d)]
    out_shape = [
        jax.ShapeDtypeStruct((N_HEADS, n, HP), F32),
        jax.ShapeDtypeStruct((N_HEADS, n, KV_LORA), F32),
        jax.ShapeDtypeStruct((n, KV_LORA), F32),
        jax.ShapeDtypeStruct((n, HP), F32),
        jax.ShapeDtypeStruct((n, CONV_DIM), F32),
        jax.ShapeDtypeStruct((n, GM_DIM), F32),
        jax.ShapeDtypeStruct((n, d), BF16),
        jax.ShapeDtypeStruct((n, d), F32),
    ]
    return pl.pallas_call(
        _inproj_sample_kernel,
        grid=(1,),
        in_specs=in_specs, out_specs=out_specs, out_shape=out_shape,
        compiler_params=_cparams(("arbitrary",)),
        name="inproj_sample",
    )(x, scale_m, shift_m, lw["g_norm_mix"], lw["w2"], lw["g_q_lat"], lw["wuq2"], lw["g_qk_q"],
      lw["g_kv_lat"], lw["g_qk_k"], lw["wukt_pad"], cos, sin, lw["w_conv"], hist0, hist1,
      lw["w_br_b"], lw["g_v_ln"], lw["sp_coef"], lw["b_sp"][0:1], lw["w_br_c"])


def _flash_kernel(q_ref, k_ref, v_ref, o_ref, *, tq, tk):
    assert tq == tk
    qi = pl.program_id(2)
    q = q_ref[...]

    def step(j, carry, masked):
        m, l, acc = carry
        start = pl.multiple_of(j * tk, tk)
        k = k_ref[pl.ds(start, tk), :]
        v = v_ref[pl.ds(start, tk), :]
        s = lax.dot_general(q, k, (((1,), (1,)), ((), ())), preferred_element_type=F32)
        if masked:
            row = lax.broadcasted_iota(jnp.int32, (tq, tk), 0)
            col = lax.broadcasted_iota(jnp.int32, (tq, tk), 1)
            s = jnp.where(col <= row, s, -jnp.inf)
        m_new = jnp.maximum(m, jnp.max(s, axis=-1, keepdims=True))
        alpha = jnp.exp(m - m_new)
        p = jnp.exp(s - m_new)
        l = alpha * l + jnp.sum(p, axis=-1, keepdims=True)
        acc = alpha * acc + _dot(p.astype(BF16), v)
        return m_new, l, acc

    m0 = jnp.full((tq, 1), -jnp.inf, F32)
    l0 = jnp.zeros((tq, 1), F32)
    a0 = jnp.zeros((tq, HP), F32)
    carry = lax.fori_loop(0, qi, lambda j, c: step(j, c, False), (m0, l0, a0))
    m, l, acc = step(qi, carry, True)
    o_ref[...] = (acc / l).astype(BF16)


def _flash_attention(q, k, v):
    b, h, t, _ = q.shape
    tq, tk = TQ, TK
    return pl.pallas_call(
        functools.partial(_flash_kernel, tq=tq, tk=tk),
        grid=(b, h, t // tq),
        in_specs=[pl.BlockSpec((None, None, tq, HP), lambda i, j, n: (i, j, n, 0)),
                  pl.BlockSpec((None, None, t, HP), lambda i, j, n: (i, j, 0, 0)),
                  pl.BlockSpec((None, None, t, HP), lambda i, j, n: (i, j, 0, 0))],
        out_specs=pl.BlockSpec((None, tq, HP), lambda i, j, n: (i, n, j)),
        out_shape=jax.ShapeDtypeStruct((b, t, h * HP), BF16),
        compiler_params=_cparams(("arbitrary", "arbitrary", "arbitrary")),
        name="flash_prompt",
    )(q, k, v)


def _paged_kernel(pt_ref, wkt_ref, ql_ref, qr_ref, cnew_ref, krnew_ref, poolc_ref, poolk_ref, o_ref,
                  cbuf, kbuf, csem, ksem, lhs_ref, m_ref, l_ref, acc_ref, *, layer, n_pages, n_pg, n_chunk):
    i = pl.program_id(0)
    j = pl.program_id(1)
    nj = pl.num_programs(1)
    nk = N_HEADS * QK_NOPE
    step = i * nj + j
    last_step = pl.num_programs(0) * nj - 1
    ahead = PAGE_BUFFERS - 1
    slot = step % PAGE_BUFFERS
    nxt_slot = (step + ahead) % PAGE_BUFFERS

    def step_base(t):
        t = jnp.minimum(t, last_step)
        return (t // nj) * n_pages + (t % nj) * n_pg

    nxt_base = step_base(step + ahead)

    def page_copies(base, buf_slot, p):
        pid = pt_ref[base + p]
        return (pltpu.make_async_copy(poolc_ref.at[layer, pid], cbuf.at[buf_slot, p], csem.at[buf_slot]),
                pltpu.make_async_copy(poolk_ref.at[layer, pid], kbuf.at[buf_slot, p], ksem.at[buf_slot]))

    @pl.when(step == 0)
    def _():
        for a in range(ahead):
            for p in range(n_pg):
                for cp in page_copies(step_base(a), a, p):
                    cp.start()

    for p in range(n_pg):
        for cp in page_copies(i * n_pages + j * n_pg, slot, p):
            cp.wait()

    @pl.when(j == 0)
    def _():
        lhs_ref[0:nk, :] = wkt_ref[...]
        lhs_ref[nk:nk + HEAD_ROWS, :] = ql_ref[...]
        m_ref[...] = jnp.full(m_ref.shape, -jnp.inf, F32)
        l_ref[...] = jnp.zeros(l_ref.shape, F32)
        acc_ref[...] = jnp.zeros(acc_ref.shape, F32)

    qr = qr_ref[...]

    def update(cb, krt, valid_keys):
        kx = lax.dot_general(lhs_ref[...], cb, (((1,), (1,)), ((), ())), preferred_element_type=F32)
        sq = kx[:nk, :] * kx[:nk, :]
        ssq = jnp.concatenate([jnp.sum(sq[hh * QK_NOPE:(hh + 1) * QK_NOPE, :], axis=0, keepdims=True)
                               for hh in range(N_HEADS)], axis=0)
        ssq = ssq + jnp.sum(krt * krt, axis=0, keepdims=True)
        inv = lax.rsqrt(ssq * (1.0 / QK_DIM) + EPS)
        st = (kx[nk:, :] + _dot(qr, krt.astype(BF16))) * jnp.concatenate([inv, inv], axis=0)
        if valid_keys is not None:
            kcol = lax.broadcasted_iota(jnp.int32, st.shape, 1)
            st = jnp.where(kcol < valid_keys, st, -jnp.inf)
        m_old = m_ref[...]
        m_new = jnp.maximum(m_old, jnp.max(st, axis=-1, keepdims=True))
        alpha = jnp.exp(m_old - m_new)
        p = jnp.exp(st - m_new)
        l_ref[...] = alpha * l_ref[...] + jnp.sum(p, axis=-1, keepdims=True)
        acc_ref[...] = alpha * acc_ref[...] + _dot(p.astype(BF16), cb)
        m_ref[...] = m_new

    per = n_pg // n_chunk
    for ch in range(n_chunk):
        cb = cbuf[slot, ch * per:(ch + 1) * per].reshape(per * PAGE_SIZE, KV_LORA).astype(BF16)
        krt = jnp.concatenate([kbuf[slot, p] for p in range(ch * per, (ch + 1) * per)], axis=1)
        update(cb, krt, None)
        for p in range(ch * per, (ch + 1) * per):
            for cp in page_copies(nxt_base, nxt_slot, p):
                cp.start()

    @pl.when(j == nj - 1)
    def _():
        first = lax.broadcasted_iota(jnp.int32, (LANES, KV_LORA), 0) == 0
        cn = jnp.where(first, jnp.broadcast_to(cnew_ref[...], (LANES, KV_LORA)), 0.0).astype(BF16)
        update(cn, krnew_ref[...], 1)
        o_ref[...] = (acc_ref[...] / l_ref[...])[:N_HEADS, :]

    @pl.when(step == last_step)
    def _():
        for a in range(1, PAGE_BUFFERS):
            for p in range(n_pg):
                for cp in page_copies(nxt_base, (step + a) % PAGE_BUFFERS, p):
                    cp.wait()


def _paged_attention(layer, page_table, wkt, qlat, qr, cnew, krnew_t, pool_c, pool_krt):
    s, n_pages = page_table.shape
    n_pg = PAGES_PER_STEP
    nj = n_pages // n_pg
    pt = page_table.reshape(-1)
    nk = N_HEADS * QK_NOPE

    per_seq = lambda *shape: pl.BlockSpec((None,) + shape, lambda i, j, pt_ref: (i,) + (0,) * len(shape))
    const = lambda *shape: pl.BlockSpec(shape, lambda i, j, pt_ref: (0,) * len(shape))
    in_specs = [const(nk, KV_LORA), per_seq(HEAD_ROWS, KV_LORA), per_seq(HEAD_ROWS, QK_ROPE),
                per_seq(1, KV_LORA), per_seq(QK_ROPE, LANES),
                pl.BlockSpec(memory_space=pl.ANY), pl.BlockSpec(memory_space=pl.ANY)]
    grid_spec = pltpu.PrefetchScalarGridSpec(
        num_scalar_prefetch=1,
        grid=(s, nj),
        in_specs=in_specs,
        out_specs=per_seq(N_HEADS, KV_LORA),
        scratch_shapes=[pltpu.VMEM((PAGE_BUFFERS, n_pg, PAGE_SIZE, KV_LORA), F32),
                        pltpu.VMEM((PAGE_BUFFERS, n_pg, QK_ROPE, PAGE_SIZE), F32),
                        pltpu.SemaphoreType.DMA((PAGE_BUFFERS,)), pltpu.SemaphoreType.DMA((PAGE_BUFFERS,)),
                        pltpu.VMEM((nk + HEAD_ROWS, KV_LORA), BF16), pltpu.VMEM((HEAD_ROWS, 1), F32),
                        pltpu.VMEM((HEAD_ROWS, 1), F32), pltpu.VMEM((HEAD_ROWS, KV_LORA), F32)],
    )
    return pl.pallas_call(
        functools.partial(_paged_kernel, layer=layer, n_pages=n_pages, n_pg=n_pg, n_chunk=PAGE_CHUNKS),
        grid_spec=grid_spec,
        out_shape=jax.ShapeDtypeStruct((s, N_HEADS, KV_LORA), F32),
        compiler_params=_cparams(("arbitrary", "arbitrary")),
        name="paged_sample",
    )(pt, wkt, qlat, qr, cnew, krnew_t, pool_c, pool_krt)


def _uv_kernel(ol_ref, wuv_ref, o_ref):
    for hh in range(N_HEADS):
        ol = ol_ref[:, hh * KV_LORA:(hh + 1) * KV_LORA].astype(BF16)
        o_ref[:, hh * HP:(hh + 1) * HP] = _dot(ol, wuv_ref[:, hh * HP:(hh + 1) * HP]).astype(BF16)


def _uv_project(olat, wuv):
    n = olat.shape[0]
    full = lambda *s: pl.BlockSpec(s, lambda i: (0,) * len(s))
    return pl.pallas_call(
        _uv_kernel, grid=(1,),
        in_specs=[full(n, N_HEADS * KV_LORA), full(KV_LORA, N_HEADS * HP)],
        out_specs=full(n, N_HEADS * HP),
        out_shape=jax.ShapeDtypeStruct((n, N_HEADS * HP), BF16),
        compiler_params=_cparams(("arbitrary",)),
        name="uv_sample",
    )(olat, wuv)


def _merge_kernel(oa_ref, wbra_ref, bg0_ref, part_ref, x_ref, gm_ref, wout_ref, gffn_ref, scf_ref, shf_ref,
                  x1_ref, h2_ref):
    bra = _dot(oa_ref[...], wbra_ref[...])
    merged = bg0_ref[...].astype(F32) * bra + part_ref[...]
    x1 = x_ref[...] + gm_ref[...] * _dot(merged.astype(BF16), wout_ref[...])
    x1_ref[...] = x1
    h2_ref[...] = (_rms_rows(x1) * gffn_ref[...] * (1.0 + scf_ref[...]) + shf_ref[...]).astype(BF16)


def _merge(oa, bg0, part, x, gate_m, scale_f, shift_f, lw, tm):
    g, t, d = x.shape
    mt = gate_m.shape[1]
    tok = lambda w: pl.BlockSpec((None, tm, w), lambda i, j: (i, j, 0))
    if mt == 1:
        mod = pl.BlockSpec((None, 1, d), lambda i, j: (i, 0, 0))
    else:
        mod = pl.BlockSpec((None, tm, d), lambda i, j: (i, j, 0))
    return pl.pallas_call(
        _merge_kernel,
        grid=(g, t // tm),
        in_specs=[tok(N_HEADS * HP), _const_spec((N_HEADS * HP, d)), tok(d), tok(d), tok(d), mod,
                  _const_spec((d, d)), _const_spec((1, d)), mod, mod],
        out_specs=[tok(d), tok(d)],
        out_shape=[jax.ShapeDtypeStruct((g, t, d), F32), jax.ShapeDtypeStruct((g, t, d), BF16)],
        compiler_params=_cparams(("arbitrary", "arbitrary")),
        name="merge",
    )(oa, lw["w_br_a"], bg0, part, x, gate_m, lw["w_out"], lw["g_norm_ffn"], scale_f, shift_f)


def _router_kernel(h_ref, wrt_ref, brt_ref, g_ref, idx_ref, w_ref):
    tm = h_ref.shape[0]
    logits = lax.dot_general(wrt_ref[...], h_ref[...], (((1,), (1,)), ((), ())), preferred_element_type=F32)
    scores = jax.nn.sigmoid(logits)
    choice = scores + brt_ref[...]
    neg = -jnp.inf

    gsc = []
    for g in range(N_EXPERT_GROUPS):
        xg = choice[g * GROUP_SIZE:(g + 1) * GROUP_SIZE, :]
        m1 = jnp.max(xg, axis=0, keepdims=True)
        cnt = jnp.sum(jnp.where(xg == m1, 1.0, 0.0), axis=0, keepdims=True)
        m2 = jnp.max(jnp.where(xg < m1, xg, neg), axis=0, keepdims=True)
        gsc.append(m1 + jnp.where(cnt >= 2.0, m1, m2))

    cands = []
    for g in range(N_EXPERT_GROUPS):
        rank = jnp.zeros((1, tm), F32)
        for o in range(N_EXPERT_GROUPS):
            if o == g:
                continue
            beats = (gsc[o] > gsc[g]) if o > g else (gsc[o] >= gsc[g])
            rank = rank + jnp.where(beats, 1.0, 0.0)
        keep = rank < float(TOPK_GROUPS)
        cands.append(jnp.where(keep, choice[g * GROUP_SIZE:(g + 1) * GROUP_SIZE, :], neg))
    cand = jnp.concatenate(cands, axis=0)

    ridx = lax.broadcasted_iota(jnp.int32, (N_EXPERTS, tm), 0)
    picked = jnp.zeros((N_EXPERTS, tm), F32)
    idxs, ws = [], []
    for _ in range(TOP_K):
        m = jnp.max(cand, axis=0, keepdims=True)
        idx = jnp.min(jnp.where(cand == m, ridx, N_EXPERTS), axis=0, keepdims=True)
        hit = ridx == idx
        picked = jnp.where(hit, 1.0, picked)
        cand = jnp.where(hit, neg, cand)
        idxs.append(idx)
        ws.append(jnp.sum(jnp.where(hit, scores, 0.0), axis=0, keepdims=True))

    w = jnp.where(picked > 0.0, scores, 0.0)
    norm = ROUTED_SCALE / jnp.sum(w, axis=0, keepdims=True)
    gt = w * norm
    g_ref[...] = jnp.concatenate([gt, jnp.zeros((LANES - N_EXPERTS, tm), F32)], axis=0).T
    idx_ref[...] = jnp.concatenate(idxs, axis=0)
    w_ref[...] = jnp.concatenate(ws, axis=0) * norm


def _router(h2, lw, tm):
    m, d = h2.shape
    return pl.pallas_call(
        _router_kernel,
        grid=(m // tm,),
        in_specs=[pl.BlockSpec((tm, d), lambda i: (i, 0)), _const_spec((N_EXPERTS, d)),
                  _const_spec((N_EXPERTS, 1))],
        out_specs=[pl.BlockSpec((tm, LANES), lambda i: (i, 0)), pl.BlockSpec((TOP_K, tm), lambda i: (0, i)),
                   pl.BlockSpec((TOP_K, tm), lambda i: (0, i))],
        out_shape=[jax.ShapeDtypeStruct((m, LANES), F32), jax.ShapeDtypeStruct((TOP_K, m), jnp.int32),
                   jax.ShapeDtypeStruct((TOP_K, m), F32)],
        compiler_params=_cparams(("arbitrary",)),
        name="router",
    )(h2, lw["w_router_t"], lw["b_router_t"])


def _moe_kernel(h_ref, g_ref, wg_ref, wu_ref, wd_ref, wsg_ref, wsu_ref, wsd_ref, x1_ref, gf_ref,
                o_ref, acc_ref, *, epb):
    e = pl.program_id(2)
    h = h_ref[...]
    tm = h.shape[0]

    @pl.when(e == 0)
    def _():
        hs = (_silu(_dot(h, wsg_ref[...])) * _dot(h, wsu_ref[...])).astype(BF16)
        acc_ref[...] = _dot(hs, wsd_ref[...])

    lane = lax.broadcasted_iota(jnp.int32, (tm, LANES), 1)
    gates = g_ref[...]
    for jj in range(epb):
        a = _dot(h, wg_ref[jj])
        b = _dot(h, wu_ref[jj])
        gcol = jnp.sum(jnp.where(lane == e * epb + jj, gates, 0.0), axis=-1, keepdims=True)
        hh = (_silu(a) * b * gcol).astype(BF16)
        acc_ref[...] += _dot(hh, wd_ref[jj])

    @pl.when(e == pl.num_programs(2) - 1)
    def _():
        o_ref[...] = x1_ref[...] + gf_ref[...] * acc_ref[...]


def _moe(h2, gates, x1, gate_f, lw, tm):
    g, t, d = x1.shape
    mt = gate_f.shape[1]
    epb = EXPERTS_PER_STEP
    tok = lambda w: pl.BlockSpec((None, tm, w), lambda i, j, e: (i, j, 0))
    if mt == 1:
        mod = pl.BlockSpec((None, 1, d), lambda i, j, e: (i, 0, 0))
    else:
        mod = pl.BlockSpec((None, tm, d), lambda i, j, e: (i, j, 0))
    return pl.pallas_call(
        functools.partial(_moe_kernel, epb=epb),
        grid=(g, t // tm, N_EXPERTS // epb),
        in_specs=[tok(d), tok(LANES),
                  pl.BlockSpec((epb, d, D_EXPERT), lambda i, j, e: (e, 0, 0)),
                  pl.BlockSpec((epb, d, D_EXPERT), lambda i, j, e: (e, 0, 0)),
                  pl.BlockSpec((epb, D_EXPERT, d), lambda i, j, e: (e, 0, 0)),
                  _const_spec((d, D_EXPERT)), _const_spec((d, D_EXPERT)), _const_spec((D_EXPERT, d)),
                  tok(d), mod],
        out_specs=tok(d),
        out_shape=jax.ShapeDtypeStruct((g, t, d), F32),
        scratch_shapes=[pltpu.VMEM((tm, d), F32)],
        compiler_params=_cparams(("arbitrary", "arbitrary", "arbitrary")),
        name="moe",
    )(h2, gates, lw["w_e_gate"], lw["w_e_up"], lw["w_e_down"], lw["w_sh_gate"], lw["w_sh_up"],
      lw["w_sh_down"], x1, gate_f)


def _dispatch_kernel(x_ref, idx_ref, xs_ref, posk_ref, nch_ref, off_ref, *, rc):
    x = x_ref[...]
    idx = idx_ref[...]
    nb = x.shape[0]

    eiota = lax.broadcasted_iota(jnp.int32, (N_EXPERTS, nb), 0)
    hits = [eiota == idx[k:k + 1, :] for k in range(TOP_K)]
    sel = jnp.zeros((N_EXPERTS, nb), F32)
    for hit in hits:
        sel = sel + jnp.where(hit, 1.0, 0.0)
    cnt = jnp.sum(sel, axis=1, keepdims=True)
    nch = jnp.floor((cnt + (MOE_CHUNK - 1.0)) * (1.0 / MOE_CHUNK))
    nch_b = jnp.broadcast_to(nch, (N_EXPERTS, LANES))
    earlier = (lax.broadcasted_iota(jnp.int32, (N_EXPERTS, N_EXPERTS), 1)
               < lax.broadcasted_iota(jnp.int32, (N_EXPERTS, N_EXPERTS), 0))
    off_b = _dot(jnp.where(earlier, 1.0, 0.0).astype(BF16), nch_b.astype(BF16))
    before = (lax.broadcasted_iota(jnp.int32, (nb, nb), 0) < lax.broadcasted_iota(jnp.int32, (nb, nb), 1))
    rank = _dot(sel.astype(BF16), jnp.where(before, 1.0, 0.0).astype(BF16))
    pos = off_b[:, 0:1] * float(MOE_CHUNK) + rank
    posk = jnp.concatenate([jnp.sum(jnp.where(hit, pos, 0.0), axis=0, keepdims=True) for hit in hits],
                           axis=0).astype(jnp.int32)
    posk_ref[...] = posk
    nch_ref[...] = nch_b.astype(jnp.int32)
    off_ref[...] = off_b.astype(jnp.int32)

    for c in range(MOE_ROWS // rc):
        riota = c * rc + lax.broadcasted_iota(jnp.int32, (rc, nb), 0)
        p = jnp.zeros((rc, nb), F32)
        for k in range(TOP_K):
            p = jnp.where(riota == posk[k:k + 1, :], 1.0, p)
        xs_ref[c * rc:(c + 1) * rc, :] = _dot(p.astype(BF16), x).astype(BF16)


def _dispatch(h2, idx_t):
    nblk, nb, d = h2.shape
    small = pl.BlockSpec((None, N_EXPERTS, LANES), lambda i: (i, 0, 0))
    return pl.pallas_call(
        functools.partial(_dispatch_kernel, rc=512),
        grid=(nblk,),
        in_specs=[pl.BlockSpec((None, nb, d), lambda i: (i, 0, 0)),
                  pl.BlockSpec((TOP_K, nb), lambda i: (0, i))],
        out_specs=[pl.BlockSpec((None, MOE_ROWS, d), lambda i: (i, 0, 0)),
                   pl.BlockSpec((None, TOP_K, nb), lambda i: (i, 0, 0)), small, small],
        out_shape=[jax.ShapeDtypeStruct((nblk, MOE_ROWS, d), BF16),
                   jax.ShapeDtypeStruct((nblk, TOP_K, nb), jnp.int32),
                   jax.ShapeDtypeStruct((nblk, N_EXPERTS, LANES), jnp.int32),
                   jax.ShapeDtypeStruct((nblk, N_EXPERTS, LANES), jnp.int32)],
        compiler_params=_cparams(("arbitrary",)),
        name="moe_dispatch",
    )(h2, idx_t)


def _expert_kernel(te_ref, src_ref, nact_ref, xs_ref, wg_ref, wu_ref, wd_ref, ys_ref, xbuf, sem):
    t = pl.program_id(0)
    nact = nact_ref[0]
    ahead = EXP_BUFFERS - 1
    slot = t % EXP_BUFFERS

    def copies(tile, buf_slot):
        return [pltpu.make_async_copy(xs_ref.at[src_ref[tile * EXP_CHUNKS + c]], xbuf.at[buf_slot, c],
                                      sem.at[buf_slot]) for c in range(EXP_CHUNKS)]

    @pl.when(t == 0)
    def _():
        for a in range(ahead):
            @pl.when(a < nact)
            def _():
                for cp in copies(a, a):
                    cp.start()

    @pl.when(t + ahead < nact)
    def _():
        for cp in copies(t + ahead, (t + ahead) % EXP_BUFFERS):
            cp.start()

    @pl.when(t < nact)
    def _():
        for cp in copies(t, slot):
            cp.wait()
        x = xbuf[slot].reshape(EXP_CHUNKS * MOE_CHUNK, xbuf.shape[-1])
        h = (_silu(_dot(x, wg_ref[...])) * _dot(x, wu_ref[...])).astype(BF16)
        ys_ref[...] = _dot(h, wd_ref[...]).astype(BF16)

    @pl.when(t >= nact)
    def _():
        ys_ref[...] = jnp.zeros(ys_ref.shape, BF16)


def _experts(xs_chunks, tile_expert, src_chunk, n_active, lw):
    d = xs_chunks.shape[-1]
    n_tiles = tile_expert.shape[0]
    tm = EXP_CHUNKS * MOE_CHUNK
    grid_spec = pltpu.PrefetchScalarGridSpec(
        num_scalar_prefetch=3,
        grid=(n_tiles,),
        in_specs=[pl.BlockSpec(memory_space=pl.ANY),
                  pl.BlockSpec((None, d, D_EXPERT), lambda t, te, src, na: (te[t], 0, 0)),
                  pl.BlockSpec((None, d, D_EXPERT), lambda t, te, src, na: (te[t], 0, 0)),
                  pl.BlockSpec((None, D_EXPERT, d), lambda t, te, src, na: (te[t], 0, 0))],
        out_specs=pl.BlockSpec((tm, d), lambda t, te, src, na: (t, 0)),
        scratch_shapes=[pltpu.VMEM((EXP_BUFFERS, EXP_CHUNKS, MOE_CHUNK, d), BF16),
                        pltpu.SemaphoreType.DMA((EXP_BUFFERS,))],
    )
    return pl.pallas_call(
        _expert_kernel,
        grid_spec=grid_spec,
        out_shape=jax.ShapeDtypeStruct((n_tiles * tm, d), BF16),
        compiler_params=_cparams(("arbitrary",)),
        name="moe_experts",
    )(tile_expert, src_chunk, n_active, xs_chunks, lw["w_e_gate"], lw["w_e_up"], lw["w_e_down"])


def _combine_kernel(src_ref, ys_ref, pos_ref, w_ref, h_ref, x1_ref, gf_ref, wsg_ref, wsu_ref, wsd_ref,
                    o_ref, ybuf, sem, *, cc):
    b = pl.program_id(0)
    last = pl.num_programs(0) - 1
    ahead = COMBINE_BUFFERS - 1
    slot = b % COMBINE_BUFFERS
    nxt_slot = (b + ahead) % COMBINE_BUFFERS
    nxt = jnp.minimum(b + ahead, last)
    nch = MOE_ROWS // MOE_CHUNK

    def copies(blk, buf_slot, lo=0, hi=nch):
        return [pltpu.make_async_copy(ys_ref.at[src_ref[blk * nch + c]], ybuf.at[buf_slot, c], sem.at[buf_slot])
                for c in range(lo, hi)]

    @pl.when(b == 0)
    def _():
        for a in range(ahead):
            for cp in copies(jnp.minimum(a, last), a):
                cp.start()

    h = h_ref[...]
    hs = (_silu(_dot(h, wsg_ref[...])) * _dot(h, wsu_ref[...])).astype(BF16)
    acc = _dot(hs, wsd_ref[...])

    for cp in copies(b, slot):
        cp.wait()
    pos = pos_ref[...]
    w = w_ref[...]
    nb = pos.shape[0]
    per = cc // MOE_CHUNK
    for c in range(MOE_ROWS // cc):
        liota = c * cc + lax.broadcasted_iota(jnp.int32, (nb, cc), 1)
        pw = jnp.zeros((nb, cc), F32)
        for k in range(TOP_K):
            pw = jnp.where(liota == pos[:, k:k + 1], w[:, k:k + 1], pw)
        y = ybuf[slot, c * per:(c + 1) * per].reshape(cc, ybuf.shape[-1])
        acc = acc + _dot(pw.astype(BF16), y)
        for cp in copies(nxt, nxt_slot, c * per, (c + 1) * per):
            cp.start()
    o_ref[...] = x1_ref[...] + gf_ref[...] * acc

    @pl.when(b == last)
    def _():
        for a in range(1, COMBINE_BUFFERS):
            for cp in copies(nxt, (b + a) % COMBINE_BUFFERS):
                cp.wait()


def _combine(ys_chunks, src_chunk, posk, wk, h2, x1, gate_f, lw, blocks_per_seq):
    nblk, nb, d = h2.shape
    nch = MOE_ROWS // MOE_CHUNK
    blk = lambda w: pl.BlockSpec((None, nb, w), lambda i, src: (i, 0, 0))
    const = lambda *s: pl.BlockSpec(s, lambda i, src: (0,) * len(s))
    grid_spec = pltpu.PrefetchScalarGridSpec(
        num_scalar_prefetch=1,
        grid=(nblk,),
        in_specs=[pl.BlockSpec(memory_space=pl.ANY), blk(TOP_K), blk(TOP_K), blk(d), blk(d),
                  pl.BlockSpec((None, 1, d), lambda i, src: (i // blocks_per_seq, 0, 0)),
                  const(d, D_EXPERT), const(d, D_EXPERT), const(D_EXPERT, d)],
        out_specs=blk(d),
        scratch_shapes=[pltpu.VMEM((COMBINE_BUFFERS, nch, MOE_CHUNK, d), BF16),
                        pltpu.SemaphoreType.DMA((COMBINE_BUFFERS,))],
    )
    return pl.pallas_call(
        functools.partial(_combine_kernel, cc=512),
        grid_spec=grid_spec,
        out_shape=jax.ShapeDtypeStruct((nblk, nb, d), F32),
        compiler_params=_cparams(("arbitrary",)),
        name="moe_combine",
    )(src_chunk, ys_chunks, posk, wk, h2, x1, gate_f, lw["w_sh_gate"], lw["w_sh_up"], lw["w_sh_down"])


def _excl_cumsum(x, axis):
    n = x.shape[axis]
    earlier = jnp.arange(n)[:, None] > jnp.arange(n)[None, :]
    xm = jnp.moveaxis(x, axis, -1)
    out = jnp.sum(jnp.where(earlier, xm[..., None, :], 0), axis=-1)
    return jnp.moveaxis(out, -1, axis)


def _pick(onehot, table):
    return jnp.sum(jnp.where(onehot, table, 0), axis=-1)


def _moe_plan(nch, off, n_tokens):
    nblk = nch.shape[0]
    i32 = jnp.int32
    cpb = MOE_ROWS // MOE_CHUNK
    experts = jnp.arange(N_EXPERTS, dtype=i32)
    tot = jnp.sum(nch, axis=1)
    ech = jnp.sum(nch, axis=0)
    tiles_e = (ech + EXP_CHUNKS - 1) // EXP_CHUNKS
    tile_start = _excl_cumsum(tiles_e, 0)
    tile_end = tile_start + tiles_e
    n_active = jnp.sum(tiles_e)
    cumb = _excl_cumsum(nch, 0)

    n_tiles = _max_expert_tiles(n_tokens, nblk)
    tid = jnp.arange(n_tiles, dtype=i32)
    te = jnp.sum((tid[:, None] >= tile_end[None, :]).astype(i32), axis=1)
    te_last = jnp.sum((n_active - 1 >= tile_end).astype(i32))
    te = jnp.where(tid < n_active, te, te_last)
    is_e = te[:, None] == experts[None, :]
    ts_t = _pick(is_e, tile_start[None, :])
    ech_t = _pick(is_e, ech[None, :])
    ends_t = _pick(is_e[:, None, :], (cumb + nch)[None, :, :])
    off_t = _pick(is_e[:, None, :], off[None, :, :])
    cumb_t = _pick(is_e[:, None, :], cumb[None, :, :])
    c = (tid - ts_t)[:, None] * EXP_CHUNKS + jnp.arange(EXP_CHUNKS, dtype=i32)[None, :]
    valid = jnp.logical_and(c < ech_t[:, None], (tid < n_active)[:, None])
    sb = jnp.minimum(jnp.sum((c[:, :, None] >= ends_t[:, None, :]).astype(i32), axis=2), nblk - 1)
    is_b = sb[:, :, None] == jnp.arange(nblk, dtype=i32)[None, None, :]
    src_e = jnp.where(valid, sb * cpb + _pick(is_b, off_t[:, None, :]) + c - _pick(is_b, cumb_t[:, None, :]), 0)

    j = jnp.arange(cpb, dtype=i32)
    ce = jnp.minimum(jnp.sum((j[None, :, None] >= (off + nch)[:, None, :]).astype(i32), axis=2), N_EXPERTS - 1)
    is_ce = ce[:, :, None] == experts[None, None, :]
    g = (_pick(is_ce, tile_start[None, None, :]) * EXP_CHUNKS + _pick(is_ce, cumb[:, None, :])
         + j[None, :] - _pick(is_ce, off[:, None, :]))
    src_c = jnp.where(j[None, :] < tot[:, None], g, 0)
    return te, src_e.reshape(-1), n_active.reshape(1), src_c.reshape(-1)


def _max_expert_tiles(t, nblk):
    max_chunks = t * TOP_K // MOE_CHUNK + nblk * N_EXPERTS
    return max_chunks // EXP_CHUNKS + N_EXPERTS


def _moe_sparse(h2, idx_t, w_t, x1, gate_f, lw):
    b, t, d = x1.shape
    nblk = b * t // MOE_BLOCK
    h2b = h2.reshape(nblk, MOE_BLOCK, d)
    xs, posk_t, nch, off = _dispatch(h2b, idx_t)
    te, src_e, n_active, src_c = _moe_plan(nch[:, :, 0], off[:, :, 0], b * t)
    ys = _experts(xs.reshape(nblk * MOE_ROWS // MOE_CHUNK, MOE_CHUNK, d), te, src_e, n_active, lw)
    wk = w_t.reshape(TOP_K, nblk, MOE_BLOCK).transpose(1, 2, 0)
    out = _combine(ys.reshape(-1, MOE_CHUNK, d), src_c, posk_t.transpose(0, 2, 1), wk,
                   h2b, x1.reshape(nblk, MOE_BLOCK, d), gate_f, lw, t // MOE_BLOCK)
    return out.reshape(b, t, d)


def _pad_heads(w, width):
    pad = [(0, 0)] * (w.ndim - 1) + [(0, HP - width)]
    wp = jnp.pad(w, pad)
    return wp.reshape(w.shape[:-2] + (w.shape[-2] * HP,))


def _rot_cols(w):
    half = QK_ROPE // 2
    return jnp.concatenate([-w[..., half:], w[..., :half]], axis=-1)


def _prep_layer(l, p):
    d = D_MODEL
    w_in = p["w_in"][l]
    off_kv = Q_LORA
    off_kr = off_kv + KV_LORA
    off_cb = off_kr + QK_ROPE
    off_g = off_cb + 3 * CONV_DIM + 2 * GM_DIM
    wkr = w_in[:, off_kr:off_cb]
    lead = jnp.zeros((d, QK_NOPE), F32)
    trail = jnp.zeros((d, HP - QK_DIM), F32)
    w2 = jnp.concatenate([
        w_in[:, :off_kr],
        lead, wkr, trail,
        lead, _rot_cols(wkr), trail,
        w_in[:, off_cb:],
    ], axis=1).astype(BF16)
    assert w2.shape[1] == S_END and off_g + N_BRANCH * d == w_in.shape[1]

    wuq = p["w_uq"][l]
    wuq_rot = jnp.concatenate([jnp.zeros_like(wuq[..., :QK_NOPE]), _rot_cols(wuq[..., QK_NOPE:])], axis=-1)
    wuq2 = jnp.concatenate([_pad_heads(wuq, QK_DIM), _pad_heads(wuq_rot, QK_DIM)], axis=-1).astype(BF16)

    def pad_gain(gq):
        return jnp.pad(gq, (0, HP - QK_DIM)).reshape(1, HP)

    w_br_a = jnp.pad(p["w_br_a"][l].reshape(N_HEADS, V_DIM, d), ((0, 0), (0, HP - V_DIM), (0, 0)))
    w_sp = p["w_spatial"][l]
    b_sp = jnp.repeat(p["b_spatial"][l].T, GM_GROUP_DIM, axis=1)
    return dict(
        w_ada=p["w_ada"][l], b_ada=p["b_ada"][l],
        g_norm_mix=p["g_norm_mix"][l].reshape(1, d), g_norm_ffn=p["g_norm_ffn"][l].reshape(1, d),
        w2=w2, g_q_lat=p["g_q_lat"][l].reshape(1, Q_LORA), wuq2=wuq2,
        g_qk_q=pad_gain(p["g_qk_q"][l]), g_qk_k=pad_gain(p["g_qk_k"][l]),
        g_kv_lat=p["g_kv_lat"][l].reshape(1, KV_LORA),
        wuk=_pad_heads(p["w_uk"][l], QK_NOPE).astype(BF16),
        wkt=p["w_uk"][l].reshape(KV_LORA, N_HEADS * QK_NOPE).T.astype(BF16),
        wukt_pad=jnp.pad(p["w_uk"][l].transpose(1, 2, 0), ((0, 0), (0, HP - QK_NOPE), (0, 0))).astype(BF16),
        wuv=_pad_heads(p["w_uv"][l], V_DIM).astype(BF16),
        w_br_a=w_br_a.reshape(N_HEADS * HP, d).astype(BF16),
        w_conv=p["w_conv"][l], w_br_b=p["w_br_b"][l].astype(BF16),
        g_v_ln=p["g_v_ln"][l].reshape(1, GM_DIM),
        w_sp=w_sp.reshape(GM_GROUPS * CHUNK, CHUNK), b_sp=b_sp,
        sp_coef=jnp.repeat(w_sp[:, 0, 0], GM_GROUP_DIM).reshape(1, GM_DIM),
        w_br_c=p["w_br_c"][l].astype(BF16), w_out=p["w_out"][l].astype(BF16),
        w_router_t=p["w_router"][l].T.astype(BF16),
        b_router_t=p["b_router"][l].reshape(N_EXPERTS, 1),
        w_e_gate=p["w_e_gate"][l].astype(BF16), w_e_up=p["w_e_up"][l].astype(BF16),
        w_e_down=p["w_e_down"][l].astype(BF16),
        w_sh_gate=p["w_sh_gate"][l].astype(BF16), w_sh_up=p["w_sh_up"][l].astype(BF16),
        w_sh_down=p["w_sh_down"][l].astype(BF16),
    )


def _rope_tables(pos):
    inv_freq = ROPE_THETA ** (-jnp.arange(0, QK_ROPE, 2, dtype=F32) / QK_ROPE)
    ang = pos.astype(F32)[:, None] * inv_freq[None, :]
    c, s = jnp.cos(ang), jnp.sin(ang)
    n = pos.shape[0]
    cos = jnp.concatenate([jnp.ones((n, QK_NOPE), F32), c, c, jnp.ones((n, HP - QK_DIM), F32)], axis=1)
    sin = jnp.concatenate([jnp.zeros((n, QK_NOPE), F32), s, s, jnp.zeros((n, HP - QK_DIM), F32)], axis=1)
    return cos, sin


def _split_mod(m):
    return [m[:, i * D_MODEL:(i + 1) * D_MODEL] for i in range(6)]


def _prompt_layer(x, mod, lw, cos, sin):
    b, t, d = x.shape
    shift_m, scale_m, gate_m, shift_f, scale_f, gate_f = [a.reshape(b, 1, d) for a in _split_mod(mod)]
    q, k, v, ckv, krp, tail, cv, bg0, part = _inproj_prompt(x, scale_m, shift_m, lw, cos, sin)
    oa = _flash_attention(q, k, v)
    x1, h2 = _merge(oa, bg0, part, x, gate_m, scale_f, shift_f, lw, TM_MERGE)
    _, idx_t, w_t = _router(h2.reshape(b * t, d), lw, TM_ROUTE)
    y = _moe_sparse(h2, idx_t, w_t, x1, gate_f, lw)
    state = (ckv, krp[:, :, QK_NOPE:QK_DIM], tail[:, 8 - (CONV_W - 1):], cv)
    return y, state


def _sample_layer(layer, x, mod, lw, cos, sin, hist, pool_c, pool_krt, page_table):
    n, d = x.shape
    shift_m, scale_m, gate_m, shift_f, scale_f, gate_f = _split_mod(mod)
    qg, ql, ckv, krp, z, cv, bg0, part = _inproj_sample(x, scale_m, shift_m, lw, cos, sin, hist[:, 0], hist[:, 1])
    head_pad = ((0, 0), (0, HEAD_ROWS - N_HEADS), (0, 0))
    qlat = jnp.pad(ql.transpose(1, 0, 2), head_pad).astype(BF16)
    qr = jnp.pad(qg[:, :, QK_NOPE:QK_DIM].transpose(1, 0, 2), head_pad).astype(BF16)
    kr = krp[:, QK_NOPE:QK_DIM]
    krnew_t = jnp.pad(kr[:, :, None], ((0, 0), (0, 0), (0, LANES - 1)))
    olat = _paged_attention(layer, page_table, lw["wkt"], qlat, qr, ckv.reshape(n, 1, KV_LORA), krnew_t,
                            pool_c, pool_krt)
    oa = _uv_project(olat.reshape(n, N_HEADS * KV_LORA), lw["wuv"])
    g3 = lambda a: a.reshape(1, n, -1)
    x1, h2 = _merge(g3(oa), g3(bg0), g3(part), g3(x), g3(gate_m), g3(scale_f), g3(shift_f), lw, n)
    gates, _, _ = _router(h2.reshape(n, d), lw, n)
    y = _moe(h2, g3(gates), x1, g3(gate_f), lw, n).reshape(n, d)
    state = (ckv.reshape(n, 1, KV_LORA), kr.reshape(n, 1, QK_ROPE),
             jnp.stack([hist[:, 1], z], axis=1), cv.reshape(n, 1, GM_DIM))
    return y, state


def kernel(x_prompt, x_sample, cache_kv_latent, cache_k_rope, state_conv, page_table, c_prompt, c_sample,
           w_ada, b_ada, g_norm_mix, g_norm_ffn, w_in, g_q_lat, w_uq, g_kv_lat, w_uk, w_uv, g_qk_q, g_qk_k,
           w_br_a, w_conv, w_br_b, g_v_ln, w_spatial, b_spatial, w_br_c, w_out, w_router, b_router,
           w_e_gate, w_e_up, w_e_down, w_sh_gate, w_sh_up, w_sh_down):
    params = dict(w_ada=w_ada, b_ada=b_ada, g_norm_mix=g_norm_mix, g_norm_ffn=g_norm_ffn, w_in=w_in,
                  g_q_lat=g_q_lat, w_uq=w_uq, g_kv_lat=g_kv_lat, w_uk=w_uk, w_uv=w_uv, g_qk_q=g_qk_q,
                  g_qk_k=g_qk_k, w_br_a=w_br_a, w_conv=w_conv, w_br_b=w_br_b, g_v_ln=g_v_ln,
                  w_spatial=w_spatial, b_spatial=b_spatial, w_br_c=w_br_c, w_out=w_out, w_router=w_router,
                  b_router=b_router, w_e_gate=w_e_gate, w_e_up=w_e_up, w_e_down=w_e_down,
                  w_sh_gate=w_sh_gate, w_sh_up=w_sh_up, w_sh_down=w_sh_down)
    depth = w_in.shape[0]
    nb, t, d = x_prompt.shape
    ns = x_sample.shape[0]
    assert x_sample.shape[1] == 1 and t % TQ == 0 and t % MOE_BLOCK == 0
    past_len = page_table.shape[1] * PAGE_SIZE

    cos_p, sin_p = _rope_tables(jnp.arange(t))
    cos_s, sin_s = _rope_tables(past_len + jnp.arange(1))
    c_all = jnp.concatenate([c_prompt, c_sample], axis=0)
    cache_krt = jnp.swapaxes(cache_k_rope, 2, 3)

    hp, hs = x_prompt, x_sample.reshape(ns, d)
    st_p, st_s = [], []
    for l in range(depth):
        lw = _prep_layer(l, params)
        mod = _adaln(c_all, lw["w_ada"], lw["b_ada"])
        hp, sp = _prompt_layer(hp, mod[:nb], lw, cos_p, sin_p)
        hs, ss = _sample_layer(l, hs, mod[nb:], lw, cos_s, sin_s, state_conv[l], cache_kv_latent, cache_krt,
                               page_table)
        st_p.append(sp)
        st_s.append(ss)

    stack = lambda sts, i: jnp.stack([s[i] for s in sts])
    return (hp, hs.reshape(ns, 1, d),
            stack(st_p, 0), stack(st_p, 1), stack(st_p, 2), stack(st_p, 3),
            stack(st_s, 0), stack(st_s, 1), stack(st_s, 2), stack(st_s, 3))
```

```python
import functools

import jax
import jax.numpy as jnp
from jax import lax
from jax.experimental import pallas as pl
from jax.experimental.pallas import tpu as pltpu

F32 = jnp.float32
BF16 = jnp.bfloat16

D_MODEL = 1024
N_HEADS = 8
QK_NOPE = 64
QK_ROPE = 32
QK_DIM = QK_NOPE + QK_ROPE
V_DIM = 64
Q_LORA = 384
KV_LORA = 256
ROPE_THETA = 10000.0
CONV_W = 3
CONV_DIM = 512
CHUNK = 128
GM_GROUPS = 8
GM_DIM = 512
GM_GROUP_DIM = GM_DIM // GM_GROUPS
N_EXPERTS = 64
TOP_K = 8
N_EXPERT_GROUPS = 8
GROUP_SIZE = N_EXPERTS // N_EXPERT_GROUPS
TOPK_GROUPS = 4
D_EXPERT = 256
ROUTED_SCALE = 2.5
N_BRANCH = 3
EPS = 1e-6
PAGE_SIZE = 128

LANES = 128
HP = LANES
VMEM_LIMIT = 56 * 1024 * 1024

S_Q = 0
S_KV = S_Q + Q_LORA
S_KRM = S_KV + KV_LORA
S_KRR = S_KRM + HP
S_CB = S_KRR + HP
S_CC = S_CB + CONV_DIM
S_CH = S_CC + CONV_DIM
S_U = S_CH + CONV_DIM
S_V = S_U + GM_DIM
S_G = S_V + GM_DIM
S_END = S_G + N_BRANCH * D_MODEL

TM_IN = 512
TQ = 512
TK = 512
TM_MERGE = 512
TM_ROUTE = 512
TM_MOE = 1024
EXPERTS_PER_STEP = 2
MOE_BLOCK = 256
MOE_CHUNK = 16
EXP_CHUNKS = 32
EXP_BUFFERS = 3
COMBINE_BUFFERS = 3
MOE_ROWS = -(-(MOE_BLOCK * TOP_K + N_EXPERTS * (MOE_CHUNK - 1)) // 512) * 512
PAGES_PER_STEP = 32
PAGE_CHUNKS = 2
PAGE_BUFFERS = 3
HEAD_ROWS = 16


def _cparams(sem):
    return pltpu.CompilerParams(dimension_semantics=sem, vmem_limit_bytes=VMEM_LIMIT)


def _const_spec(shape):
    nd = len(shape)
    return pl.BlockSpec(shape, lambda *_: (0,) * nd, pipeline_mode=pl.Buffered(1))


def _dot(a, b):
    return jnp.dot(a, b, preferred_element_type=F32)


def _rms_rows(x):
    return x * lax.rsqrt(jnp.mean(x * x, axis=-1, keepdims=True) + EPS)


def _silu(x):
    return x * jax.nn.sigmoid(x)


def _adaln_kernel(c_ref, w_ref, b_ref, o_ref):
    s = _silu(c_ref[...]).astype(BF16)
    o_ref[...] = _dot(s, w_ref[...].astype(BF16)) + b_ref[...]


def _adaln(layer, c_all, w_ada, b_ada):
    n, d = c_all.shape
    e = w_ada.shape[2]
    tn = 1536
    return pl.pallas_call(
        _adaln_kernel,
        grid=(e // tn,),
        in_specs=[pl.BlockSpec((n, d), lambda j: (0, 0)),
                  pl.BlockSpec((None, d, tn), lambda j: (layer, 0, j)),
                  pl.BlockSpec((None, 1, tn), lambda j: (layer, 0, j))],
        out_specs=pl.BlockSpec((n, tn), lambda j: (0, j)),
        out_shape=jax.ShapeDtypeStruct((n, e), F32),
        compiler_params=_cparams(("arbitrary",)),
        name="adaln",
    )(c_all, w_ada, b_ada)


def _inproj_common(x_ref, sc_ref, sh_ref, gmix_ref, w2_ref):
    h = (_rms_rows(x_ref[...]) * gmix_ref[...] * (1.0 + sc_ref[...]) + sh_ref[...]).astype(BF16)

    def seg(a, b):
        return _dot(h, w2_ref[:, a:b])

    return seg


def _heads_q(seg, gq_ref, wuq_ref, gqq_ref, cos, sin):
    cqn = (_rms_rows(seg(S_Q, S_KV)) * gq_ref[...]).astype(BF16)
    q2 = _dot(cqn, wuq_ref[...])
    out = []
    for hh in range(N_HEADS):
        qm = q2[:, hh * HP:(hh + 1) * HP]
        qr = q2[:, (N_HEADS + hh) * HP:(N_HEADS + hh + 1) * HP]
        qh = qm * cos + qr * sin
        inv = lax.rsqrt(jnp.sum(qh * qh, axis=-1, keepdims=True) * (1.0 / QK_DIM) + EPS)
        out.append(qh * inv * gqq_ref[...] * (QK_DIM ** -0.5))
    return out


def _latent_kv(seg, gkv_ref, cos, sin):
    ckv = _rms_rows(seg(S_KV, S_KRM)) * gkv_ref[...]
    krp = seg(S_KRM, S_KRR) * cos + seg(S_KRR, S_CB) * sin
    return ckv, krp


def _gates_and_partial(seg, brb_in, brc_in, wbrb_ref, wbrc_ref, bg0_ref, part_ref):
    brb = _dot(brb_in.astype(BF16), wbrb_ref[...])
    brc = _dot(brc_in.astype(BF16), wbrc_ref[...])
    bg = jax.nn.sigmoid(seg(S_G, S_END))
    bg0_ref[...] = bg[:, :D_MODEL].astype(BF16)
    part_ref[...] = bg[:, D_MODEL:2 * D_MODEL] * brb + bg[:, 2 * D_MODEL:] * brc


def _layernorm_rows(v, g):
    xc = v - jnp.mean(v, axis=-1, keepdims=True)
    return xc * lax.rsqrt(jnp.mean(xc * xc, axis=-1, keepdims=True) + EPS) * g


def _inproj_prompt_kernel(x_ref, sc_ref, sh_ref, gmix_ref, w2_ref, gq_ref, wuq_ref, gqq_ref,
                          gkv_ref, wuk_ref, gqk_ref, wuv_ref, cos_ref, sin_ref, wconv_ref,
                          wbrb_ref, gvln_ref, wsp_ref, bsp_ref, wbrc_ref,
                          q_ref, k_ref, v_ref, ckv_ref, kr_ref, tail_ref, cv_ref, bg0_ref, part_ref,
                          zbuf_ref, *, tm):
    t = pl.program_id(1)
    seg = _inproj_common(x_ref, sc_ref, sh_ref, gmix_ref, w2_ref)
    cos = cos_ref[...]
    sin = sin_ref[...]

    qs = _heads_q(seg, gq_ref, wuq_ref, gqq_ref, cos, sin)
    for hh in range(N_HEADS):
        q_ref[hh] = qs[hh].astype(BF16)
    ckv, krp = _latent_kv(seg, gkv_ref, cos, sin)
    ckv_ref[...] = ckv
    kr_ref[...] = krp
    ckv_b = ckv.astype(BF16)
    k2 = _dot(ckv_b, wuk_ref[...])
    v2 = _dot(ckv_b, wuv_ref[...])
    ones_lane = jnp.where(lax.broadcasted_iota(jnp.int32, (1, HP), 1) == V_DIM, 1.0, 0.0)
    for hh in range(N_HEADS):
        kh = k2[:, hh * HP:(hh + 1) * HP] + krp
        inv = lax.rsqrt(jnp.sum(kh * kh, axis=-1, keepdims=True) * (1.0 / QK_DIM) + EPS)
        k_ref[hh] = (kh * inv * gqk_ref[...]).astype(BF16)
        v_ref[hh] = (v2[:, hh * HP:(hh + 1) * HP] + ones_lane).astype(BF16)

    gate_b = seg(S_CB, S_CC)
    z = seg(S_CC, S_CH) * seg(S_CH, S_U)

    @pl.when(t == 0)
    def _():
        zbuf_ref[0:8, :] = jnp.zeros((8, CONV_DIM), F32)

    zbuf_ref[8:8 + tm, :] = z
    z1 = zbuf_ref[7:7 + tm, :]
    z2 = zbuf_ref[6:6 + tm, :]
    wc = wconv_ref[...]
    y = wc[0:1, :] * z2 + wc[1:2, :] * z1 + wc[2:3, :] * z
    zbuf_ref[0:8, :] = z[tm - 8:tm, :]
    tail_ref[...] = z[tm - 8:tm, :]

    u = seg(S_U, S_V)
    vn = _layernorm_rows(seg(S_V, S_G), gvln_ref[...])
    cv_ref[...] = vn[tm - CHUNK:tm, :]
    vnb = vn.astype(BF16)
    rows = lax.broadcasted_iota(jnp.int32, (GM_GROUPS * CHUNK, CHUNK), 0) % CHUNK
    cols = lax.broadcasted_iota(jnp.int32, (GM_GROUPS * CHUNK, CHUNK), 1)
    wsp = jnp.where(cols <= rows, wsp_ref[...], 0.0).astype(BF16)
    lane_grp = lax.broadcasted_iota(jnp.int32, (CHUNK, GM_DIM), 1) // GM_GROUP_DIM
    sgs = []
    for c in range(tm // CHUNK):
        r = _dot(wsp, vnb[c * CHUNK:(c + 1) * CHUNK, :])
        s = bsp_ref[...]
        for g in range(GM_GROUPS):
            s = s + jnp.where(lane_grp == g, r[g * CHUNK:(g + 1) * CHUNK, :], 0.0)
        sgs.append(u[c * CHUNK:(c + 1) * CHUNK, :] * s)
    sg = jnp.concatenate(sgs, axis=0)

    _gates_and_partial(seg, gate_b * y, sg, wbrb_ref, wbrc_ref, bg0_ref, part_ref)


def _inproj_sample_kernel(x_ref, sc_ref, sh_ref, gmix_ref, w2_ref, gq_ref, wuq_ref, gqq_ref,
                          gkv_ref, gqk_ref, wukt_ref, cos_ref, sin_ref, wconv_ref, h0_ref, h1_ref,
                          wbrb_ref, gvln_ref, coef_ref, bias_ref, wbrc_ref,
                          qg_ref, ql_ref, ckv_ref, kr_ref, z_ref, cv_ref, bg0_ref, part_ref):
    seg = _inproj_common(x_ref, sc_ref, sh_ref, gmix_ref, w2_ref)
    cos = cos_ref[...]
    sin = sin_ref[...]
    qs = _heads_q(seg, gq_ref, wuq_ref, gqq_ref, cos, sin)
    for hh in range(N_HEADS):
        qg = qs[hh] * gqk_ref[...]
        qg_ref[hh] = qg
        ql_ref[hh] = _dot(qg.astype(BF16), wukt_ref[hh])
    ckv, krp = _latent_kv(seg, gkv_ref, cos, sin)
    ckv_ref[...] = ckv
    kr_ref[...] = krp

    gate_b = seg(S_CB, S_CC)
    z = seg(S_CC, S_CH) * seg(S_CH, S_U)
    wc = wconv_ref[...]
    y = wc[0:1, :] * h0_ref[...] + wc[1:2, :] * h1_ref[...] + wc[2:3, :] * z
    z_ref[...] = z

    u = seg(S_U, S_V)
    vn = _layernorm_rows(seg(S_V, S_G), gvln_ref[...])
    cv_ref[...] = vn
    sg = u * (vn * coef_ref[...] + bias_ref[...])

    _gates_and_partial(seg, gate_b * y, sg, wbrb_ref, wbrc_ref, bg0_ref, part_ref)


def _inproj_prompt(x, scale_m, shift_m, lw, cos, sin):
    b, t, d = x.shape
    tm = TM_IN
    nt = t // tm
    tok = lambda w: pl.BlockSpec((None, tm, w), lambda i, j: (i, j, 0))
    mod = pl.BlockSpec((None, 1, d), lambda i, j: (i, 0, 0))
    head = pl.BlockSpec((None, N_HEADS, tm, HP), lambda i, j: (i, 0, j, 0))
    in_specs = [
        tok(d), mod, mod, _const_spec((1, d)), _const_spec((d, S_END)),
        _const_spec((1, Q_LORA)), _const_spec((Q_LORA, 2 * N_HEADS * HP)), _const_spec((1, HP)),
        _const_spec((1, KV_LORA)), _const_spec((KV_LORA, N_HEADS * HP)), _const_spec((1, HP)),
        _const_spec((KV_LORA, N_HEADS * HP)),
        pl.BlockSpec((tm, HP), lambda i, j: (j, 0)), pl.BlockSpec((tm, HP), lambda i, j: (j, 0)),
        _const_spec((CONV_W, CONV_DIM)), _const_spec((CONV_DIM, d)), _const_spec((1, GM_DIM)),
        _const_spec((GM_GROUPS * CHUNK, CHUNK)), _const_spec((CHUNK, GM_DIM)), _const_spec((GM_DIM, d)),
    ]
    out_specs = [
        head, head, head, tok(KV_LORA), tok(HP),
        pl.BlockSpec((None, 8, CONV_DIM), lambda i, j: (i, 0, 0)),
        pl.BlockSpec((None, CHUNK, GM_DIM), lambda i, j: (i, 0, 0)),
        tok(d), tok(d),
    ]
    out_shape = [
        jax.ShapeDtypeStruct((b, N_HEADS, t, HP), BF16),
        jax.ShapeDtypeStruct((b, N_HEADS, t, HP), BF16),
        jax.ShapeDtypeStruct((b, N_HEADS, t, HP), BF16),
        jax.ShapeDtypeStruct((b, t, KV_LORA), F32),
        jax.ShapeDtypeStruct((b, t, HP), F32),
        jax.ShapeDtypeStruct((b, 8, CONV_DIM), F32),
        jax.ShapeDtypeStruct((b, CHUNK, GM_DIM), F32),
        jax.ShapeDtypeStruct((b, t, d), BF16),
        jax.ShapeDtypeStruct((b, t, d), F32),
    ]
    return pl.pallas_call(
        functools.partial(_inproj_prompt_kernel, tm=tm),
        grid=(b, nt),
        in_specs=in_specs, out_specs=out_specs, out_shape=out_shape,
        scratch_shapes=[pltpu.VMEM((tm + 8, CONV_DIM), F32)],
        compiler_params=_cparams(("arbitrary", "arbitrary")),
        name="inproj_prompt",
    )(x, scale_m, shift_m, lw["g_norm_mix"], lw["w2"], lw["g_q_lat"], lw["wuq2"], lw["g_qk_q"],
      lw["g_kv_lat"], lw["wuk"], lw["g_qk_k"], lw["wuv"], cos, sin, lw["w_conv"],
      lw["w_br_b"], lw["g_v_ln"], lw["w_sp"], lw["b_sp"], lw["w_br_c"])


def _inproj_sample(x, scale_m, shift_m, lw, cos, sin, hist0, hist1):
    n, d = x.shape
    full = lambda *s: pl.BlockSpec(s, lambda i: (0,) * len(s))
    in_specs = [
        full(n, d), full(n, d), full(n, d), full(1, d), full(d, S_END),
        full(1, Q_LORA), full(Q_LORA, 2 * N_HEADS * HP), full(1, HP),
        full(1, KV_LORA), full(1, HP), full(N_HEADS, HP, KV_LORA), full(1, HP), full(1, HP),
        full(CONV_W, CONV_DIM), full(n, CONV_DIM), full(n, CONV_DIM),
        full(CONV_DIM, d), full(1, GM_DIM), full(1, GM_DIM), full(1, GM_DIM), full(GM_DIM, d),
    ]
    out_specs = [full(N_HEADS, n, HP), full(N_HEADS, n, KV_LORA), full(n, KV_LORA), full(n, HP),
                 full(n, CONV_DIM), full(n, GM_DIM), full(n, d), full(n, d)]
    out_shape = [
        jax.ShapeDtypeStruct((N_HEADS, n, HP), F32),
        jax.ShapeDtypeStruct((N_HEADS, n, KV_LORA), F32),
        jax.ShapeDtypeStruct((n, KV_LORA), F32),
        jax.ShapeDtypeStruct((n, HP), F32),
        jax.ShapeDtypeStruct((n, CONV_DIM), F32),
        jax.ShapeDtypeStruct((n, GM_DIM), F32),
        jax.ShapeDtypeStruct((n, d), BF16),
        jax.ShapeDtypeStruct((n, d), F32),
    ]
    return pl.pallas_call(
        _inproj_sample_kernel,
        grid=(1,),
        in_specs=in_specs, out_specs=out_specs, out_shape=out_shape,
        compiler_params=_cparams(("arbitrary",)),
        name="inproj_sample",
    )(x, scale_m, shift_m, lw["g_norm_mix"], lw["w2"], lw["g_q_lat"], lw["wuq2"], lw["g_qk_q"],
      lw["g_kv_lat"], lw["g_qk_k"], lw["wukt_pad"], cos, sin, lw["w_conv"], hist0, hist1,
      lw["w_br_b"], lw["g_v_ln"], lw["sp_coef"], lw["b_sp"][0:1], lw["w_br_c"])


def _flash_kernel(q_ref, k_ref, v_ref, o_ref, *, tq, tk):
    assert tq == tk
    qi = pl.program_id(2)
    q = q_ref[...]

    def step(j, carry, masked):
        m, acc = carry
        start = pl.multiple_of(j * tk, tk)
        k = k_ref[pl.ds(start, tk), :]
        v = v_ref[pl.ds(start, tk), :]
        s = lax.dot_general(q, k, (((1,), (1,)), ((), ())), preferred_element_type=F32)
        if masked:
            row = lax.broadcasted_iota(jnp.int32, (tq, tk), 0)
            col = lax.broadcasted_iota(jnp.int32, (tq, tk), 1)
            s = jnp.where(col <= row, s, -jnp.inf)
        m_new = jnp.maximum(m, jnp.max(s, axis=-1, keepdims=True))
        alpha = jnp.exp(m - m_new)
        p = jnp.exp(s - m_new)
        acc = alpha * acc + _dot(p.astype(BF16), v)
        return m_new, acc

    m0 = jnp.full((tq, 1), -jnp.inf, F32)
    a0 = jnp.zeros((tq, HP), F32)
    carry = lax.fori_loop(0, qi, lambda j, c: step(j, c, False), (m0, a0))
    _, acc = step(qi, carry, True)
    lane = lax.broadcasted_iota(jnp.int32, (tq, HP), 1)
    l = jnp.sum(jnp.where(lane == V_DIM, acc, 0.0), axis=-1, keepdims=True)
    o_ref[...] = (acc / l).astype(BF16)


def _flash_attention(q, k, v):
    b, h, t, _ = q.shape
    tq, tk = TQ, TK
    return pl.pallas_call(
        functools.partial(_flash_kernel, tq=tq, tk=tk),
        grid=(b, h, t // tq),
        in_specs=[pl.BlockSpec((None, None, tq, HP), lambda i, j, n: (i, j, n, 0)),
                  pl.BlockSpec((None, None, t, HP), lambda i, j, n: (i, j, 0, 0)),
                  pl.BlockSpec((None, None, t, HP), lambda i, j, n: (i, j, 0, 0))],
        out_specs=pl.BlockSpec((None, tq, HP), lambda i, j, n: (i, n, j)),
        out_shape=jax.ShapeDtypeStruct((b, t, h * HP), BF16),
        compiler_params=_cparams(("arbitrary", "arbitrary", "arbitrary")),
        name="flash_prompt",
    )(q, k, v)


def _paged_kernel(pt_ref, wkt_ref, ql_ref, qr_ref, cnew_ref, krnew_ref, poolc_ref, poolk_ref, o_ref,
                  cbuf, kbuf, csem, ksem, lhs_ref, m_ref, l_ref, acc_ref, *, layer, n_pages, n_pg, n_chunk):
    i = pl.program_id(0)
    j = pl.program_id(1)
    nj = pl.num_programs(1)
    nk = N_HEADS * QK_NOPE
    step = i * nj + j
    last_step = pl.num_programs(0) * nj - 1
    ahead = PAGE_BUFFERS - 1
    slot = step % PAGE_BUFFERS
    nxt_slot = (step + ahead) % PAGE_BUFFERS

    def step_base(t):
        t = jnp.minimum(t, last_step)
        return (t // nj) * n_pages + (t % nj) * n_pg

    nxt_base = step_base(step + ahead)

    def page_copies(base, buf_slot, p):
        pid = pt_ref[base + p]
        return (pltpu.make_async_copy(poolc_ref.at[layer, pid], cbuf.at[buf_slot, p], csem.at[buf_slot]),
                pltpu.make_async_copy(poolk_ref.at[layer, pid], kbuf.at[buf_slot, p], ksem.at[buf_slot]))

    def start_page(base, buf_slot, p):
        for n, cp in enumerate(page_copies(base, buf_slot, p)):
            cp.start(priority=(p + n) % 2)

    @pl.when(step == 0)
    def _():
        for a in range(ahead):
            for p in range(n_pg):
                start_page(step_base(a), a, p)

    for p in range(n_pg):
        for cp in page_copies(i * n_pages + j * n_pg, slot, p):
            cp.wait()

    @pl.when(j == 0)
    def _():
        lhs_ref[0:nk, :] = wkt_ref[...]
        lhs_ref[nk:nk + HEAD_ROWS, :] = ql_ref[...]
        m_ref[...] = jnp.full(m_ref.shape, -jnp.inf, F32)
        l_ref[...] = jnp.zeros(l_ref.shape, F32)
        acc_ref[...] = jnp.zeros(acc_ref.shape, F32)

    qr = qr_ref[...]

    def update(cb, krt, valid_keys):
        kx = lax.dot_general(lhs_ref[...], cb, (((1,), (1,)), ((), ())), preferred_element_type=F32)
        sq = kx[:nk, :] * kx[:nk, :]
        ssq = jnp.concatenate([jnp.sum(sq[hh * QK_NOPE:(hh + 1) * QK_NOPE, :], axis=0, keepdims=True)
                               for hh in range(N_HEADS)], axis=0)
        ssq = ssq + jnp.sum(krt * krt, axis=0, keepdims=True)
        inv = lax.rsqrt(ssq * (1.0 / QK_DIM) + EPS)
        st = (kx[nk:, :] + _dot(qr, krt.astype(BF16))) * jnp.concatenate([inv, inv], axis=0)
        if valid_keys is not None:
            kcol = lax.broadcasted_iota(jnp.int32, st.shape, 1)
            st = jnp.where(kcol < valid_keys, st, -jnp.inf)
        m_old = m_ref[...]
        m_new = jnp.maximum(m_old, jnp.max(st, axis=-1, keepdims=True))
        alpha = jnp.exp(m_old - m_new)
        p = jnp.exp(st - m_new)
        l_ref[...] = alpha * l_ref[...] + jnp.sum(p, axis=-1, keepdims=True)
        acc_ref[...] = alpha * acc_ref[...] + _dot(p.astype(BF16), cb)
        m_ref[...] = m_new

    per = n_pg // n_chunk
    for ch in range(n_chunk):
        cb = cbuf[slot, ch * per:(ch + 1) * per].reshape(per * PAGE_SIZE, KV_LORA).astype(BF16)
        krt = jnp.concatenate([kbuf[slot, p] for p in range(ch * per, (ch + 1) * per)], axis=1)
        update(cb, krt, None)
        for p in range(ch * per, (ch + 1) * per):
            start_page(nxt_base, nxt_slot, p)

    @pl.when(j == nj - 1)
    def _():
        first = lax.broadcasted_iota(jnp.int32, (LANES, KV_LORA), 0) == 0
        cn = jnp.where(first, jnp.broadcast_to(cnew_ref[...], (LANES, KV_LORA)), 0.0).astype(BF16)
        update(cn, krnew_ref[...], 1)
        o_ref[...] = (acc_ref[...] / l_ref[...])[:N_HEADS, :]

    @pl.when(step == last_step)
    def _():
        for a in range(1, PAGE_BUFFERS):
            for p in range(n_pg):
                for cp in page_copies(nxt_base, (step + a) % PAGE_BUFFERS, p):
                    cp.wait()


def _paged_attention(layer, page_table, wkt, qlat, qr, cnew, krnew_t, pool_c, pool_krt):
    s, n_pages = page_table.shape
    n_pg = PAGES_PER_STEP
    nj = n_pages // n_pg
    pt = page_table.reshape(-1)
    nk = N_HEADS * QK_NOPE

    per_seq = lambda *shape: pl.BlockSpec((None,) + shape, lambda i, j, pt_ref: (i,) + (0,) * len(shape))
    const = lambda *shape: pl.BlockSpec(shape, lambda i, j, pt_ref: (0,) * len(shape))
    in_specs = [const(nk, KV_LORA), per_seq(HEAD_ROWS, KV_LORA), per_seq(HEAD_ROWS, QK_ROPE),
                per_seq(1, KV_LORA), per_seq(QK_ROPE, LANES),
                pl.BlockSpec(memory_space=pl.ANY), pl.BlockSpec(memory_space=pl.ANY)]
    grid_spec = pltpu.PrefetchScalarGridSpec(
        num_scalar_prefetch=1,
        grid=(s, nj),
        in_specs=in_specs,
        out_specs=per_seq(N_HEADS, KV_LORA),
        scratch_shapes=[pltpu.VMEM((PAGE_BUFFERS, n_pg, PAGE_SIZE, KV_LORA), F32),
                        pltpu.VMEM((PAGE_BUFFERS, n_pg, QK_ROPE, PAGE_SIZE), F32),
                        pltpu.SemaphoreType.DMA((PAGE_BUFFERS,)), pltpu.SemaphoreType.DMA((PAGE_BUFFERS,)),
                        pltpu.VMEM((nk + HEAD_ROWS, KV_LORA), BF16), pltpu.VMEM((HEAD_ROWS, 1), F32),
                        pltpu.VMEM((HEAD_ROWS, 1), F32), pltpu.VMEM((HEAD_ROWS, KV_LORA), F32)],
    )
    return pl.pallas_call(
        functools.partial(_paged_kernel, layer=layer, n_pages=n_pages, n_pg=n_pg, n_chunk=PAGE_CHUNKS),
        grid_spec=grid_spec,
        out_shape=jax.ShapeDtypeStruct((s, N_HEADS, KV_LORA), F32),
        compiler_params=_cparams(("arbitrary", "arbitrary")),
        name="paged_sample",
    )(pt, wkt, qlat, qr, cnew, krnew_t, pool_c, pool_krt)


def _uv_kernel(ol_ref, wuv_ref, o_ref):
    for hh in range(N_HEADS):
        ol = ol_ref[:, hh * KV_LORA:(hh + 1) * KV_LORA].astype(BF16)
        o_ref[:, hh * HP:(hh + 1) * HP] = _dot(ol, wuv_ref[:, hh * HP:(hh + 1) * HP]).astype(BF16)


def _uv_project(olat, wuv):
    n = olat.shape[0]
    full = lambda *s: pl.BlockSpec(s, lambda i: (0,) * len(s))
    return pl.pallas_call(
        _uv_kernel, grid=(1,),
        in_specs=[full(n, N_HEADS * KV_LORA), full(KV_LORA, N_HEADS * HP)],
        out_specs=full(n, N_HEADS * HP),
        out_shape=jax.ShapeDtypeStruct((n, N_HEADS * HP), BF16),
        compiler_params=_cparams(("arbitrary",)),
        name="uv_sample",
    )(olat, wuv)


def _merge_kernel(oa_ref, wbra_ref, bg0_ref, part_ref, x_ref, gm_ref, wout_ref, gffn_ref, scf_ref, shf_ref,
                  x1_ref, h2_ref):
    bra = _dot(oa_ref[...], wbra_ref[...])
    merged = bg0_ref[...].astype(F32) * bra + part_ref[...]
    x1 = x_ref[...] + gm_ref[...] * _dot(merged.astype(BF16), wout_ref[...])
    x1_ref[...] = x1
    h2_ref[...] = (_rms_rows(x1) * gffn_ref[...] * (1.0 + scf_ref[...]) + shf_ref[...]).astype(BF16)


def _merge(oa, bg0, part, x, gate_m, scale_f, shift_f, lw, tm):
    g, t, d = x.shape
    mt = gate_m.shape[1]
    tok = lambda w: pl.BlockSpec((None, tm, w), lambda i, j: (i, j, 0))
    if mt == 1:
        mod = pl.BlockSpec((None, 1, d), lambda i, j: (i, 0, 0))
    else:
        mod = pl.BlockSpec((None, tm, d), lambda i, j: (i, j, 0))
    return pl.pallas_call(
        _merge_kernel,
        grid=(g, t // tm),
        in_specs=[tok(N_HEADS * HP), _const_spec((N_HEADS * HP, d)), tok(d), tok(d), tok(d), mod,
                  _const_spec((d, d)), _const_spec((1, d)), mod, mod],
        out_specs=[tok(d), tok(d)],
        out_shape=[jax.ShapeDtypeStruct((g, t, d), F32), jax.ShapeDtypeStruct((g, t, d), BF16)],
        compiler_params=_cparams(("arbitrary", "arbitrary")),
        name="merge",
    )(oa, lw["w_br_a"], bg0, part, x, gate_m, lw["w_out"], lw["g_norm_ffn"], scale_f, shift_f)


def _router_kernel(h_ref, wrt_ref, brt_ref, g_ref, idx_ref, w_ref):
    tm = h_ref.shape[0]
    logits = lax.dot_general(wrt_ref[...], h_ref[...], (((1,), (1,)), ((), ())), preferred_element_type=F32)
    scores = jax.nn.sigmoid(logits)
    choice = scores + brt_ref[...]
    neg = -jnp.inf

    gsc = []
    for g in range(N_EXPERT_GROUPS):
        xg = choice[g * GROUP_SIZE:(g + 1) * GROUP_SIZE, :]
        m1 = jnp.max(xg, axis=0, keepdims=True)
        cnt = jnp.sum(jnp.where(xg == m1, 1.0, 0.0), axis=0, keepdims=True)
        m2 = jnp.max(jnp.where(xg < m1, xg, neg), axis=0, keepdims=True)
        gsc.append(m1 + jnp.where(cnt >= 2.0, m1, m2))

    cands = []
    for g in range(N_EXPERT_GROUPS):
        rank = jnp.zeros((1, tm), F32)
        for o in range(N_EXPERT_GROUPS):
            if o == g:
                continue
            beats = (gsc[o] > gsc[g]) if o > g else (gsc[o] >= gsc[g])
            rank = rank + jnp.where(beats, 1.0, 0.0)
        keep = rank < float(TOPK_GROUPS)
        cands.append(jnp.where(keep, choice[g * GROUP_SIZE:(g + 1) * GROUP_SIZE, :], neg))
    cand = jnp.concatenate(cands, axis=0)

    ridx = lax.broadcasted_iota(jnp.int32, (N_EXPERTS, tm), 0)
    picked = jnp.zeros((N_EXPERTS, tm), F32)
    idxs, ws = [], []
    for _ in range(TOP_K):
        m = jnp.max(cand, axis=0, keepdims=True)
        idx = jnp.min(jnp.where(cand == m, ridx, N_EXPERTS), axis=0, keepdims=True)
        hit = ridx == idx
        picked = jnp.where(hit, 1.0, picked)
        cand = jnp.where(hit, neg, cand)
        idxs.append(idx)
        ws.append(jnp.sum(jnp.where(hit, scores, 0.0), axis=0, keepdims=True))

    w = jnp.where(picked > 0.0, scores, 0.0)
    norm = ROUTED_SCALE / jnp.sum(w, axis=0, keepdims=True)
    gt = w * norm
    g_ref[...] = jnp.concatenate([gt, jnp.zeros((LANES - N_EXPERTS, tm), F32)], axis=0).T
    idx_ref[...] = jnp.concatenate(idxs, axis=0)
    w_ref[...] = jnp.concatenate(ws, axis=0) * norm


def _router(h2, lw, tm):
    m, d = h2.shape
    return pl.pallas_call(
        _router_kernel,
        grid=(m // tm,),
        in_specs=[pl.BlockSpec((tm, d), lambda i: (i, 0)), _const_spec((N_EXPERTS, d)),
                  _const_spec((N_EXPERTS, 1))],
        out_specs=[pl.BlockSpec((tm, LANES), lambda i: (i, 0)), pl.BlockSpec((TOP_K, tm), lambda i: (0, i)),
                   pl.BlockSpec((TOP_K, tm), lambda i: (0, i))],
        out_shape=[jax.ShapeDtypeStruct((m, LANES), F32), jax.ShapeDtypeStruct((TOP_K, m), jnp.int32),
                   jax.ShapeDtypeStruct((TOP_K, m), F32)],
        compiler_params=_cparams(("arbitrary",)),
        name="router",
    )(h2, lw["w_router_t"], lw["b_router_t"])


def _swiglu(x, wgu):
    gu = _dot(x, wgu)
    return _silu(gu[:, :D_EXPERT]) * gu[:, D_EXPERT:]


def _moe_kernel(h_ref, g_ref, wgu_ref, wd_ref, wsgu_ref, wsd_ref, x1_ref, gf_ref, o_ref, acc_ref, *, epb):
    e = pl.program_id(2)
    h = h_ref[...]
    tm = h.shape[0]

    @pl.when(e == 0)
    def _():
        acc_ref[...] = _dot(_swiglu(h, wsgu_ref[...]).astype(BF16), wsd_ref[...])

    lane = lax.broadcasted_iota(jnp.int32, (tm, LANES), 1)
    gates = g_ref[...]
    for jj in range(epb):
        gcol = jnp.sum(jnp.where(lane == e * epb + jj, gates, 0.0), axis=-1, keepdims=True)
        hh = (_swiglu(h, wgu_ref[jj]) * gcol).astype(BF16)
        acc_ref[...] += _dot(hh, wd_ref[jj])

    @pl.when(e == pl.num_programs(2) - 1)
    def _():
        o_ref[...] = x1_ref[...] + gf_ref[...] * acc_ref[...]


def _moe(h2, gates, x1, gate_f, lw, tm):
    g, t, d = x1.shape
    mt = gate_f.shape[1]
    epb = EXPERTS_PER_STEP
    layer = lw["layer"]
    tok = lambda w: pl.BlockSpec((None, tm, w), lambda i, j, e: (i, j, 0))
    if mt == 1:
        mod = pl.BlockSpec((None, 1, d), lambda i, j, e: (i, 0, 0))
    else:
        mod = pl.BlockSpec((None, tm, d), lambda i, j, e: (i, j, 0))
    return pl.pallas_call(
        functools.partial(_moe_kernel, epb=epb),
        grid=(g, t // tm, N_EXPERTS // epb),
        in_specs=[tok(d), tok(LANES),
                  pl.BlockSpec((None, epb, d, 2 * D_EXPERT), lambda i, j, e: (layer, e, 0, 0)),
                  pl.BlockSpec((None, epb, D_EXPERT, d), lambda i, j, e: (layer, e, 0, 0)),
                  _const_spec((d, 2 * D_EXPERT)), _const_spec((D_EXPERT, d)),
                  tok(d), mod],
        out_specs=tok(d),
        out_shape=jax.ShapeDtypeStruct((g, t, d), F32),
        scratch_shapes=[pltpu.VMEM((tm, d), F32)],
        compiler_params=_cparams(("arbitrary", "arbitrary", "arbitrary")),
        name="moe",
    )(h2, gates, lw["w_e_gu"], lw["w_e_down"], lw["w_sh_gu"], lw["w_sh_down"], x1, gate_f)


def _dispatch_kernel(x_ref, idx_ref, xs_ref, posk_ref, nch_ref, off_ref, *, rc):
    x = x_ref[...]
    idx = idx_ref[...]
    nb = x.shape[0]

    eiota = lax.broadcasted_iota(jnp.int32, (N_EXPERTS, nb), 0)
    hits = [eiota == idx[k:k + 1, :] for k in range(TOP_K)]
    sel = jnp.zeros((N_EXPERTS, nb), F32)
    for hit in hits:
        sel = sel + jnp.where(hit, 1.0, 0.0)
    cnt = jnp.sum(sel, axis=1, keepdims=True)
    nch = jnp.floor((cnt + (MOE_CHUNK - 1.0)) * (1.0 / MOE_CHUNK))
    nch_b = jnp.broadcast_to(nch, (N_EXPERTS, LANES))
    earlier = (lax.broadcasted_iota(jnp.int32, (N_EXPERTS, N_EXPERTS), 1)
               < lax.broadcasted_iota(jnp.int32, (N_EXPERTS, N_EXPERTS), 0))
    off_b = _dot(jnp.where(earlier, 1.0, 0.0).astype(BF16), nch_b.astype(BF16))
    before = (lax.broadcasted_iota(jnp.int32, (nb, nb), 0) < lax.broadcasted_iota(jnp.int32, (nb, nb), 1))
    rank = _dot(sel.astype(BF16), jnp.where(before, 1.0, 0.0).astype(BF16))
    pos = off_b[:, 0:1] * float(MOE_CHUNK) + rank
    posk = jnp.concatenate([jnp.sum(jnp.where(hit, pos, 0.0), axis=0, keepdims=True) for hit in hits],
                           axis=0).astype(jnp.int32)
    posk_ref[...] = posk
    nch_ref[...] = nch_b.astype(jnp.int32)
    off_ref[...] = off_b.astype(jnp.int32)

    for c in range(MOE_ROWS // rc):
        riota = c * rc + lax.broadcasted_iota(jnp.int32, (rc, nb), 0)
        p = jnp.zeros((rc, nb), F32)
        for k in range(TOP_K):
            p = jnp.where(riota == posk[k:k + 1, :], 1.0, p)
        xs_ref[c * rc:(c + 1) * rc, :] = _dot(p.astype(BF16), x).astype(BF16)


def _dispatch(h2, idx_t):
    nblk, nb, d = h2.shape
    small = pl.BlockSpec((None, N_EXPERTS, LANES), lambda i: (i, 0, 0))
    return pl.pallas_call(
        functools.partial(_dispatch_kernel, rc=512),
        grid=(nblk,),
        in_specs=[pl.BlockSpec((None, nb, d), lambda i: (i, 0, 0)),
                  pl.BlockSpec((TOP_K, nb), lambda i: (0, i))],
        out_specs=[pl.BlockSpec((None, MOE_ROWS, d), lambda i: (i, 0, 0)),
                   pl.BlockSpec((None, TOP_K, nb), lambda i: (i, 0, 0)), small, small],
        out_shape=[jax.ShapeDtypeStruct((nblk, MOE_ROWS, d), BF16),
                   jax.ShapeDtypeStruct((nblk, TOP_K, nb), jnp.int32),
                   jax.ShapeDtypeStruct((nblk, N_EXPERTS, LANES), jnp.int32),
                   jax.ShapeDtypeStruct((nblk, N_EXPERTS, LANES), jnp.int32)],
        compiler_params=_cparams(("arbitrary",)),
        name="moe_dispatch",
    )(h2, idx_t)


def _start_alternating(copies):
    for n, cp in enumerate(copies):
        cp.start(priority=n % 2)


def _expert_kernel(te_ref, src_ref, nact_ref, xs_ref, wgu_ref, wd_ref, ys_ref, xbuf, sem):
    t = pl.program_id(0)
    nact = nact_ref[0]
    ahead = EXP_BUFFERS - 1
    slot = t % EXP_BUFFERS

    def copies(tile, buf_slot):
        return [pltpu.make_async_copy(xs_ref.at[src_ref[tile * EXP_CHUNKS + c]], xbuf.at[buf_slot, c],
                                      sem.at[buf_slot]) for c in range(EXP_CHUNKS)]

    @pl.when(t == 0)
    def _():
        for a in range(ahead):
            @pl.when(a < nact)
            def _():
                _start_alternating(copies(a, a))

    @pl.when(t + ahead < nact)
    def _():
        _start_alternating(copies(t + ahead, (t + ahead) % EXP_BUFFERS))

    @pl.when(t < nact)
    def _():
        for cp in copies(t, slot):
            cp.wait()
        x = xbuf[slot].reshape(EXP_CHUNKS * MOE_CHUNK, xbuf.shape[-1])
        ys_ref[...] = _dot(_swiglu(x, wgu_ref[...]).astype(BF16), wd_ref[...]).astype(BF16)

    @pl.when(t >= nact)
    def _():
        ys_ref[...] = jnp.zeros(ys_ref.shape, BF16)


def _experts(xs_chunks, tile_expert, src_chunk, n_active, lw):
    d = xs_chunks.shape[-1]
    n_tiles = tile_expert.shape[0]
    tm = EXP_CHUNKS * MOE_CHUNK
    layer = lw["layer"]
    grid_spec = pltpu.PrefetchScalarGridSpec(
        num_scalar_prefetch=3,
        grid=(n_tiles,),
        in_specs=[pl.BlockSpec(memory_space=pl.ANY),
                  pl.BlockSpec((None, None, d, 2 * D_EXPERT), lambda t, te, src, na: (layer, te[t], 0, 0)),
                  pl.BlockSpec((None, None, D_EXPERT, d), lambda t, te, src, na: (layer, te[t], 0, 0))],
        out_specs=pl.BlockSpec((tm, d), lambda t, te, src, na: (t, 0)),
        scratch_shapes=[pltpu.VMEM((EXP_BUFFERS, EXP_CHUNKS, MOE_CHUNK, d), BF16),
                        pltpu.SemaphoreType.DMA((EXP_BUFFERS,))],
    )
    return pl.pallas_call(
        _expert_kernel,
        grid_spec=grid_spec,
        out_shape=jax.ShapeDtypeStruct((n_tiles * tm, d), BF16),
        compiler_params=_cparams(("arbitrary",)),
        name="moe_experts",
    )(tile_expert, src_chunk, n_active, xs_chunks, lw["w_e_gu"], lw["w_e_down"])


def _combine_kernel(src_ref, ys_ref, pos_ref, w_ref, h_ref, x1_ref, gf_ref, wsgu_ref, wsd_ref,
                    o_ref, ybuf, sem, *, cc):
    b = pl.program_id(0)
    last = pl.num_programs(0) - 1
    ahead = COMBINE_BUFFERS - 1
    slot = b % COMBINE_BUFFERS
    nxt_slot = (b + ahead) % COMBINE_BUFFERS
    nxt = jnp.minimum(b + ahead, last)
    nch = MOE_ROWS // MOE_CHUNK

    def copies(blk, buf_slot, lo=0, hi=nch):
        return [pltpu.make_async_copy(ys_ref.at[src_ref[blk * nch + c]], ybuf.at[buf_slot, c], sem.at[buf_slot])
                for c in range(lo, hi)]

    @pl.when(b == 0)
    def _():
        for a in range(ahead):
            _start_alternating(copies(jnp.minimum(a, last), a))

    acc = _dot(_swiglu(h_ref[...], wsgu_ref[...]).astype(BF16), wsd_ref[...])

    for cp in copies(b, slot):
        cp.wait()
    pos = pos_ref[...]
    w = w_ref[...]
    nb = pos.shape[0]
    per = cc // MOE_CHUNK
    for c in range(MOE_ROWS // cc):
        liota = c * cc + lax.broadcasted_iota(jnp.int32, (nb, cc), 1)
        pw = jnp.zeros((nb, cc), F32)
        for k in range(TOP_K):
            pw = jnp.where(liota == pos[:, k:k + 1], w[:, k:k + 1], pw)
        y = ybuf[slot, c * per:(c + 1) * per].reshape(cc, ybuf.shape[-1])
        acc = acc + _dot(pw.astype(BF16), y)
        _start_alternating(copies(nxt, nxt_slot, c * per, (c + 1) * per))
    o_ref[...] = x1_ref[...] + gf_ref[...] * acc

    @pl.when(b == last)
    def _():
        for a in range(1, COMBINE_BUFFERS):
            for cp in copies(nxt, (b + a) % COMBINE_BUFFERS):
                cp.wait()


def _combine(ys_chunks, src_chunk, posk, wk, h2, x1, gate_f, lw, blocks_per_seq):
    nblk, nb, d = h2.shape
    nch = MOE_ROWS // MOE_CHUNK
    blk = lambda w: pl.BlockSpec((None, nb, w), lambda i, src: (i, 0, 0))
    const = lambda *s: pl.BlockSpec(s, lambda i, src: (0,) * len(s))
    grid_spec = pltpu.PrefetchScalarGridSpec(
        num_scalar_prefetch=1,
        grid=(nblk,),
        in_specs=[pl.BlockSpec(memory_space=pl.ANY), blk(TOP_K), blk(TOP_K), blk(d), blk(d),
                  pl.BlockSpec((None, 1, d), lambda i, src: (i // blocks_per_seq, 0, 0)),
                  const(d, 2 * D_EXPERT), const(D_EXPERT, d)],
        out_specs=blk(d),
        scratch_shapes=[pltpu.VMEM((COMBINE_BUFFERS, nch, MOE_CHUNK, d), BF16),
                        pltpu.SemaphoreType.DMA((COMBINE_BUFFERS,))],
    )
    return pl.pallas_call(
        functools.partial(_combine_kernel, cc=512),
        grid_spec=grid_spec,
        out_shape=jax.ShapeDtypeStruct((nblk, nb, d), F32),
        compiler_params=_cparams(("arbitrary",)),
        name="moe_combine",
    )(src_chunk, ys_chunks, posk, wk, h2, x1, gate_f, lw["w_sh_gu"], lw["w_sh_down"])


def _excl_cumsum(x, axis):
    n = x.shape[axis]
    earlier = jnp.arange(n)[:, None] > jnp.arange(n)[None, :]
    xm = jnp.moveaxis(x, axis, -1)
    out = jnp.sum(jnp.where(earlier, xm[..., None, :], 0), axis=-1)
    return jnp.moveaxis(out, -1, axis)


def _pick(onehot, table):
    return jnp.sum(jnp.where(onehot, table, 0), axis=-1)


def _moe_plan(nch, off, n_tokens):
    nblk = nch.shape[0]
    i32 = jnp.int32
    cpb = MOE_ROWS // MOE_CHUNK
    experts = jnp.arange(N_EXPERTS, dtype=i32)
    tot = jnp.sum(nch, axis=1)
    ech = jnp.sum(nch, axis=0)
    tiles_e = (ech + EXP_CHUNKS - 1) // EXP_CHUNKS
    tile_start = _excl_cumsum(tiles_e, 0)
    tile_end = tile_start + tiles_e
    n_active = jnp.sum(tiles_e)
    cumb = _excl_cumsum(nch, 0)

    n_tiles = _max_expert_tiles(n_tokens, nblk)
    tid = jnp.arange(n_tiles, dtype=i32)
    te = jnp.sum((tid[:, None] >= tile_end[None, :]).astype(i32), axis=1)
    te_last = jnp.sum((n_active - 1 >= tile_end).astype(i32))
    te = jnp.where(tid < n_active, te, te_last)
    is_e = te[:, None] == experts[None, :]
    ts_t = _pick(is_e, tile_start[None, :])
    ech_t = _pick(is_e, ech[None, :])
    ends_t = _pick(is_e[:, None, :], (cumb + nch)[None, :, :])
    off_t = _pick(is_e[:, None, :], off[None, :, :])
    cumb_t = _pick(is_e[:, None, :], cumb[None, :, :])
    c = (tid - ts_t)[:, None] * EXP_CHUNKS + jnp.arange(EXP_CHUNKS, dtype=i32)[None, :]
    valid = jnp.logical_and(c < ech_t[:, None], (tid < n_active)[:, None])
    sb = jnp.minimum(jnp.sum((c[:, :, None] >= ends_t[:, None, :]).astype(i32), axis=2), nblk - 1)
    is_b = sb[:, :, None] == jnp.arange(nblk, dtype=i32)[None, None, :]
    src_e = jnp.where(valid, sb * cpb + _pick(is_b, off_t[:, None, :]) + c - _pick(is_b, cumb_t[:, None, :]), 0)

    j = jnp.arange(cpb, dtype=i32)
    ce = jnp.minimum(jnp.sum((j[None, :, None] >= (off + nch)[:, None, :]).astype(i32), axis=2), N_EXPERTS - 1)
    is_ce = ce[:, :, None] == experts[None, None, :]
    g = (_pick(is_ce, tile_start[None, None, :]) * EXP_CHUNKS + _pick(is_ce, cumb[:, None, :])
         + j[None, :] - _pick(is_ce, off[:, None, :]))
    src_c = jnp.where(j[None, :] < tot[:, None], g, 0)
    return te, src_e.reshape(-1), n_active.reshape(1), src_c.reshape(-1)


def _max_expert_tiles(t, nblk):
    max_chunks = t * TOP_K // MOE_CHUNK + nblk * N_EXPERTS
    return max_chunks // EXP_CHUNKS + N_EXPERTS


def _moe_sparse(h2, idx_t, w_t, x1, gate_f, lw):
    b, t, d = x1.shape
    nblk = b * t // MOE_BLOCK
    h2b = h2.reshape(nblk, MOE_BLOCK, d)
    xs, posk_t, nch, off = _dispatch(h2b, idx_t)
    te, src_e, n_active, src_c = _moe_plan(nch[:, :, 0], off[:, :, 0], b * t)
    ys = _experts(xs.reshape(nblk * MOE_ROWS // MOE_CHUNK, MOE_CHUNK, d), te, src_e, n_active, lw)
    wk = w_t.reshape(TOP_K, nblk, MOE_BLOCK).transpose(1, 2, 0)
    out = _combine(ys.reshape(-1, MOE_CHUNK, d), src_c, posk_t.transpose(0, 2, 1), wk,
                   h2b, x1.reshape(nblk, MOE_BLOCK, d), gate_f, lw, t // MOE_BLOCK)
    return out.reshape(b, t, d)


def _pad_heads(w, width):
    pad = [(0, 0)] * (w.ndim - 1) + [(0, HP - width)]
    wp = jnp.pad(w, pad)
    return wp.reshape(w.shape[:-2] + (w.shape[-2] * HP,))


def _rot_cols(w):
    half = QK_ROPE // 2
    return jnp.concatenate([-w[..., half:], w[..., :half]], axis=-1)


def _prep_layer(l, p):
    d = D_MODEL
    w_in = p["w_in"][l]
    off_kv = Q_LORA
    off_kr = off_kv + KV_LORA
    off_cb = off_kr + QK_ROPE
    off_g = off_cb + 3 * CONV_DIM + 2 * GM_DIM
    wkr = w_in[:, off_kr:off_cb]
    lead = jnp.zeros((d, QK_NOPE), F32)
    trail = jnp.zeros((d, HP - QK_DIM), F32)
    w2 = jnp.concatenate([
        w_in[:, :off_kr],
        lead, wkr, trail,
        lead, _rot_cols(wkr), trail,
        w_in[:, off_cb:],
    ], axis=1).astype(BF16)
    assert w2.shape[1] == S_END and off_g + N_BRANCH * d == w_in.shape[1]

    wuq = p["w_uq"][l]
    wuq_rot = jnp.concatenate([jnp.zeros_like(wuq[..., :QK_NOPE]), _rot_cols(wuq[..., QK_NOPE:])], axis=-1)
    wuq2 = jnp.concatenate([_pad_heads(wuq, QK_DIM), _pad_heads(wuq_rot, QK_DIM)], axis=-1).astype(BF16)

    def pad_gain(gq):
        return jnp.pad(gq, (0, HP - QK_DIM)).reshape(1, HP)

    w_br_a = jnp.pad(p["w_br_a"][l].reshape(N_HEADS, V_DIM, d), ((0, 0), (0, HP - V_DIM), (0, 0)))
    w_sp = p["w_spatial"][l]
    b_sp = jnp.repeat(p["b_spatial"][l].T, GM_GROUP_DIM, axis=1)
    return dict(
        g_norm_mix=p["g_norm_mix"][l].reshape(1, d), g_norm_ffn=p["g_norm_ffn"][l].reshape(1, d),
        w2=w2, g_q_lat=p["g_q_lat"][l].reshape(1, Q_LORA), wuq2=wuq2,
        g_qk_q=pad_gain(p["g_qk_q"][l]), g_qk_k=pad_gain(p["g_qk_k"][l]),
        g_kv_lat=p["g_kv_lat"][l].reshape(1, KV_LORA),
        wuk=_pad_heads(p["w_uk"][l], QK_NOPE).astype(BF16),
        wkt=p["w_uk"][l].reshape(KV_LORA, N_HEADS * QK_NOPE).T.astype(BF16),
        wukt_pad=jnp.pad(p["w_uk"][l].transpose(1, 2, 0), ((0, 0), (0, HP - QK_NOPE), (0, 0))).astype(BF16),
        wuv=_pad_heads(p["w_uv"][l], V_DIM).astype(BF16),
        w_br_a=w_br_a.reshape(N_HEADS * HP, d).astype(BF16),
        w_conv=p["w_conv"][l], w_br_b=p["w_br_b"][l].astype(BF16),
        g_v_ln=p["g_v_ln"][l].reshape(1, GM_DIM),
        w_sp=w_sp.reshape(GM_GROUPS * CHUNK, CHUNK), b_sp=b_sp,
        sp_coef=jnp.repeat(w_sp[:, 0, 0], GM_GROUP_DIM).reshape(1, GM_DIM),
        w_br_c=p["w_br_c"][l].astype(BF16), w_out=p["w_out"][l].astype(BF16),
        w_router_t=p["w_router"][l].T.astype(BF16),
        b_router_t=p["b_router"][l].reshape(N_EXPERTS, 1),
        w_sh_gu=jnp.concatenate([p["w_sh_gate"][l], p["w_sh_up"][l]], axis=-1).astype(BF16),
        w_sh_down=p["w_sh_down"][l].astype(BF16),
    )


def _rope_tables(pos):
    inv_freq = ROPE_THETA ** (-jnp.arange(0, QK_ROPE, 2, dtype=F32) / QK_ROPE)
    ang = pos.astype(F32)[:, None] * inv_freq[None, :]
    c, s = jnp.cos(ang), jnp.sin(ang)
    n = pos.shape[0]
    cos = jnp.concatenate([jnp.ones((n, QK_NOPE), F32), c, c, jnp.ones((n, HP - QK_DIM), F32)], axis=1)
    sin = jnp.concatenate([jnp.zeros((n, QK_NOPE), F32), s, s, jnp.zeros((n, HP - QK_DIM), F32)], axis=1)
    return cos, sin


def _split_mod(m):
    return [m[:, i * D_MODEL:(i + 1) * D_MODEL] for i in range(6)]


def _prompt_layer(x, mod, lw, cos, sin):
    b, t, d = x.shape
    shift_m, scale_m, gate_m, shift_f, scale_f, gate_f = [a.reshape(b, 1, d) for a in _split_mod(mod)]
    q, k, v, ckv, krp, tail, cv, bg0, part = _inproj_prompt(x, scale_m, shift_m, lw, cos, sin)
    oa = _flash_attention(q, k, v)
    x1, h2 = _merge(oa, bg0, part, x, gate_m, scale_f, shift_f, lw, TM_MERGE)
    _, idx_t, w_t = _router(h2.reshape(b * t, d), lw, TM_ROUTE)
    y = _moe_sparse(h2, idx_t, w_t, x1, gate_f, lw)
    state = (ckv, krp[:, :, QK_NOPE:QK_DIM], tail[:, 8 - (CONV_W - 1):], cv)
    return y, state


def _sample_layer(layer, x, mod, lw, cos, sin, hist, pool_c, pool_krt, page_table):
    n, d = x.shape
    shift_m, scale_m, gate_m, shift_f, scale_f, gate_f = _split_mod(mod)
    qg, ql, ckv, krp, z, cv, bg0, part = _inproj_sample(x, scale_m, shift_m, lw, cos, sin, hist[:, 0], hist[:, 1])
    head_pad = ((0, 0), (0, HEAD_ROWS - N_HEADS), (0, 0))
    qlat = jnp.pad(ql.transpose(1, 0, 2), head_pad).astype(BF16)
    qr = jnp.pad(qg[:, :, QK_NOPE:QK_DIM].transpose(1, 0, 2), head_pad).astype(BF16)
    kr = krp[:, QK_NOPE:QK_DIM]
    krnew_t = jnp.pad(kr[:, :, None], ((0, 0), (0, 0), (0, LANES - 1)))
    olat = _paged_attention(layer, page_table, lw["wkt"], qlat, qr, ckv.reshape(n, 1, KV_LORA), krnew_t,
                            pool_c, pool_krt)
    oa = _uv_project(olat.reshape(n, N_HEADS * KV_LORA), lw["wuv"])
    g3 = lambda a: a.reshape(1, n, -1)
    x1, h2 = _merge(g3(oa), g3(bg0), g3(part), g3(x), g3(gate_m), g3(scale_f), g3(shift_f), lw, n)
    gates, _, _ = _router(h2.reshape(n, d), lw, n)
    y = _moe(h2, g3(gates), x1, g3(gate_f), lw, n).reshape(n, d)
    state = (ckv.reshape(n, 1, KV_LORA), kr.reshape(n, 1, QK_ROPE),
             jnp.stack([hist[:, 1], z], axis=1), cv.reshape(n, 1, GM_DIM))
    return y, state


def kernel(x_prompt, x_sample, cache_kv_latent, cache_k_rope, state_conv, page_table, c_prompt, c_sample,
           w_ada, b_ada, g_norm_mix, g_norm_ffn, w_in, g_q_lat, w_uq, g_kv_lat, w_uk, w_uv, g_qk_q, g_qk_k,
           w_br_a, w_conv, w_br_b, g_v_ln, w_spatial, b_spatial, w_br_c, w_out, w_router, b_router,
           w_e_gate, w_e_up, w_e_down, w_sh_gate, w_sh_up, w_sh_down):
    params = dict(w_ada=w_ada, b_ada=b_ada, g_norm_mix=g_norm_mix, g_norm_ffn=g_norm_ffn, w_in=w_in,
                  g_q_lat=g_q_lat, w_uq=w_uq, g_kv_lat=g_kv_lat, w_uk=w_uk, w_uv=w_uv, g_qk_q=g_qk_q,
                  g_qk_k=g_qk_k, w_br_a=w_br_a, w_conv=w_conv, w_br_b=w_br_b, g_v_ln=g_v_ln,
                  w_spatial=w_spatial, b_spatial=b_spatial, w_br_c=w_br_c, w_out=w_out, w_router=w_router,
                  b_router=b_router, w_e_gate=w_e_gate, w_e_up=w_e_up, w_e_down=w_e_down,
                  w_sh_gate=w_sh_gate, w_sh_up=w_sh_up, w_sh_down=w_sh_down)
    depth = w_in.shape[0]
    nb, t, d = x_prompt.shape
    ns = x_sample.shape[0]
    assert x_sample.shape[1] == 1 and t % TQ == 0 and t % MOE_BLOCK == 0
    past_len = page_table.shape[1] * PAGE_SIZE

    cos_p, sin_p = _rope_tables(jnp.arange(t))
    cos_s, sin_s = _rope_tables(past_len + jnp.arange(1))
    c_all = jnp.concatenate([c_prompt, c_sample], axis=0)
    cache_krt = jnp.swapaxes(cache_k_rope, 2, 3)
    w_e_gu = jnp.concatenate([w_e_gate, w_e_up], axis=-1).astype(BF16)
    w_e_dn = w_e_down.astype(BF16)
    b_ada3 = b_ada.reshape(depth, 1, -1)

    hp, hs = x_prompt, x_sample.reshape(ns, d)
    st_p, st_s = [], []
    for l in range(depth):
        lw = _prep_layer(l, params)
        lw.update(layer=l, w_e_gu=w_e_gu, w_e_down=w_e_dn)
        mod = _adaln(l, c_all, w_ada, b_ada3)
        hp, sp = _prompt_layer(hp, mod[:nb], lw, cos_p, sin_p)
        hs, ss = _sample_layer(l, hs, mod[nb:], lw, cos_s, sin_s, state_conv[l], cache_kv_latent, cache_krt,
                               page_table)
        st_p.append(sp)
        st_s.append(ss)

    stack = lambda sts, i: jnp.stack([s[i] for s in sts])
    return (hp, hs.reshape(ns, 1, d),
            stack(st_p, 0), stack(st_p, 1), stack(st_p, 2), stack(st_p, 3),
            stack(st_s, 0), stack(st_s, 1), stack(st_s, 2), stack(st_s, 3))
```

```python
import functools

import jax
import jax.numpy as jnp
from jax import lax
from jax.experimental import pallas as pl
from jax.experimental.pallas import tpu as pltpu

F32 = jnp.float32
BF16 = jnp.bfloat16

D_MODEL = 1024
N_HEADS = 8
QK_NOPE = 64
QK_ROPE = 32
QK_DIM = QK_NOPE + QK_ROPE
V_DIM = 64
Q_LORA = 384
KV_LORA = 256
ROPE_THETA = 10000.0
CONV_W = 3
CONV_DIM = 512
CHUNK = 128
GM_GROUPS = 8
GM_DIM = 512
GM_GROUP_DIM = GM_DIM // GM_GROUPS
N_EXPERTS = 64
TOP_K = 8
N_EXPERT_GROUPS = 8
GROUP_SIZE = N_EXPERTS // N_EXPERT_GROUPS
TOPK_GROUPS = 4
D_EXPERT = 256
ROUTED_SCALE = 2.5
N_BRANCH = 3
EPS = 1e-6
PAGE_SIZE = 128

LANES = 128
HP = LANES
VMEM_LIMIT = 56 * 1024 * 1024

S_Q = 0
S_KV = S_Q + Q_LORA
S_KRM = S_KV + KV_LORA
S_KRR = S_KRM + HP
S_CB = S_KRR + HP
S_CC = S_CB + CONV_DIM
S_CH = S_CC + CONV_DIM
S_U = S_CH + CONV_DIM
S_V = S_U + GM_DIM
S_G = S_V + GM_DIM
S_END = S_G + N_BRANCH * D_MODEL

TM_IN = 512
TQ = 512
TK = 512
TM_MERGE = 512
TM_ROUTE = 512
TM_MOE = 1024
EXPERTS_PER_STEP = 2
MOE_BLOCK = 256
MOE_CHUNK = 16
EXP_CHUNKS = 32
EXP_BUFFERS = 3
COMBINE_BUFFERS = 3
MOE_ROWS = -(-(MOE_BLOCK * TOP_K + N_EXPERTS * (MOE_CHUNK - 1)) // 512) * 512
PAGES_PER_STEP = 32
PAGE_CHUNKS = 2
PAGE_BUFFERS = 3
HEAD_ROWS = 16


def _cparams(sem):
    return pltpu.CompilerParams(dimension_semantics=sem, vmem_limit_bytes=VMEM_LIMIT)


def _const_spec(shape):
    nd = len(shape)
    return pl.BlockSpec(shape, lambda *_: (0,) * nd, pipeline_mode=pl.Buffered(1))


def _dot(a, b):
    return jnp.dot(a, b, preferred_element_type=F32)


def _rms_rows(x):
    return x * lax.rsqrt(jnp.mean(x * x, axis=-1, keepdims=True) + EPS)


def _silu(x):
    return x * jax.nn.sigmoid(x)


def _adaln_kernel(c_ref, w_ref, b_ref, o_ref):
    s = _silu(c_ref[...]).astype(BF16)
    o_ref[...] = _dot(s, w_ref[...].astype(BF16)) + b_ref[...]


def _adaln(layer, c_all, w_ada, b_ada):
    n, d = c_all.shape
    e = w_ada.shape[2]
    tn = 1536
    return pl.pallas_call(
        _adaln_kernel,
        grid=(e // tn,),
        in_specs=[pl.BlockSpec((n, d), lambda j: (0, 0)),
                  pl.BlockSpec((None, d, tn), lambda j: (layer, 0, j)),
                  pl.BlockSpec((None, 1, tn), lambda j: (layer, 0, j))],
        out_specs=pl.BlockSpec((n, tn), lambda j: (0, j)),
        out_shape=jax.ShapeDtypeStruct((n, e), F32),
        compiler_params=_cparams(("arbitrary",)),
        name="adaln",
    )(c_all, w_ada, b_ada)


def _inproj_common(x_ref, sc_ref, sh_ref, gmix_ref, w2_ref):
    h = (_rms_rows(x_ref[...]) * gmix_ref[...] * (1.0 + sc_ref[...]) + sh_ref[...]).astype(BF16)

    def seg(a, b):
        return _dot(h, w2_ref[:, a:b])

    return seg


def _heads_q(seg, gq_ref, wuq_ref, gqq_ref, cos, sin):
    cqn = (_rms_rows(seg(S_Q, S_KV)) * gq_ref[...]).astype(BF16)
    q2 = _dot(cqn, wuq_ref[...])
    out = []
    for hh in range(N_HEADS):
        qm = q2[:, hh * HP:(hh + 1) * HP]
        qr = q2[:, (N_HEADS + hh) * HP:(N_HEADS + hh + 1) * HP]
        qh = qm * cos + qr * sin
        inv = lax.rsqrt(jnp.sum(qh * qh, axis=-1, keepdims=True) * (1.0 / QK_DIM) + EPS)
        out.append(qh * inv * gqq_ref[...] * (QK_DIM ** -0.5))
    return out


def _latent_kv(seg, gkv_ref, cos, sin):
    ckv = _rms_rows(seg(S_KV, S_KRM)) * gkv_ref[...]
    krp = seg(S_KRM, S_KRR) * cos + seg(S_KRR, S_CB) * sin
    return ckv, krp


def _gates_and_partial(seg, brb_in, brc_in, wbrb_ref, wbrc_ref, bg0_ref, part_ref):
    brb = _dot(brb_in.astype(BF16), wbrb_ref[...])
    brc = _dot(brc_in.astype(BF16), wbrc_ref[...])
    bg = jax.nn.sigmoid(seg(S_G, S_END))
    bg0_ref[...] = bg[:, :D_MODEL].astype(BF16)
    part_ref[...] = bg[:, D_MODEL:2 * D_MODEL] * brb + bg[:, 2 * D_MODEL:] * brc


def _layernorm_rows(v, g):
    xc = v - jnp.mean(v, axis=-1, keepdims=True)
    return xc * lax.rsqrt(jnp.mean(xc * xc, axis=-1, keepdims=True) + EPS) * g


def _inproj_prompt_kernel(x_ref, sc_ref, sh_ref, gmix_ref, w2_ref, gq_ref, wuq_ref, gqq_ref,
                          gkv_ref, wuk_ref, gqk_ref, wuv_ref, cos_ref, sin_ref, wconv_ref,
                          wbrb_ref, gvln_ref, wsp_ref, bsp_ref, wbrc_ref,
                          q_ref, k_ref, v_ref, ckv_ref, kr_ref, tail_ref, cv_ref, bg0_ref, part_ref,
                          zbuf_ref, *, tm):
    t = pl.program_id(1)
    seg = _inproj_common(x_ref, sc_ref, sh_ref, gmix_ref, w2_ref)
    cos = cos_ref[...]
    sin = sin_ref[...]

    qs = _heads_q(seg, gq_ref, wuq_ref, gqq_ref, cos, sin)
    for hh in range(N_HEADS):
        q_ref[hh] = qs[hh].astype(BF16)
    ckv, krp = _latent_kv(seg, gkv_ref, cos, sin)
    ckv_ref[...] = ckv
    kr_ref[...] = krp
    ckv_b = ckv.astype(BF16)
    k2 = _dot(ckv_b, wuk_ref[...])
    v2 = _dot(ckv_b, wuv_ref[...])
    ones_lane = jnp.where(lax.broadcasted_iota(jnp.int32, (1, HP), 1) == V_DIM, 1.0, 0.0)
    for hh in range(N_HEADS):
        kh = k2[:, hh * HP:(hh + 1) * HP] + krp
        inv = lax.rsqrt(jnp.sum(kh * kh, axis=-1, keepdims=True) * (1.0 / QK_DIM) + EPS)
        k_ref[hh] = (kh * inv * gqk_ref[...]).astype(BF16)
        v_ref[hh] = (v2[:, hh * HP:(hh + 1) * HP] + ones_lane).astype(BF16)

    gate_b = seg(S_CB, S_CC)
    z = seg(S_CC, S_CH) * seg(S_CH, S_U)

    @pl.when(t == 0)
    def _():
        zbuf_ref[0:8, :] = jnp.zeros((8, CONV_DIM), F32)

    zbuf_ref[8:8 + tm, :] = z
    z1 = zbuf_ref[7:7 + tm, :]
    z2 = zbuf_ref[6:6 + tm, :]
    wc = wconv_ref[...]
    y = wc[0:1, :] * z2 + wc[1:2, :] * z1 + wc[2:3, :] * z
    zbuf_ref[0:8, :] = z[tm - 8:tm, :]
    tail_ref[...] = z[tm - 8:tm, :]

    u = seg(S_U, S_V)
    vn = _layernorm_rows(seg(S_V, S_G), gvln_ref[...])
    cv_ref[...] = vn[tm - CHUNK:tm, :]
    vnb = vn.astype(BF16)
    rows = lax.broadcasted_iota(jnp.int32, (GM_GROUPS * CHUNK, CHUNK), 0) % CHUNK
    cols = lax.broadcasted_iota(jnp.int32, (GM_GROUPS * CHUNK, CHUNK), 1)
    wsp = jnp.where(cols <= rows, wsp_ref[...], 0.0).astype(BF16)
    lane_grp = lax.broadcasted_iota(jnp.int32, (CHUNK, GM_DIM), 1) // GM_GROUP_DIM
    sgs = []
    for c in range(tm // CHUNK):
        r = _dot(wsp, vnb[c * CHUNK:(c + 1) * CHUNK, :])
        s = bsp_ref[...]
        for g in range(GM_GROUPS):
            s = s + jnp.where(lane_grp == g, r[g * CHUNK:(g + 1) * CHUNK, :], 0.0)
        sgs.append(u[c * CHUNK:(c + 1) * CHUNK, :] * s)
    sg = jnp.concatenate(sgs, axis=0)

    _gates_and_partial(seg, gate_b * y, sg, wbrb_ref, wbrc_ref, bg0_ref, part_ref)


def _inproj_sample_kernel(x_ref, sc_ref, sh_ref, gmix_ref, w2_ref, gq_ref, wuq_ref, gqq_ref,
                          gkv_ref, gqk_ref, wukt_ref, cos_ref, sin_ref, wconv_ref, h0_ref, h1_ref,
                          wbrb_ref, gvln_ref, coef_ref, bias_ref, wbrc_ref,
                          qg_ref, ql_ref, ckv_ref, kr_ref, z_ref, cv_ref, bg0_ref, part_ref):
    seg = _inproj_common(x_ref, sc_ref, sh_ref, gmix_ref, w2_ref)
    cos = cos_ref[...]
    sin = sin_ref[...]
    qs = _heads_q(seg, gq_ref, wuq_ref, gqq_ref, cos, sin)
    for hh in range(N_HEADS):
        qg = qs[hh] * gqk_ref[...]
        qg_ref[hh] = qg
        ql_ref[hh] = _dot(qg.astype(BF16), wukt_ref[hh])
    ckv, krp = _latent_kv(seg, gkv_ref, cos, sin)
    ckv_ref[...] = ckv
    kr_ref[...] = krp

    gate_b = seg(S_CB, S_CC)
    z = seg(S_CC, S_CH) * seg(S_CH, S_U)
    wc = wconv_ref[...]
    y = wc[0:1, :] * h0_ref[...] + wc[1:2, :] * h1_ref[...] + wc[2:3, :] * z
    z_ref[...] = z

    u = seg(S_U, S_V)
    vn = _layernorm_rows(seg(S_V, S_G), gvln_ref[...])
    cv_ref[...] = vn
    sg = u * (vn * coef_ref[...] + bias_ref[...])

    _gates_and_partial(seg, gate_b * y, sg, wbrb_ref, wbrc_ref, bg0_ref, part_ref)


def _inproj_prompt(x, scale_m, shift_m, lw, cos, sin):
    b, t, d = x.shape
    tm = TM_IN
    nt = t // tm
    tok = lambda w: pl.BlockSpec((None, tm, w), lambda i, j: (i, j, 0))
    mod = pl.BlockSpec((None, 1, d), lambda i, j: (i, 0, 0))
    head = pl.BlockSpec((None, N_HEADS, tm, HP), lambda i, j: (i, 0, j, 0))
    in_specs = [
        tok(d), mod, mod, _const_spec((1, d)), _const_spec((d, S_END)),
        _const_spec((1, Q_LORA)), _const_spec((Q_LORA, 2 * N_HEADS * HP)), _const_spec((1, HP)),
        _const_spec((1, KV_LORA)), _const_spec((KV_LORA, N_HEADS * HP)), _const_spec((1, HP)),
        _const_spec((KV_LORA, N_HEADS * HP)),
        pl.BlockSpec((tm, HP), lambda i, j: (j, 0)), pl.BlockSpec((tm, HP), lambda i, j: (j, 0)),
        _const_spec((CONV_W, CONV_DIM)), _const_spec((CONV_DIM, d)), _const_spec((1, GM_DIM)),
        _const_spec((GM_GROUPS * CHUNK, CHUNK)), _const_spec((CHUNK, GM_DIM)), _const_spec((GM_DIM, d)),
    ]
    out_specs = [
        head, head, head, tok(KV_LORA), tok(HP),
        pl.BlockSpec((None, 8, CONV_DIM), lambda i, j: (i, 0, 0)),
        pl.BlockSpec((None, CHUNK, GM_DIM), lambda i, j: (i, 0, 0)),
        tok(d), tok(d),
    ]
    out_shape = [
        jax.ShapeDtypeStruct((b, N_HEADS, t, HP), BF16),
        jax.ShapeDtypeStruct((b, N_HEADS, t, HP), BF16),
        jax.ShapeDtypeStruct((b, N_HEADS, t, HP), BF16),
        jax.ShapeDtypeStruct((b, t, KV_LORA), F32),
        jax.ShapeDtypeStruct((b, t, HP), F32),
        jax.ShapeDtypeStruct((b, 8, CONV_DIM), F32),
        jax.ShapeDtypeStruct((b, CHUNK, GM_DIM), F32),
        jax.ShapeDtypeStruct((b, t, d), BF16),
        jax.ShapeDtypeStruct((b, t, d), F32),
    ]
    return pl.pallas_call(
        functools.partial(_inproj_prompt_kernel, tm=tm),
        grid=(b, nt),
        in_specs=in_specs, out_specs=out_specs, out_shape=out_shape,
        scratch_shapes=[pltpu.VMEM((tm + 8, CONV_DIM), F32)],
        compiler_params=_cparams(("arbitrary", "arbitrary")),
        name="inproj_prompt",
    )(x, scale_m, shift_m, lw["g_norm_mix"], lw["w2"], lw["g_q_lat"], lw["wuq2"], lw["g_qk_q"],
      lw["g_kv_lat"], lw["wuk"], lw["g_qk_k"], lw["wuv"], cos, sin, lw["w_conv"],
      lw["w_br_b"], lw["g_v_ln"], lw["w_sp"], lw["b_sp"], lw["w_br_c"])


def _inproj_sample(x, scale_m, shift_m, lw, cos, sin, hist0, hist1):
    n, d = x.shape
    full = lambda *s: pl.BlockSpec(s, lambda i: (0,) * len(s))
    in_specs = [
        full(n, d), full(n, d), full(n, d), full(1, d), full(d, S_END),
        full(1, Q_LORA), full(Q_LORA, 2 * N_HEADS * HP), full(1, HP),
        full(1, KV_LORA), full(1, HP), full(N_HEADS, HP, KV_LORA), full(1, HP), full(1, HP),
        full(CONV_W, CONV_DIM), full(n, CONV_DIM), full(n, CONV_DIM),
        full(CONV_DIM, d), full(1, GM_DIM), full(1, GM_DIM), full(1, GM_DIM), full(GM_DIM, d),
    ]
    out_specs = [full(N_HEADS, n, HP), full(N_HEADS, n, KV_LORA), full(n, KV_LORA), full(n, HP),
                 full(n, CONV_DIM), full(n, GM_DIM), full(n, d), full(n, d)]
    out_shape = [
        jax.ShapeDtypeStruct((N_HEADS, n, HP), F32),
        jax.ShapeDtypeStruct((N_HEADS, n, KV_LORA), F32),
        jax.ShapeDtypeStruct((n, KV_LORA), F32),
        jax.ShapeDtypeStruct((n, HP), F32),
        jax.ShapeDtypeStruct((n, CONV_DIM), F32),
        jax.ShapeDtypeStruct((n, GM_DIM), F32),
        jax.ShapeDtypeStruct((n, d), BF16),
        jax.ShapeDtypeStruct((n, d), F32),
    ]
    return pl.pallas_call(
        _inproj_sample_kernel,
        grid=(1,),
        in_specs=in_specs, out_specs=out_specs, out_shape=out_shape,
        compiler_params=_cparams(("arbitrary",)),
        name="inproj_sample",
    )(x, scale_m, shift_m, lw["g_norm_mix"], lw["w2"], lw["g_q_lat"], lw["wuq2"], lw["g_qk_q"],
      lw["g_kv_lat"], lw["g_qk_k"], lw["wukt_pad"], cos, sin, lw["w_conv"], hist0, hist1,
      lw["w_br_b"], lw["g_v_ln"], lw["sp_coef"], lw["b_sp"][0:1], lw["w_br_c"])


def _flash_kernel(q_ref, k_ref, v_ref, o_ref, *, tq, tk):
    assert tq == tk
    qi = pl.program_id(2)
    q = q_ref[...]

    def step(j, carry, masked):
        m, acc = carry
        start = pl.multiple_of(j * tk, tk)
        k = k_ref[pl.ds(start, tk), :]
        v = v_ref[pl.ds(start, tk), :]
        s = lax.dot_general(q, k, (((1,), (1,)), ((), ())), preferred_element_type=F32)
        if masked:
            row = lax.broadcasted_iota(jnp.int32, (tq, tk), 0)
            col = lax.broadcasted_iota(jnp.int32, (tq, tk), 1)
            s = jnp.where(col <= row, s, -jnp.inf)
        m_new = jnp.maximum(m, jnp.max(s, axis=-1, keepdims=True))
        alpha = jnp.exp(m - m_new)
        p = jnp.exp(s - m_new)
        acc = alpha * acc + _dot(p.astype(BF16), v)
        return m_new, acc

    m0 = jnp.full((tq, 1), -jnp.inf, F32)
    a0 = jnp.zeros((tq, HP), F32)
    carry = lax.fori_loop(0, qi, lambda j, c: step(j, c, False), (m0, a0))
    _, acc = step(qi, carry, True)
    lane = lax.broadcasted_iota(jnp.int32, (tq, HP), 1)
    l = jnp.sum(jnp.where(lane == V_DIM, acc, 0.0), axis=-1, keepdims=True)
    o_ref[...] = (acc / l).astype(BF16)


def _flash_attention(q, k, v):
    b, h, t, _ = q.shape
    tq, tk = TQ, TK
    return pl.pallas_call(
        functools.partial(_flash_kernel, tq=tq, tk=tk),
        grid=(b, h, t // tq),
        in_specs=[pl.BlockSpec((None, None, tq, HP), lambda i, j, n: (i, j, n, 0)),
                  pl.BlockSpec((None, None, t, HP), lambda i, j, n: (i, j, 0, 0)),
                  pl.BlockSpec((None, None, t, HP), lambda i, j, n: (i, j, 0, 0))],
        out_specs=pl.BlockSpec((None, tq, HP), lambda i, j, n: (i, n, j)),
        out_shape=jax.ShapeDtypeStruct((b, t, h * HP), BF16),
        compiler_params=_cparams(("arbitrary", "arbitrary", "arbitrary")),
        name="flash_prompt",
    )(q, k, v)


def _paged_kernel(pt_ref, wkt_ref, ql_ref, qr_ref, cnew_ref, krnew_ref, poolc_ref, poolk_ref, o_ref,
                  cbuf, kbuf, csem, ksem, lhs_ref, m_ref, l_ref, acc_ref, *, layer, n_pages, n_pg, n_chunk):
    i = pl.program_id(0)
    j = pl.program_id(1)
    nj = pl.num_programs(1)
    nk = N_HEADS * QK_NOPE
    step = i * nj + j
    last_step = pl.num_programs(0) * nj - 1
    ahead = PAGE_BUFFERS - 1
    slot = step % PAGE_BUFFERS
    nxt_slot = (step + ahead) % PAGE_BUFFERS

    def step_base(t):
        t = jnp.minimum(t, last_step)
        return (t // nj) * n_pages + (t % nj) * n_pg

    nxt_base = step_base(step + ahead)

    def page_copies(base, buf_slot, p):
        pid = pt_ref[base + p]
        return (pltpu.make_async_copy(poolc_ref.at[layer, pid], cbuf.at[buf_slot, p], csem.at[buf_slot]),
                pltpu.make_async_copy(poolk_ref.at[layer, pid], kbuf.at[buf_slot, p], ksem.at[buf_slot]))

    def start_page(base, buf_slot, p):
        for cp in page_copies(base, buf_slot, p):
            cp.start()

    @pl.when(step == 0)
    def _():
        for a in range(ahead):
            for p in range(n_pg):
                start_page(step_base(a), a, p)

    for p in range(n_pg):
        for cp in page_copies(i * n_pages + j * n_pg, slot, p):
            cp.wait()

    @pl.when(j == 0)
    def _():
        lhs_ref[0:nk, :] = wkt_ref[...]
        lhs_ref[nk:nk + HEAD_ROWS, :] = ql_ref[...]
        m_ref[...] = jnp.full(m_ref.shape, -jnp.inf, F32)
        l_ref[...] = jnp.zeros(l_ref.shape, F32)
        acc_ref[...] = jnp.zeros(acc_ref.shape, F32)

    qr = qr_ref[...]

    def update(cb, krt, valid_keys):
        kx = lax.dot_general(lhs_ref[...], cb, (((1,), (1,)), ((), ())), preferred_element_type=F32)
        sq = kx[:nk, :] * kx[:nk, :]
        ssq = jnp.sum(sq.reshape(QK_NOPE, N_HEADS, sq.shape[-1]), axis=0)
        ssq = ssq + jnp.sum(krt * krt, axis=0, keepdims=True)
        inv = lax.rsqrt(ssq * (1.0 / QK_DIM) + EPS)
        st = (kx[nk:, :] + _dot(qr, krt.astype(BF16))) * jnp.concatenate([inv, inv], axis=0)
        if valid_keys is not None:
            kcol = lax.broadcasted_iota(jnp.int32, st.shape, 1)
            st = jnp.where(kcol < valid_keys, st, -jnp.inf)
        m_old = m_ref[...]
        m_new = jnp.maximum(m_old, jnp.max(st, axis=-1, keepdims=True))
        alpha = jnp.exp(m_old - m_new)
        p = jnp.exp(st - m_new)
        l_ref[...] = alpha * l_ref[...] + jnp.sum(p, axis=-1, keepdims=True)
        acc_ref[...] = alpha * acc_ref[...] + _dot(p.astype(BF16), cb)
        m_ref[...] = m_new

    per = n_pg // n_chunk
    for ch in range(n_chunk):
        cb = cbuf[slot, ch * per:(ch + 1) * per].reshape(per * PAGE_SIZE, KV_LORA).astype(BF16)
        krt = jnp.concatenate([kbuf[slot, p] for p in range(ch * per, (ch + 1) * per)], axis=1)
        update(cb, krt, None)
        for p in range(ch * per, (ch + 1) * per):
            start_page(nxt_base, nxt_slot, p)

    @pl.when(j == nj - 1)
    def _():
        first = lax.broadcasted_iota(jnp.int32, (LANES, KV_LORA), 0) == 0
        cn = jnp.where(first, jnp.broadcast_to(cnew_ref[...], (LANES, KV_LORA)), 0.0).astype(BF16)
        update(cn, krnew_ref[...], 1)
        o_ref[...] = (acc_ref[...] / l_ref[...])[:N_HEADS, :]

    @pl.when(step == last_step)
    def _():
        for a in range(1, PAGE_BUFFERS):
            for p in range(n_pg):
                for cp in page_copies(nxt_base, (step + a) % PAGE_BUFFERS, p):
                    cp.wait()


def _paged_attention(layer, page_table, wkt, qlat, qr, cnew, krnew_t, pool_c, pool_krt):
    s, n_pages = page_table.shape
    n_pg = PAGES_PER_STEP
    nj = n_pages // n_pg
    pt = page_table.reshape(-1)
    nk = N_HEADS * QK_NOPE

    per_seq = lambda *shape: pl.BlockSpec((None,) + shape, lambda i, j, pt_ref: (i,) + (0,) * len(shape))
    const = lambda *shape: pl.BlockSpec(shape, lambda i, j, pt_ref: (0,) * len(shape))
    in_specs = [const(nk, KV_LORA), per_seq(HEAD_ROWS, KV_LORA), per_seq(HEAD_ROWS, QK_ROPE),
                per_seq(1, KV_LORA), per_seq(QK_ROPE, LANES),
                pl.BlockSpec(memory_space=pl.ANY), pl.BlockSpec(memory_space=pl.ANY)]
    grid_spec = pltpu.PrefetchScalarGridSpec(
        num_scalar_prefetch=1,
        grid=(s, nj),
        in_specs=in_specs,
        out_specs=per_seq(N_HEADS, KV_LORA),
        scratch_shapes=[pltpu.VMEM((PAGE_BUFFERS, n_pg, PAGE_SIZE, KV_LORA), F32),
                        pltpu.VMEM((PAGE_BUFFERS, n_pg, QK_ROPE, PAGE_SIZE), F32),
                        pltpu.SemaphoreType.DMA((PAGE_BUFFERS,)), pltpu.SemaphoreType.DMA((PAGE_BUFFERS,)),
                        pltpu.VMEM((nk + HEAD_ROWS, KV_LORA), BF16), pltpu.VMEM((HEAD_ROWS, 1), F32),
                        pltpu.VMEM((HEAD_ROWS, 1), F32), pltpu.VMEM((HEAD_ROWS, KV_LORA), F32)],
    )
    return pl.pallas_call(
        functools.partial(_paged_kernel, layer=layer, n_pages=n_pages, n_pg=n_pg, n_chunk=PAGE_CHUNKS),
        grid_spec=grid_spec,
        out_shape=jax.ShapeDtypeStruct((s, N_HEADS, KV_LORA), F32),
        compiler_params=_cparams(("arbitrary", "arbitrary")),
        name="paged_sample",
    )(pt, wkt, qlat, qr, cnew, krnew_t, pool_c, pool_krt)


def _uv_kernel(ol_ref, wuv_ref, o_ref):
    for hh in range(N_HEADS):
        ol = ol_ref[:, hh * KV_LORA:(hh + 1) * KV_LORA].astype(BF16)
        o_ref[:, hh * HP:(hh + 1) * HP] = _dot(ol, wuv_ref[:, hh * HP:(hh + 1) * HP]).astype(BF16)


def _uv_project(olat, wuv):
    n = olat.shape[0]
    full = lambda *s: pl.BlockSpec(s, lambda i: (0,) * len(s))
    return pl.pallas_call(
        _uv_kernel, grid=(1,),
        in_specs=[full(n, N_HEADS * KV_LORA), full(KV_LORA, N_HEADS * HP)],
        out_specs=full(n, N_HEADS * HP),
        out_shape=jax.ShapeDtypeStruct((n, N_HEADS * HP), BF16),
        compiler_params=_cparams(("arbitrary",)),
        name="uv_sample",
    )(olat, wuv)


def _merge_kernel(oa_ref, wbra_ref, bg0_ref, part_ref, x_ref, gm_ref, wout_ref, gffn_ref, scf_ref, shf_ref,
                  x1_ref, h2_ref):
    bra = _dot(oa_ref[...], wbra_ref[...])
    merged = bg0_ref[...].astype(F32) * bra + part_ref[...]
    x1 = x_ref[...] + gm_ref[...] * _dot(merged.astype(BF16), wout_ref[...])
    x1_ref[...] = x1
    h2_ref[...] = (_rms_rows(x1) * gffn_ref[...] * (1.0 + scf_ref[...]) + shf_ref[...]).astype(BF16)


def _merge(oa, bg0, part, x, gate_m, scale_f, shift_f, lw, tm):
    g, t, d = x.shape
    mt = gate_m.shape[1]
    tok = lambda w: pl.BlockSpec((None, tm, w), lambda i, j: (i, j, 0))
    if mt == 1:
        mod = pl.BlockSpec((None, 1, d), lambda i, j: (i, 0, 0))
    else:
        mod = pl.BlockSpec((None, tm, d), lambda i, j: (i, j, 0))
    return pl.pallas_call(
        _merge_kernel,
        grid=(g, t // tm),
        in_specs=[tok(N_HEADS * HP), _const_spec((N_HEADS * HP, d)), tok(d), tok(d), tok(d), mod,
                  _const_spec((d, d)), _const_spec((1, d)), mod, mod],
        out_specs=[tok(d), tok(d)],
        out_shape=[jax.ShapeDtypeStruct((g, t, d), F32), jax.ShapeDtypeStruct((g, t, d), BF16)],
        compiler_params=_cparams(("arbitrary", "arbitrary")),
        name="merge",
    )(oa, lw["w_br_a"], bg0, part, x, gate_m, lw["w_out"], lw["g_norm_ffn"], scale_f, shift_f)


def _router_kernel(h_ref, wrt_ref, brt_ref, g_ref, idx_ref, w_ref):
    tm = h_ref.shape[0]
    logits = lax.dot_general(wrt_ref[...], h_ref[...], (((1,), (1,)), ((), ())), preferred_element_type=F32)
    scores = jax.nn.sigmoid(logits)
    choice = scores + brt_ref[...]
    neg = -jnp.inf

    gsc = []
    for g in range(N_EXPERT_GROUPS):
        xg = choice[g * GROUP_SIZE:(g + 1) * GROUP_SIZE, :]
        m1 = jnp.max(xg, axis=0, keepdims=True)
        cnt = jnp.sum(jnp.where(xg == m1, 1.0, 0.0), axis=0, keepdims=True)
        m2 = jnp.max(jnp.where(xg < m1, xg, neg), axis=0, keepdims=True)
        gsc.append(m1 + jnp.where(cnt >= 2.0, m1, m2))

    cands = []
    for g in range(N_EXPERT_GROUPS):
        rank = jnp.zeros((1, tm), F32)
        for o in range(N_EXPERT_GROUPS):
            if o == g:
                continue
            beats = (gsc[o] > gsc[g]) if o > g else (gsc[o] >= gsc[g])
            rank = rank + jnp.where(beats, 1.0, 0.0)
        keep = rank < float(TOPK_GROUPS)
        cands.append(jnp.where(keep, choice[g * GROUP_SIZE:(g + 1) * GROUP_SIZE, :], neg))
    cand = jnp.concatenate(cands, axis=0)

    ridx = lax.broadcasted_iota(jnp.int32, (N_EXPERTS, tm), 0)
    picked = jnp.zeros((N_EXPERTS, tm), F32)
    idxs, ws = [], []
    for _ in range(TOP_K):
        m = jnp.max(cand, axis=0, keepdims=True)
        idx = jnp.min(jnp.where(cand == m, ridx, N_EXPERTS), axis=0, keepdims=True)
        hit = ridx == idx
        picked = jnp.where(hit, 1.0, picked)
        cand = jnp.where(hit, neg, cand)
        idxs.append(idx)
        ws.append(jnp.sum(jnp.where(hit, scores, 0.0), axis=0, keepdims=True))

    w = jnp.where(picked > 0.0, scores, 0.0)
    norm = ROUTED_SCALE / jnp.sum(w, axis=0, keepdims=True)
    gt = w * norm
    g_ref[...] = jnp.concatenate([gt, jnp.zeros((LANES - N_EXPERTS, tm), F32)], axis=0).T
    idx_ref[...] = jnp.concatenate(idxs, axis=0)
    w_ref[...] = jnp.concatenate(ws, axis=0) * norm


def _router(h2, lw, tm):
    m, d = h2.shape
    return pl.pallas_call(
        _router_kernel,
        grid=(m // tm,),
        in_specs=[pl.BlockSpec((tm, d), lambda i: (i, 0)), _const_spec((N_EXPERTS, d)),
                  _const_spec((N_EXPERTS, 1))],
        out_specs=[pl.BlockSpec((tm, LANES), lambda i: (i, 0)), pl.BlockSpec((TOP_K, tm), lambda i: (0, i)),
                   pl.BlockSpec((TOP_K, tm), lambda i: (0, i))],
        out_shape=[jax.ShapeDtypeStruct((m, LANES), F32), jax.ShapeDtypeStruct((TOP_K, m), jnp.int32),
                   jax.ShapeDtypeStruct((TOP_K, m), F32)],
        compiler_params=_cparams(("arbitrary",)),
        name="router",
    )(h2, lw["w_router_t"], lw["b_router_t"])


def _swiglu(x, wgu):
    gu = _dot(x, wgu)
    return _silu(gu[:, :D_EXPERT]) * gu[:, D_EXPERT:]


def _expert_ffn(x, wg, wu, wd):
    h = _silu(_dot(x, wg.astype(BF16))) * _dot(x, wu.astype(BF16))
    return h, wd.astype(BF16)


def _moe_kernel(h_ref, g_ref, wg_ref, wu_ref, wd_ref, wsgu_ref, wsd_ref, x1_ref, gf_ref, o_ref, acc_ref, *, epb):
    e = pl.program_id(2)
    h = h_ref[...]
    tm = h.shape[0]

    @pl.when(e == 0)
    def _():
        acc_ref[...] = _dot(_swiglu(h, wsgu_ref[...]).astype(BF16), wsd_ref[...])

    lane = lax.broadcasted_iota(jnp.int32, (tm, LANES), 1)
    gates = g_ref[...]
    for jj in range(epb):
        gcol = jnp.sum(jnp.where(lane == e * epb + jj, gates, 0.0), axis=-1, keepdims=True)
        hh, wd = _expert_ffn(h, wg_ref[jj], wu_ref[jj], wd_ref[jj])
        acc_ref[...] += _dot((hh * gcol).astype(BF16), wd)

    @pl.when(e == pl.num_programs(2) - 1)
    def _():
        o_ref[...] = x1_ref[...] + gf_ref[...] * acc_ref[...]


def _moe(h2, gates, x1, gate_f, lw, tm):
    g, t, d = x1.shape
    mt = gate_f.shape[1]
    epb = EXPERTS_PER_STEP
    layer = lw["layer"]
    tok = lambda w: pl.BlockSpec((None, tm, w), lambda i, j, e: (i, j, 0))
    if mt == 1:
        mod = pl.BlockSpec((None, 1, d), lambda i, j, e: (i, 0, 0))
    else:
        mod = pl.BlockSpec((None, tm, d), lambda i, j, e: (i, j, 0))
    return pl.pallas_call(
        functools.partial(_moe_kernel, epb=epb),
        grid=(g, t // tm, N_EXPERTS // epb),
        in_specs=[tok(d), tok(LANES),
                  pl.BlockSpec((None, epb, d, D_EXPERT), lambda i, j, e: (layer, e, 0, 0)),
                  pl.BlockSpec((None, epb, d, D_EXPERT), lambda i, j, e: (layer, e, 0, 0)),
                  pl.BlockSpec((None, epb, D_EXPERT, d), lambda i, j, e: (layer, e, 0, 0)),
                  _const_spec((d, 2 * D_EXPERT)), _const_spec((D_EXPERT, d)),
                  tok(d), mod],
        out_specs=tok(d),
        out_shape=jax.ShapeDtypeStruct((g, t, d), F32),
        scratch_shapes=[pltpu.VMEM((tm, d), F32)],
        compiler_params=_cparams(("arbitrary", "arbitrary", "arbitrary")),
        name="moe",
    )(h2, gates, lw["w_e_gate"], lw["w_e_up"], lw["w_e_down"], lw["w_sh_gu"], lw["w_sh_down"], x1, gate_f)


def _dispatch_kernel(x_ref, idx_ref, xs_ref, posk_ref, nch_ref, off_ref, *, rc):
    x = x_ref[...]
    idx = idx_ref[...]
    nb = x.shape[0]

    eiota = lax.broadcasted_iota(jnp.int32, (N_EXPERTS, nb), 0)
    hits = [eiota == idx[k:k + 1, :] for k in range(TOP_K)]
    sel = jnp.zeros((N_EXPERTS, nb), F32)
    for hit in hits:
        sel = sel + jnp.where(hit, 1.0, 0.0)
    cnt = jnp.sum(sel, axis=1, keepdims=True)
    nch = jnp.floor((cnt + (MOE_CHUNK - 1.0)) * (1.0 / MOE_CHUNK))
    nch_b = jnp.broadcast_to(nch, (N_EXPERTS, LANES))
    earlier = (lax.broadcasted_iota(jnp.int32, (N_EXPERTS, N_EXPERTS), 1)
               < lax.broadcasted_iota(jnp.int32, (N_EXPERTS, N_EXPERTS), 0))
    off_b = _dot(jnp.where(earlier, 1.0, 0.0).astype(BF16), nch_b.astype(BF16))
    before = (lax.broadcasted_iota(jnp.int32, (nb, nb), 0) < lax.broadcasted_iota(jnp.int32, (nb, nb), 1))
    rank = _dot(sel.astype(BF16), jnp.where(before, 1.0, 0.0).astype(BF16))
    pos = off_b[:, 0:1] * float(MOE_CHUNK) + rank
    posk = jnp.concatenate([jnp.sum(jnp.where(hit, pos, 0.0), axis=0, keepdims=True) for hit in hits],
                           axis=0).astype(jnp.int32)
    posk_ref[...] = posk
    nch_ref[...] = nch_b.astype(jnp.int32)
    off_ref[...] = off_b.astype(jnp.int32)

    posk16 = posk.astype(jnp.int16)
    for c in range(MOE_ROWS // rc):
        riota = (c * rc + lax.broadcasted_iota(jnp.int32, (rc, nb), 0)).astype(jnp.int16)
        p = jnp.zeros((rc, nb), BF16)
        for k in range(TOP_K):
            p = jnp.where(riota == posk16[k:k + 1, :], jnp.ones((), BF16), p)
        xs_ref[c * rc:(c + 1) * rc, :] = _dot(p, x).astype(BF16)


def _dispatch(h2, idx_t):
    nblk, nb, d = h2.shape
    small = pl.BlockSpec((None, N_EXPERTS, LANES), lambda i: (i, 0, 0))
    return pl.pallas_call(
        functools.partial(_dispatch_kernel, rc=512),
        grid=(nblk,),
        in_specs=[pl.BlockSpec((None, nb, d), lambda i: (i, 0, 0)),
                  pl.BlockSpec((TOP_K, nb), lambda i: (0, i))],
        out_specs=[pl.BlockSpec((None, MOE_ROWS, d), lambda i: (i, 0, 0)),
                   pl.BlockSpec((None, TOP_K, nb), lambda i: (i, 0, 0)), small, small],
        out_shape=[jax.ShapeDtypeStruct((nblk, MOE_ROWS, d), BF16),
                   jax.ShapeDtypeStruct((nblk, TOP_K, nb), jnp.int32),
                   jax.ShapeDtypeStruct((nblk, N_EXPERTS, LANES), jnp.int32),
                   jax.ShapeDtypeStruct((nblk, N_EXPERTS, LANES), jnp.int32)],
        compiler_params=_cparams(("arbitrary",)),
        name="moe_dispatch",
    )(h2, idx_t)


def _start_all(copies):
    for cp in copies:
        cp.start()


def _expert_kernel(te_ref, src_ref, nact_ref, xs_ref, wg_ref, wu_ref, wd_ref, ys_ref, xbuf, sem):
    t = pl.program_id(0)
    nact = nact_ref[0]
    ahead = EXP_BUFFERS - 1
    slot = t % EXP_BUFFERS

    def copies(tile, buf_slot):
        return [pltpu.make_async_copy(xs_ref.at[src_ref[tile * EXP_CHUNKS + c]], xbuf.at[buf_slot, c],
                                      sem.at[buf_slot]) for c in range(EXP_CHUNKS)]

    @pl.when(t == 0)
    def _():
        for a in range(ahead):
            @pl.when(a < nact)
            def _():
                _start_all(copies(a, a))

    @pl.when(t + ahead < nact)
    def _():
        _start_all(copies(t + ahead, (t + ahead) % EXP_BUFFERS))

    @pl.when(t < nact)
    def _():
        for cp in copies(t, slot):
            cp.wait()
        x = xbuf[slot].reshape(EXP_CHUNKS * MOE_CHUNK, xbuf.shape[-1])
        hh, wd = _expert_ffn(x, wg_ref[...], wu_ref[...], wd_ref[...])
        ys_ref[...] = _dot(hh.astype(BF16), wd).astype(BF16)

    @pl.when(t >= nact)
    def _():
        ys_ref[...] = jnp.zeros(ys_ref.shape, BF16)


def _experts(xs_chunks, tile_expert, src_chunk, n_active, lw):
    d = xs_chunks.shape[-1]
    n_tiles = tile_expert.shape[0]
    tm = EXP_CHUNKS * MOE_CHUNK
    layer = lw["layer"]
    grid_spec = pltpu.PrefetchScalarGridSpec(
        num_scalar_prefetch=3,
        grid=(n_tiles,),
        in_specs=[pl.BlockSpec(memory_space=pl.ANY),
                  pl.BlockSpec((None, None, d, D_EXPERT), lambda t, te, src, na: (layer, te[t], 0, 0)),
                  pl.BlockSpec((None, None, d, D_EXPERT), lambda t, te, src, na: (layer, te[t], 0, 0)),
                  pl.BlockSpec((None, None, D_EXPERT, d), lambda t, te, src, na: (layer, te[t], 0, 0))],
        out_specs=pl.BlockSpec((tm, d), lambda t, te, src, na: (t, 0)),
        scratch_shapes=[pltpu.VMEM((EXP_BUFFERS, EXP_CHUNKS, MOE_CHUNK, d), BF16),
                        pltpu.SemaphoreType.DMA((EXP_BUFFERS,))],
    )
    return pl.pallas_call(
        _expert_kernel,
        grid_spec=grid_spec,
        out_shape=jax.ShapeDtypeStruct((n_tiles * tm, d), BF16),
        compiler_params=_cparams(("arbitrary",)),
        name="moe_experts",
    )(tile_expert, src_chunk, n_active, xs_chunks, lw["w_e_gate"], lw["w_e_up"], lw["w_e_down"])


def _combine_kernel(src_ref, ys_ref, pos_ref, w_ref, h_ref, x1_ref, gf_ref, wsgu_ref, wsd_ref,
                    o_ref, ybuf, sem, *, cc):
    b = pl.program_id(0)
    last = pl.num_programs(0) - 1
    ahead = COMBINE_BUFFERS - 1
    slot = b % COMBINE_BUFFERS
    nxt_slot = (b + ahead) % COMBINE_BUFFERS
    nxt = jnp.minimum(b + ahead, last)
    nch = MOE_ROWS // MOE_CHUNK

    def copies(blk, buf_slot, lo=0, hi=nch):
        return [pltpu.make_async_copy(ys_ref.at[src_ref[blk * nch + c]], ybuf.at[buf_slot, c], sem.at[buf_slot])
                for c in range(lo, hi)]

    @pl.when(b == 0)
    def _():
        for a in range(ahead):
            _start_all(copies(jnp.minimum(a, last), a))

    acc = _dot(_swiglu(h_ref[...], wsgu_ref[...]).astype(BF16), wsd_ref[...])

    for cp in copies(b, slot):
        cp.wait()
    pos = pos_ref[...].astype(jnp.int16)
    w = w_ref[...].astype(BF16)
    nb = pos.shape[0]
    per = cc // MOE_CHUNK
    for c in range(MOE_ROWS // cc):
        liota = (c * cc + lax.broadcasted_iota(jnp.int32, (nb, cc), 1)).astype(jnp.int16)
        pw = jnp.zeros((nb, cc), BF16)
        for k in range(TOP_K):
            pw = jnp.where(liota == pos[:, k:k + 1], w[:, k:k + 1], pw)
        y = ybuf[slot, c * per:(c + 1) * per].reshape(cc, ybuf.shape[-1])
        acc = acc + _dot(pw, y)
        _start_all(copies(nxt, nxt_slot, c * per, (c + 1) * per))
    o_ref[...] = x1_ref[...] + gf_ref[...] * acc

    @pl.when(b == last)
    def _():
        for a in range(1, COMBINE_BUFFERS):
            for cp in copies(nxt, (b + a) % COMBINE_BUFFERS):
                cp.wait()


def _combine(ys_chunks, src_chunk, posk, wk, h2, x1, gate_f, lw, blocks_per_seq):
    nblk, nb, d = h2.shape
    nch = MOE_ROWS // MOE_CHUNK
    blk = lambda w: pl.BlockSpec((None, nb, w), lambda i, src: (i, 0, 0))
    const = lambda *s: pl.BlockSpec(s, lambda i, src: (0,) * len(s))
    grid_spec = pltpu.PrefetchScalarGridSpec(
        num_scalar_prefetch=1,
        grid=(nblk,),
        in_specs=[pl.BlockSpec(memory_space=pl.ANY), blk(TOP_K), blk(TOP_K), blk(d), blk(d),
                  pl.BlockSpec((None, 1, d), lambda i, src: (i // blocks_per_seq, 0, 0)),
                  const(d, 2 * D_EXPERT), const(D_EXPERT, d)],
        out_specs=blk(d),
        scratch_shapes=[pltpu.VMEM((COMBINE_BUFFERS, nch, MOE_CHUNK, d), BF16),
                        pltpu.SemaphoreType.DMA((COMBINE_BUFFERS,))],
    )
    return pl.pallas_call(
        functools.partial(_combine_kernel, cc=512),
        grid_spec=grid_spec,
        out_shape=jax.ShapeDtypeStruct((nblk, nb, d), F32),
        compiler_params=_cparams(("arbitrary",)),
        name="moe_combine",
    )(src_chunk, ys_chunks, posk, wk, h2, x1, gate_f, lw["w_sh_gu"], lw["w_sh_down"])


def _excl_cumsum(x, axis):
    n = x.shape[axis]
    earlier = jnp.arange(n)[:, None] > jnp.arange(n)[None, :]
    xm = jnp.moveaxis(x, axis, -1)
    out = jnp.sum(jnp.where(earlier, xm[..., None, :], 0), axis=-1)
    return jnp.moveaxis(out, -1, axis)


def _pick(onehot, table):
    return jnp.sum(jnp.where(onehot, table, 0), axis=-1)


def _moe_plan(nch, off, n_tokens):
    nblk = nch.shape[0]
    i32 = jnp.int32
    cpb = MOE_ROWS // MOE_CHUNK
    experts = jnp.arange(N_EXPERTS, dtype=i32)
    tot = jnp.sum(nch, axis=1)
    ech = jnp.sum(nch, axis=0)
    tiles_e = (ech + EXP_CHUNKS - 1) // EXP_CHUNKS
    tile_start = _excl_cumsum(tiles_e, 0)
    tile_end = tile_start + tiles_e
    n_active = jnp.sum(tiles_e)
    cumb = _excl_cumsum(nch, 0)

    n_tiles = _max_expert_tiles(n_tokens, nblk)
    tid = jnp.arange(n_tiles, dtype=i32)
    te = jnp.sum((tid[:, None] >= tile_end[None, :]).astype(i32), axis=1)
    te_last = jnp.sum((n_active - 1 >= tile_end).astype(i32))
    te = jnp.where(tid < n_active, te, te_last)
    is_e = te[:, None] == experts[None, :]
    ts_t = _pick(is_e, tile_start[None, :])
    ech_t = _pick(is_e, ech[None, :])
    ends_t = _pick(is_e[:, None, :], (cumb + nch)[None, :, :])
    off_t = _pick(is_e[:, None, :], off[None, :, :])
    cumb_t = _pick(is_e[:, None, :], cumb[None, :, :])
    c = (tid - ts_t)[:, None] * EXP_CHUNKS + jnp.arange(EXP_CHUNKS, dtype=i32)[None, :]
    valid = jnp.logical_and(c < ech_t[:, None], (tid < n_active)[:, None])
    sb = jnp.minimum(jnp.sum((c[:, :, None] >= ends_t[:, None, :]).astype(i32), axis=2), nblk - 1)
    is_b = sb[:, :, None] == jnp.arange(nblk, dtype=i32)[None, None, :]
    src_e = jnp.where(valid, sb * cpb + _pick(is_b, off_t[:, None, :]) + c - _pick(is_b, cumb_t[:, None, :]), 0)

    j = jnp.arange(cpb, dtype=i32)
    ce = jnp.minimum(jnp.sum((j[None, :, None] >= (off + nch)[:, None, :]).astype(i32), axis=2), N_EXPERTS - 1)
    is_ce = ce[:, :, None] == experts[None, None, :]
    g = (_pick(is_ce, tile_start[None, None, :]) * EXP_CHUNKS + _pick(is_ce, cumb[:, None, :])
         + j[None, :] - _pick(is_ce, off[:, None, :]))
    src_c = jnp.where(j[None, :] < tot[:, None], g, 0)
    return te, src_e.reshape(-1), n_active.reshape(1), src_c.reshape(-1)


def _max_expert_tiles(t, nblk):
    max_chunks = t * TOP_K // MOE_CHUNK + nblk * N_EXPERTS
    return max_chunks // EXP_CHUNKS + N_EXPERTS


def _moe_sparse(h2, idx_t, w_t, x1, gate_f, lw):
    b, t, d = x1.shape
    nblk = b * t // MOE_BLOCK
    h2b = h2.reshape(nblk, MOE_BLOCK, d)
    xs, posk_t, nch, off = _dispatch(h2b, idx_t)
    te, src_e, n_active, src_c = _moe_plan(nch[:, :, 0], off[:, :, 0], b * t)
    ys = _experts(xs.reshape(nblk * MOE_ROWS // MOE_CHUNK, MOE_CHUNK, d), te, src_e, n_active, lw)
    wk = w_t.reshape(TOP_K, nblk, MOE_BLOCK).transpose(1, 2, 0)
    out = _combine(ys.reshape(-1, MOE_CHUNK, d), src_c, posk_t.transpose(0, 2, 1), wk,
                   h2b, x1.reshape(nblk, MOE_BLOCK, d), gate_f, lw, t // MOE_BLOCK)
    return out.reshape(b, t, d)


def _pad_heads(w, width):
    pad = [(0, 0)] * (w.ndim - 1) + [(0, HP - width)]
    wp = jnp.pad(w, pad)
    return wp.reshape(w.shape[:-2] + (w.shape[-2] * HP,))


def _rot_cols(w):
    half = QK_ROPE // 2
    return jnp.concatenate([-w[..., half:], w[..., :half]], axis=-1)


def _prep_layer(l, p):
    d = D_MODEL
    w_in = p["w_in"][l]
    off_kv = Q_LORA
    off_kr = off_kv + KV_LORA
    off_cb = off_kr + QK_ROPE
    off_g = off_cb + 3 * CONV_DIM + 2 * GM_DIM
    wkr = w_in[:, off_kr:off_cb]
    lead = jnp.zeros((d, QK_NOPE), F32)
    trail = jnp.zeros((d, HP - QK_DIM), F32)
    w2 = jnp.concatenate([
        w_in[:, :off_kr],
        lead, wkr, trail,
        lead, _rot_cols(wkr), trail,
        w_in[:, off_cb:],
    ], axis=1).astype(BF16)
    assert w2.shape[1] == S_END and off_g + N_BRANCH * d == w_in.shape[1]

    wuq = p["w_uq"][l]
    wuq_rot = jnp.concatenate([jnp.zeros_like(wuq[..., :QK_NOPE]), _rot_cols(wuq[..., QK_NOPE:])], axis=-1)
    wuq2 = jnp.concatenate([_pad_heads(wuq, QK_DIM), _pad_heads(wuq_rot, QK_DIM)], axis=-1).astype(BF16)

    def pad_gain(gq):
        return jnp.pad(gq, (0, HP - QK_DIM)).reshape(1, HP)

    w_br_a = jnp.pad(p["w_br_a"][l].reshape(N_HEADS, V_DIM, d), ((0, 0), (0, HP - V_DIM), (0, 0)))
    w_sp = p["w_spatial"][l]
    b_sp = jnp.repeat(p["b_spatial"][l].T, GM_GROUP_DIM, axis=1)
    return dict(
        g_norm_mix=p["g_norm_mix"][l].reshape(1, d), g_norm_ffn=p["g_norm_ffn"][l].reshape(1, d),
        w2=w2, g_q_lat=p["g_q_lat"][l].reshape(1, Q_LORA), wuq2=wuq2,
        g_qk_q=pad_gain(p["g_qk_q"][l]), g_qk_k=pad_gain(p["g_qk_k"][l]),
        g_kv_lat=p["g_kv_lat"][l].reshape(1, KV_LORA),
        wuk=_pad_heads(p["w_uk"][l], QK_NOPE).astype(BF16),
        wkt=p["w_uk"][l].transpose(2, 1, 0).reshape(QK_NOPE * N_HEADS, KV_LORA).astype(BF16),
        wukt_pad=jnp.pad(p["w_uk"][l].transpose(1, 2, 0), ((0, 0), (0, HP - QK_NOPE), (0, 0))).astype(BF16),
        wuv=_pad_heads(p["w_uv"][l], V_DIM).astype(BF16),
        w_br_a=w_br_a.reshape(N_HEADS * HP, d).astype(BF16),
        w_conv=p["w_conv"][l], w_br_b=p["w_br_b"][l].astype(BF16),
        g_v_ln=p["g_v_ln"][l].reshape(1, GM_DIM),
        w_sp=w_sp.reshape(GM_GROUPS * CHUNK, CHUNK), b_sp=b_sp,
        sp_coef=jnp.repeat(w_sp[:, 0, 0], GM_GROUP_DIM).reshape(1, GM_DIM),
        w_br_c=p["w_br_c"][l].astype(BF16), w_out=p["w_out"][l].astype(BF16),
        w_router_t=p["w_router"][l].T.astype(BF16),
        b_router_t=p["b_router"][l].reshape(N_EXPERTS, 1),
        w_sh_gu=jnp.concatenate([p["w_sh_gate"][l], p["w_sh_up"][l]], axis=-1).astype(BF16),
        w_sh_down=p["w_sh_down"][l].astype(BF16),
    )


def _rope_tables(pos):
    inv_freq = ROPE_THETA ** (-jnp.arange(0, QK_ROPE, 2, dtype=F32) / QK_ROPE)
    ang = pos.astype(F32)[:, None] * inv_freq[None, :]
    c, s = jnp.cos(ang), jnp.sin(ang)
    n = pos.shape[0]
    cos = jnp.concatenate([jnp.ones((n, QK_NOPE), F32), c, c, jnp.ones((n, HP - QK_DIM), F32)], axis=1)
    sin = jnp.concatenate([jnp.zeros((n, QK_NOPE), F32), s, s, jnp.zeros((n, HP - QK_DIM), F32)], axis=1)
    return cos, sin


def _split_mod(m):
    return [m[:, i * D_MODEL:(i + 1) * D_MODEL] for i in range(6)]


def _prompt_layer(x, mod, lw, cos, sin):
    b, t, d = x.shape
    shift_m, scale_m, gate_m, shift_f, scale_f, gate_f = [a.reshape(b, 1, d) for a in _split_mod(mod)]
    q, k, v, ckv, krp, tail, cv, bg0, part = _inproj_prompt(x, scale_m, shift_m, lw, cos, sin)
    oa = _flash_attention(q, k, v)
    x1, h2 = _merge(oa, bg0, part, x, gate_m, scale_f, shift_f, lw, TM_MERGE)
    _, idx_t, w_t = _router(h2.reshape(b * t, d), lw, TM_ROUTE)
    y = _moe_sparse(h2, idx_t, w_t, x1, gate_f, lw)
    state = (ckv, krp[:, :, QK_NOPE:QK_DIM], tail[:, 8 - (CONV_W - 1):], cv)
    return y, state


def _sample_layer(layer, x, mod, lw, cos, sin, hist, pool_c, pool_krt, page_table):
    n, d = x.shape
    shift_m, scale_m, gate_m, shift_f, scale_f, gate_f = _split_mod(mod)
    qg, ql, ckv, krp, z, cv, bg0, part = _inproj_sample(x, scale_m, shift_m, lw, cos, sin, hist[:, 0], hist[:, 1])
    head_pad = ((0, 0), (0, HEAD_ROWS - N_HEADS), (0, 0))
    qlat = jnp.pad(ql.transpose(1, 0, 2), head_pad).astype(BF16)
    qr = jnp.pad(qg[:, :, QK_NOPE:QK_DIM].transpose(1, 0, 2), head_pad).astype(BF16)
    kr = krp[:, QK_NOPE:QK_DIM]
    krnew_t = jnp.pad(kr[:, :, None], ((0, 0), (0, 0), (0, LANES - 1)))
    olat = _paged_attention(layer, page_table, lw["wkt"], qlat, qr, ckv.reshape(n, 1, KV_LORA), krnew_t,
                            pool_c, pool_krt)
    oa = _uv_project(olat.reshape(n, N_HEADS * KV_LORA), lw["wuv"])
    g3 = lambda a: a.reshape(1, n, -1)
    x1, h2 = _merge(g3(oa), g3(bg0), g3(part), g3(x), g3(gate_m), g3(scale_f), g3(shift_f), lw, n)
    gates, _, _ = _router(h2.reshape(n, d), lw, n)
    y = _moe(h2, g3(gates), x1, g3(gate_f), lw, n).reshape(n, d)
    state = (ckv.reshape(n, 1, KV_LORA), kr.reshape(n, 1, QK_ROPE),
             jnp.stack([hist[:, 1], z], axis=1), cv.reshape(n, 1, GM_DIM))
    return y, state


def kernel(x_prompt, x_sample, cache_kv_latent, cache_k_rope, state_conv, page_table, c_prompt, c_sample,
           w_ada, b_ada, g_norm_mix, g_norm_ffn, w_in, g_q_lat, w_uq, g_kv_lat, w_uk, w_uv, g_qk_q, g_qk_k,
           w_br_a, w_conv, w_br_b, g_v_ln, w_spatial, b_spatial, w_br_c, w_out, w_router, b_router,
           w_e_gate, w_e_up, w_e_down, w_sh_gate, w_sh_up, w_sh_down):
    params = dict(w_ada=w_ada, b_ada=b_ada, g_norm_mix=g_norm_mix, g_norm_ffn=g_norm_ffn, w_in=w_in,
                  g_q_lat=g_q_lat, w_uq=w_uq, g_kv_lat=g_kv_lat, w_uk=w_uk, w_uv=w_uv, g_qk_q=g_qk_q,
                  g_qk_k=g_qk_k, w_br_a=w_br_a, w_conv=w_conv, w_br_b=w_br_b, g_v_ln=g_v_ln,
                  w_spatial=w_spatial, b_spatial=b_spatial, w_br_c=w_br_c, w_out=w_out, w_router=w_router,
                  b_router=b_router, w_e_gate=w_e_gate, w_e_up=w_e_up, w_e_down=w_e_down,
                  w_sh_gate=w_sh_gate, w_sh_up=w_sh_up, w_sh_down=w_sh_down)
    depth = w_in.shape[0]
    nb, t, d = x_prompt.shape
    ns = x_sample.shape[0]
    assert x_sample.shape[1] == 1 and t % TQ == 0 and t % MOE_BLOCK == 0
    past_len = page_table.shape[1] * PAGE_SIZE

    cos_p, sin_p = _rope_tables(jnp.arange(t))
    cos_s, sin_s = _rope_tables(past_len + jnp.arange(1))
    c_all = jnp.concatenate([c_prompt, c_sample], axis=0)
    cache_krt = jnp.swapaxes(cache_k_rope, 2, 3)
    b_ada3 = b_ada.reshape(depth, 1, -1)

    hp, hs = x_prompt, x_sample.reshape(ns, d)
    st_p, st_s = [], []
    for l in range(depth):
        lw = _prep_layer(l, params)
        lw.update(layer=l, w_e_gate=w_e_gate, w_e_up=w_e_up, w_e_down=w_e_down)
        mod = _adaln(l, c_all, w_ada, b_ada3)
        hp, sp = _prompt_layer(hp, mod[:nb], lw, cos_p, sin_p)
        hs, ss = _sample_layer(l, hs, mod[nb:], lw, cos_s, sin_s, state_conv[l], cache_kv_latent, cache_krt,
                               page_table)
        st_p.append(sp)
        st_s.append(ss)

    stack = lambda sts, i: jnp.stack([s[i] for s in sts])
    return (hp, hs.reshape(ns, 1, d),
            stack(st_p, 0), stack(st_p, 1), stack(st_p, 2), stack(st_p, 3),
            stack(st_s, 0), stack(st_s, 1), stack(st_s, 2), stack(st_s, 3))
```

```python
import functools

import jax
import jax.numpy as jnp
from jax import lax
from jax.experimental import pallas as pl
from jax.experimental.pallas import tpu as pltpu

F32 = jnp.float32
BF16 = jnp.bfloat16

D_MODEL = 1024
N_HEADS = 8
QK_NOPE = 64
QK_ROPE = 32
QK_DIM = QK_NOPE + QK_ROPE
V_DIM = 64
Q_LORA = 384
KV_LORA = 256
ROPE_THETA = 10000.0
CONV_W = 3
CONV_DIM = 512
CHUNK = 128
GM_GROUPS = 8
GM_DIM = 512
GM_GROUP_DIM = GM_DIM // GM_GROUPS
N_EXPERTS = 64
TOP_K = 8
N_EXPERT_GROUPS = 8
GROUP_SIZE = N_EXPERTS // N_EXPERT_GROUPS
TOPK_GROUPS = 4
D_EXPERT = 256
ROUTED_SCALE = 2.5
N_BRANCH = 3
EPS = 1e-6
PAGE_SIZE = 128

LANES = 128
HP = LANES
VMEM_LIMIT = 56 * 1024 * 1024

S_Q = 0
S_KV = S_Q + Q_LORA
S_KRM = S_KV + KV_LORA
S_KRR = S_KRM + HP
S_CB = S_KRR + HP
S_CC = S_CB + CONV_DIM
S_CH = S_CC + CONV_DIM
S_U = S_CH + CONV_DIM
S_V = S_U + GM_DIM
S_G = S_V + GM_DIM
S_END = S_G + N_BRANCH * D_MODEL

TM_IN = 512
TQ = 512
TK = 512
TM_MERGE = 512
TM_ROUTE = 512
TM_MOE = 1024
EXPERTS_PER_STEP = 2
MOE_BLOCK = 256
MOE_CHUNK = 16
EXP_CHUNKS = 32
EXP_BUFFERS = 3
COMBINE_BUFFERS = 3
MOE_ROWS = -(-(MOE_BLOCK * TOP_K + N_EXPERTS * (MOE_CHUNK - 1)) // 512) * 512
PAGES_PER_STEP = 32
PAGE_CHUNKS = 2
PAGE_BUFFERS = 3
HEAD_ROWS = 16


def _cparams(sem):
    return pltpu.CompilerParams(dimension_semantics=sem, vmem_limit_bytes=VMEM_LIMIT)


def _const_spec(shape):
    nd = len(shape)
    return pl.BlockSpec(shape, lambda *_: (0,) * nd, pipeline_mode=pl.Buffered(1))


def _dot(a, b):
    return jnp.dot(a, b, preferred_element_type=F32)


def _rms_rows(x):
    return x * lax.rsqrt(jnp.mean(x * x, axis=-1, keepdims=True) + EPS)


def _silu(x):
    return x * jax.nn.sigmoid(x)


def _adaln_kernel(c_ref, w_ref, b_ref, o_ref):
    s = _silu(c_ref[...]).astype(BF16)
    o_ref[...] = _dot(s, w_ref[...].astype(BF16)) + b_ref[...]


def _adaln(layer, c_all, w_ada, b_ada):
    n, d = c_all.shape
    e = w_ada.shape[2]
    tn = 1536
    return pl.pallas_call(
        _adaln_kernel,
        grid=(e // tn,),
        in_specs=[pl.BlockSpec((n, d), lambda j: (0, 0)),
                  pl.BlockSpec((None, d, tn), lambda j: (layer, 0, j)),
                  pl.BlockSpec((None, 1, tn), lambda j: (layer, 0, j))],
        out_specs=pl.BlockSpec((n, tn), lambda j: (0, j)),
        out_shape=jax.ShapeDtypeStruct((n, e), F32),
        compiler_params=_cparams(("arbitrary",)),
        name="adaln",
    )(c_all, w_ada, b_ada)


def _inproj_common(x_ref, sc_ref, sh_ref, gmix_ref, w2_ref):
    h = (_rms_rows(x_ref[...]) * gmix_ref[...] * (1.0 + sc_ref[...]) + sh_ref[...]).astype(BF16)

    def seg(a, b):
        return _dot(h, w2_ref[:, a:b])

    return seg


def _heads_q(seg, gq_ref, wuq_ref, gqq_ref, cos, sin):
    cqn = (_rms_rows(seg(S_Q, S_KV)) * gq_ref[...]).astype(BF16)
    q2 = _dot(cqn, wuq_ref[...])
    out = []
    for hh in range(N_HEADS):
        qm = q2[:, hh * HP:(hh + 1) * HP]
        qr = q2[:, (N_HEADS + hh) * HP:(N_HEADS + hh + 1) * HP]
        qh = qm * cos + qr * sin
        inv = lax.rsqrt(jnp.sum(qh * qh, axis=-1, keepdims=True) * (1.0 / QK_DIM) + EPS)
        out.append(qh * inv * gqq_ref[...] * (QK_DIM ** -0.5))
    return out


def _latent_kv(seg, gkv_ref, cos, sin):
    ckv = _rms_rows(seg(S_KV, S_KRM)) * gkv_ref[...]
    krp = seg(S_KRM, S_KRR) * cos + seg(S_KRR, S_CB) * sin
    return ckv, krp


def _gates_and_partial(seg, brb_in, brc_in, wbrb_ref, wbrc_ref, bg0_ref, part_ref):
    brb = _dot(brb_in.astype(BF16), wbrb_ref[...])
    brc = _dot(brc_in.astype(BF16), wbrc_ref[...])
    bg = jax.nn.sigmoid(seg(S_G, S_END))
    bg0_ref[...] = bg[:, :D_MODEL].astype(BF16)
    part_ref[...] = bg[:, D_MODEL:2 * D_MODEL] * brb + bg[:, 2 * D_MODEL:] * brc


def _layernorm_rows(v, g):
    xc = v - jnp.mean(v, axis=-1, keepdims=True)
    return xc * lax.rsqrt(jnp.mean(xc * xc, axis=-1, keepdims=True) + EPS) * g


def _inproj_prompt_kernel(x_ref, sc_ref, sh_ref, gmix_ref, w2_ref, gq_ref, wuq_ref, gqq_ref,
                          gkv_ref, wuk_ref, gqk_ref, wuv_ref, cos_ref, sin_ref, wconv_ref,
                          wbrb_ref, gvln_ref, wsp_ref, bsp_ref, wbrc_ref,
                          q_ref, k_ref, v_ref, ckv_ref, kr_ref, tail_ref, cv_ref, bg0_ref, part_ref,
                          zbuf_ref, *, tm):
    t = pl.program_id(1)
    seg = _inproj_common(x_ref, sc_ref, sh_ref, gmix_ref, w2_ref)
    cos = cos_ref[...]
    sin = sin_ref[...]

    qs = _heads_q(seg, gq_ref, wuq_ref, gqq_ref, cos, sin)
    for hh in range(N_HEADS):
        q_ref[hh] = qs[hh].astype(BF16)
    ckv, krp = _latent_kv(seg, gkv_ref, cos, sin)
    ckv_ref[...] = ckv
    kr_ref[...] = krp
    ckv_b = ckv.astype(BF16)
    k2 = _dot(ckv_b, wuk_ref[...])
    v2 = _dot(ckv_b, wuv_ref[...])
    ones_lane = jnp.where(lax.broadcasted_iota(jnp.int32, (1, HP), 1) == V_DIM, 1.0, 0.0)
    for hh in range(N_HEADS):
        kh = k2[:, hh * HP:(hh + 1) * HP] + krp
        inv = lax.rsqrt(jnp.sum(kh * kh, axis=-1, keepdims=True) * (1.0 / QK_DIM) + EPS)
        k_ref[hh] = (kh * inv * gqk_ref[...]).astype(BF16)
        v_ref[hh] = (v2[:, hh * HP:(hh + 1) * HP] + ones_lane).astype(BF16)

    gate_b = seg(S_CB, S_CC)
    z = seg(S_CC, S_CH) * seg(S_CH, S_U)

    @pl.when(t == 0)
    def _():
        zbuf_ref[0:8, :] = jnp.zeros((8, CONV_DIM), F32)

    zbuf_ref[8:8 + tm, :] = z
    z1 = zbuf_ref[7:7 + tm, :]
    z2 = zbuf_ref[6:6 + tm, :]
    wc = wconv_ref[...]
    y = wc[0:1, :] * z2 + wc[1:2, :] * z1 + wc[2:3, :] * z
    zbuf_ref[0:8, :] = z[tm - 8:tm, :]
    tail_ref[...] = z[tm - 8:tm, :]

    u = seg(S_U, S_V)
    vn = _layernorm_rows(seg(S_V, S_G), gvln_ref[...])
    cv_ref[...] = vn[tm - CHUNK:tm, :]
    vnb = vn.astype(BF16)
    rows = lax.broadcasted_iota(jnp.int32, (GM_GROUPS * CHUNK, CHUNK), 0) % CHUNK
    cols = lax.broadcasted_iota(jnp.int32, (GM_GROUPS * CHUNK, CHUNK), 1)
    wsp = jnp.where(cols <= rows, wsp_ref[...], 0.0).astype(BF16)
    lane_grp = lax.broadcasted_iota(jnp.int32, (CHUNK, GM_DIM), 1) // GM_GROUP_DIM
    sgs = []
    for c in range(tm // CHUNK):
        r = _dot(wsp, vnb[c * CHUNK:(c + 1) * CHUNK, :])
        s = bsp_ref[...]
        for g in range(GM_GROUPS):
            s = s + jnp.where(lane_grp == g, r[g * CHUNK:(g + 1) * CHUNK, :], 0.0)
        sgs.append(u[c * CHUNK:(c + 1) * CHUNK, :] * s)
    sg = jnp.concatenate(sgs, axis=0)

    _gates_and_partial(seg, gate_b * y, sg, wbrb_ref, wbrc_ref, bg0_ref, part_ref)


def _inproj_sample_kernel(x_ref, sc_ref, sh_ref, gmix_ref, w2_ref, gq_ref, wuq_ref, gqq_ref,
                          gkv_ref, gqk_ref, wukt_ref, cos_ref, sin_ref, wconv_ref, h0_ref, h1_ref,
                          wbrb_ref, gvln_ref, coef_ref, bias_ref, wbrc_ref,
                          qg_ref, ql_ref, ckv_ref, kr_ref, z_ref, cv_ref, bg0_ref, part_ref):
    seg = _inproj_common(x_ref, sc_ref, sh_ref, gmix_ref, w2_ref)
    cos = cos_ref[...]
    sin = sin_ref[...]
    qs = _heads_q(seg, gq_ref, wuq_ref, gqq_ref, cos, sin)
    for hh in range(N_HEADS):
        qg = qs[hh] * gqk_ref[...]
        qg_ref[hh] = qg
        ql_ref[hh] = _dot(qg.astype(BF16), wukt_ref[hh])
    ckv, krp = _latent_kv(seg, gkv_ref, cos, sin)
    ckv_ref[...] = ckv
    kr_ref[...] = krp

    gate_b = seg(S_CB, S_CC)
    z = seg(S_CC, S_CH) * seg(S_CH, S_U)
    wc = wconv_ref[...]
    y = wc[0:1, :] * h0_ref[...] + wc[1:2, :] * h1_ref[...] + wc[2:3, :] * z
    z_ref[...] = z

    u = seg(S_U, S_V)
    vn = _layernorm_rows(seg(S_V, S_G), gvln_ref[...])
    cv_ref[...] = vn
    sg = u * (vn * coef_ref[...] + bias_ref[...])

    _gates_and_partial(seg, gate_b * y, sg, wbrb_ref, wbrc_ref, bg0_ref, part_ref)


def _inproj_prompt(x, scale_m, shift_m, lw, cos, sin):
    b, t, d = x.shape
    tm = TM_IN
    nt = t // tm
    tok = lambda w: pl.BlockSpec((None, tm, w), lambda i, j: (i, j, 0))
    mod = pl.BlockSpec((None, 1, d), lambda i, j: (i, 0, 0))
    head = pl.BlockSpec((None, N_HEADS, tm, HP), lambda i, j: (i, 0, j, 0))
    in_specs = [
        tok(d), mod, mod, _const_spec((1, d)), _const_spec((d, S_END)),
        _const_spec((1, Q_LORA)), _const_spec((Q_LORA, 2 * N_HEADS * HP)), _const_spec((1, HP)),
        _const_spec((1, KV_LORA)), _const_spec((KV_LORA, N_HEADS * HP)), _const_spec((1, HP)),
        _const_spec((KV_LORA, N_HEADS * HP)),
        pl.BlockSpec((tm, HP), lambda i, j: (j, 0)), pl.BlockSpec((tm, HP), lambda i, j: (j, 0)),
        _const_spec((CONV_W, CONV_DIM)), _const_spec((CONV_DIM, d)), _const_spec((1, GM_DIM)),
        _const_spec((GM_GROUPS * CHUNK, CHUNK)), _const_spec((CHUNK, GM_DIM)), _const_spec((GM_DIM, d)),
    ]
    out_specs = [
        head, head, head, tok(KV_LORA), tok(HP),
        pl.BlockSpec((None, 8, CONV_DIM), lambda i, j: (i, 0, 0)),
        pl.BlockSpec((None, CHUNK, GM_DIM), lambda i, j: (i, 0, 0)),
        tok(d), tok(d),
    ]
    out_shape = [
        jax.ShapeDtypeStruct((b, N_HEADS, t, HP), BF16),
        jax.ShapeDtypeStruct((b, N_HEADS, t, HP), BF16),
        jax.ShapeDtypeStruct((b, N_HEADS, t, HP), BF16),
        jax.ShapeDtypeStruct((b, t, KV_LORA), F32),
        jax.ShapeDtypeStruct((b, t, HP), F32),
        jax.ShapeDtypeStruct((b, 8, CONV_DIM), F32),
        jax.ShapeDtypeStruct((b, CHUNK, GM_DIM), F32),
        jax.ShapeDtypeStruct((b, t, d), BF16),
        jax.ShapeDtypeStruct((b, t, d), F32),
    ]
    return pl.pallas_call(
        functools.partial(_inproj_prompt_kernel, tm=tm),
        grid=(b, nt),
        in_specs=in_specs, out_specs=out_specs, out_shape=out_shape,
        scratch_shapes=[pltpu.VMEM((tm + 8, CONV_DIM), F32)],
        compiler_params=_cparams(("arbitrary", "arbitrary")),
        name="inproj_prompt",
    )(x, scale_m, shift_m, lw["g_norm_mix"], lw["w2"], lw["g_q_lat"], lw["wuq2"], lw["g_qk_q"],
      lw["g_kv_lat"], lw["wuk"], lw["g_qk_k"], lw["wuv"], cos, sin, lw["w_conv"],
      lw["w_br_b"], lw["g_v_ln"], lw["w_sp"], lw["b_sp"], lw["w_br_c"])


def _inproj_sample(x, scale_m, shift_m, lw, cos, sin, hist0, hist1):
    n, d = x.shape
    full = lambda *s: pl.BlockSpec(s, lambda i: (0,) * len(s))
    in_specs = [
        full(n, d), full(n, d), full(n, d), full(1, d), full(d, S_END),
        full(1, Q_LORA), full(Q_LORA, 2 * N_HEADS * HP), full(1, HP),
        full(1, KV_LORA), full(1, HP), full(N_HEADS, HP, KV_LORA), full(1, HP), full(1, HP),
        full(CONV_W, CONV_DIM), full(n, CONV_DIM), full(n, CONV_DIM),
        full(CONV_DIM, d), full(1, GM_DIM), full(1, GM_DIM), full(1, GM_DIM), full(GM_DIM, d),
    ]
    out_specs = [full(N_HEADS, n, HP), full(N_HEADS, n, KV_LORA), full(n, KV_LORA), full(n, HP),
                 full(n, CONV_DIM), full(n, GM_DIM), full(n, d), full(n, d)]
    out_shape = [
        jax.ShapeDtypeStruct((N_HEADS, n, HP), F32),
        jax.ShapeDtypeStruct((N_HEADS, n, KV_LORA), F32),
        jax.ShapeDtypeStruct((n, KV_LORA), F32),
        jax.ShapeDtypeStruct((n, HP), F32),
        jax.ShapeDtypeStruct((n, CONV_DIM), F32),
        jax.ShapeDtypeStruct((n, GM_DIM), F32),
        jax.ShapeDtypeStruct((n, d), BF16),
        jax.ShapeDtypeStruct((n, d), F32),
    ]
    return pl.pallas_call(
        _inproj_sample_kernel,
        grid=(1,),
        in_specs=in_specs, out_specs=out_specs, out_shape=out_shape,
        compiler_params=_cparams(("arbitrary",)),
        name="inproj_sample",
    )(x, scale_m, shift_m, lw["g_norm_mix"], lw["w2"], lw["g_q_lat"], lw["wuq2"], lw["g_qk_q"],
      lw["g_kv_lat"], lw["g_qk_k"], lw["wukt_pad"], cos, sin, lw["w_conv"], hist0, hist1,
      lw["w_br_b"], lw["g_v_ln"], lw["sp_coef"], lw["b_sp"][0:1], lw["w_br_c"])


def _flash_kernel(q_ref, k_ref, v_ref, o_ref, *, tq, tk):
    assert tq == tk
    qi = pl.program_id(2)
    q = q_ref[...]

    def step(j, carry, masked):
        m, acc = carry
        start = pl.multiple_of(j * tk, tk)
        k = k_ref[pl.ds(start, tk), :]
        v = v_ref[pl.ds(start, tk), :]
        s = lax.dot_general(q, k, (((1,), (1,)), ((), ())), preferred_element_type=F32)
        if masked:
            row = lax.broadcasted_iota(jnp.int32, (tq, tk), 0)
            col = lax.broadcasted_iota(jnp.int32, (tq, tk), 1)
            s = jnp.where(col <= row, s, -jnp.inf)
        m_new = jnp.maximum(m, jnp.max(s, axis=-1, keepdims=True))
        alpha = jnp.exp(m - m_new)
        p = jnp.exp(s - m_new)
        acc = alpha * acc + _dot(p.astype(BF16), v)
        return m_new, acc

    m0 = jnp.full((tq, 1), -jnp.inf, F32)
    a0 = jnp.zeros((tq, HP), F32)
    carry = lax.fori_loop(0, qi, lambda j, c: step(j, c, False), (m0, a0))
    _, acc = step(qi, carry, True)
    lane = lax.broadcasted_iota(jnp.int32, (tq, HP), 1)
    l = jnp.sum(jnp.where(lane == V_DIM, acc, 0.0), axis=-1, keepdims=True)
    o_ref[...] = (acc / l).astype(BF16)


def _flash_attention(q, k, v):
    b, h, t, _ = q.shape
    tq, tk = TQ, TK
    return pl.pallas_call(
        functools.partial(_flash_kernel, tq=tq, tk=tk),
        grid=(b, h, t // tq),
        in_specs=[pl.BlockSpec((None, None, tq, HP), lambda i, j, n: (i, j, n, 0)),
                  pl.BlockSpec((None, None, t, HP), lambda i, j, n: (i, j, 0, 0)),
                  pl.BlockSpec((None, None, t, HP), lambda i, j, n: (i, j, 0, 0))],
        out_specs=pl.BlockSpec((None, tq, HP), lambda i, j, n: (i, n, j)),
        out_shape=jax.ShapeDtypeStruct((b, t, h * HP), BF16),
        compiler_params=_cparams(("arbitrary", "arbitrary", "arbitrary")),
        name="flash_prompt",
    )(q, k, v)


def _paged_kernel(pt_ref, wkt_ref, ql_ref, qr_ref, cnew_ref, krnew_ref, poolc_ref, poolk_ref, o_ref,
                  cbuf, kbuf, csem, ksem, lhs_ref, m_ref, l_ref, acc_ref, *, layer, n_pages, n_pg, n_chunk):
    i = pl.program_id(0)
    j = pl.program_id(1)
    nj = pl.num_programs(1)
    nk = N_HEADS * QK_NOPE
    step = i * nj + j
    last_step = pl.num_programs(0) * nj - 1
    ahead = PAGE_BUFFERS - 1
    slot = step % PAGE_BUFFERS
    nxt_slot = (step + ahead) % PAGE_BUFFERS

    def step_base(t):
        t = jnp.minimum(t, last_step)
        return (t // nj) * n_pages + (t % nj) * n_pg

    nxt_base = step_base(step + ahead)

    def page_copies(base, buf_slot, p):
        pid = pt_ref[base + p]
        return (pltpu.make_async_copy(poolc_ref.at[layer, pid], cbuf.at[buf_slot, p], csem.at[buf_slot]),
                pltpu.make_async_copy(poolk_ref.at[layer, pid], kbuf.at[buf_slot, p], ksem.at[buf_slot]))

    def start_page(base, buf_slot, p):
        for cp in page_copies(base, buf_slot, p):
            cp.start()

    @pl.when(step == 0)
    def _():
        for a in range(ahead):
            for p in range(n_pg):
                start_page(step_base(a), a, p)

    for p in range(n_pg):
        for cp in page_copies(i * n_pages + j * n_pg, slot, p):
            cp.wait()

    @pl.when(j == 0)
    def _():
        lhs_ref[0:nk, :] = wkt_ref[...]
        lhs_ref[nk:nk + HEAD_ROWS, :] = ql_ref[...]
        m_ref[...] = jnp.full(m_ref.shape, -jnp.inf, F32)
        l_ref[...] = jnp.zeros(l_ref.shape, F32)
        acc_ref[...] = jnp.zeros(acc_ref.shape, F32)

    qr = qr_ref[...]

    def update(cf, krt, valid_keys):
        cb = cf.astype(BF16)
        kx = _dot(lhs_ref[...], cf.T.astype(BF16))
        sq = kx[:nk, :] * kx[:nk, :]
        ssq = jnp.sum(sq.reshape(QK_NOPE, N_HEADS, sq.shape[-1]), axis=0)
        ssq = ssq + jnp.sum(krt * krt, axis=0, keepdims=True)
        inv = lax.rsqrt(ssq * (1.0 / QK_DIM) + EPS)
        st = (kx[nk:, :] + _dot(qr, krt.astype(BF16))) * jnp.concatenate([inv, inv], axis=0)
        if valid_keys is not None:
            kcol = lax.broadcasted_iota(jnp.int32, st.shape, 1)
            st = jnp.where(kcol < valid_keys, st, -jnp.inf)
        m_old = m_ref[...]
        m_new = jnp.maximum(m_old, jnp.max(st, axis=-1, keepdims=True))
        alpha = jnp.exp(m_old - m_new)
        p = jnp.exp(st - m_new)
        l_ref[...] = alpha * l_ref[...] + jnp.sum(p, axis=-1, keepdims=True)
        acc_ref[...] = alpha * acc_ref[...] + _dot(p.astype(BF16), cb)
        m_ref[...] = m_new

    per = n_pg // n_chunk
    for ch in range(n_chunk):
        cf = cbuf[slot, ch * per:(ch + 1) * per].reshape(per * PAGE_SIZE, KV_LORA)
        krt = jnp.concatenate([kbuf[slot, p] for p in range(ch * per, (ch + 1) * per)], axis=1)
        update(cf, krt, None)
        for p in range(ch * per, (ch + 1) * per):
            start_page(nxt_base, nxt_slot, p)

    @pl.when(j == nj - 1)
    def _():
        first = lax.broadcasted_iota(jnp.int32, (LANES, KV_LORA), 0) == 0
        cn = jnp.where(first, jnp.broadcast_to(cnew_ref[...], (LANES, KV_LORA)), 0.0)
        update(cn, krnew_ref[...], 1)
        o_ref[...] = (acc_ref[...] / l_ref[...])[:N_HEADS, :]

    @pl.when(step == last_step)
    def _():
        for a in range(1, PAGE_BUFFERS):
            for p in range(n_pg):
                for cp in page_copies(nxt_base, (step + a) % PAGE_BUFFERS, p):
                    cp.wait()


def _paged_attention(layer, page_table, wkt, qlat, qr, cnew, krnew_t, pool_c, pool_krt):
    s, n_pages = page_table.shape
    n_pg = PAGES_PER_STEP
    nj = n_pages // n_pg
    pt = page_table.reshape(-1)
    nk = N_HEADS * QK_NOPE

    per_seq = lambda *shape: pl.BlockSpec((None,) + shape, lambda i, j, pt_ref: (i,) + (0,) * len(shape))
    const = lambda *shape: pl.BlockSpec(shape, lambda i, j, pt_ref: (0,) * len(shape))
    in_specs = [const(nk, KV_LORA), per_seq(HEAD_ROWS, KV_LORA), per_seq(HEAD_ROWS, QK_ROPE),
                per_seq(1, KV_LORA), per_seq(QK_ROPE, LANES),
                pl.BlockSpec(memory_space=pl.ANY), pl.BlockSpec(memory_space=pl.ANY)]
    grid_spec = pltpu.PrefetchScalarGridSpec(
        num_scalar_prefetch=1,
        grid=(s, nj),
        in_specs=in_specs,
        out_specs=per_seq(N_HEADS, KV_LORA),
        scratch_shapes=[pltpu.VMEM((PAGE_BUFFERS, n_pg, PAGE_SIZE, KV_LORA), F32),
                        pltpu.VMEM((PAGE_BUFFERS, n_pg, QK_ROPE, PAGE_SIZE), F32),
                        pltpu.SemaphoreType.DMA((PAGE_BUFFERS,)), pltpu.SemaphoreType.DMA((PAGE_BUFFERS,)),
                        pltpu.VMEM((nk + HEAD_ROWS, KV_LORA), BF16), pltpu.VMEM((HEAD_ROWS, 1), F32),
                        pltpu.VMEM((HEAD_ROWS, 1), F32), pltpu.VMEM((HEAD_ROWS, KV_LORA), F32)],
    )
    return pl.pallas_call(
        functools.partial(_paged_kernel, layer=layer, n_pages=n_pages, n_pg=n_pg, n_chunk=PAGE_CHUNKS),
        grid_spec=grid_spec,
        out_shape=jax.ShapeDtypeStruct((s, N_HEADS, KV_LORA), F32),
        compiler_params=_cparams(("arbitrary", "arbitrary")),
        name="paged_sample",
    )(pt, wkt, qlat, qr, cnew, krnew_t, pool_c, pool_krt)


def _uv_kernel(ol_ref, wuv_ref, o_ref):
    for hh in range(N_HEADS):
        ol = ol_ref[:, hh * KV_LORA:(hh + 1) * KV_LORA].astype(BF16)
        o_ref[:, hh * HP:(hh + 1) * HP] = _dot(ol, wuv_ref[:, hh * HP:(hh + 1) * HP]).astype(BF16)


def _uv_project(olat, wuv):
    n = olat.shape[0]
    full = lambda *s: pl.BlockSpec(s, lambda i: (0,) * len(s))
    return pl.pallas_call(
        _uv_kernel, grid=(1,),
        in_specs=[full(n, N_HEADS * KV_LORA), full(KV_LORA, N_HEADS * HP)],
        out_specs=full(n, N_HEADS * HP),
        out_shape=jax.ShapeDtypeStruct((n, N_HEADS * HP), BF16),
        compiler_params=_cparams(("arbitrary",)),
        name="uv_sample",
    )(olat, wuv)


def _merge_kernel(oa_ref, wbra_ref, bg0_ref, part_ref, x_ref, gm_ref, wout_ref, gffn_ref, scf_ref, shf_ref,
                  x1_ref, h2_ref):
    bra = _dot(oa_ref[...], wbra_ref[...])
    merged = bg0_ref[...].astype(F32) * bra + part_ref[...]
    x1 = x_ref[...] + gm_ref[...] * _dot(merged.astype(BF16), wout_ref[...])
    x1_ref[...] = x1
    h2_ref[...] = (_rms_rows(x1) * gffn_ref[...] * (1.0 + scf_ref[...]) + shf_ref[...]).astype(BF16)


def _merge(oa, bg0, part, x, gate_m, scale_f, shift_f, lw, tm):
    g, t, d = x.shape
    mt = gate_m.shape[1]
    tok = lambda w: pl.BlockSpec((None, tm, w), lambda i, j: (i, j, 0))
    if mt == 1:
        mod = pl.BlockSpec((None, 1, d), lambda i, j: (i, 0, 0))
    else:
        mod = pl.BlockSpec((None, tm, d), lambda i, j: (i, j, 0))
    return pl.pallas_call(
        _merge_kernel,
        grid=(g, t // tm),
        in_specs=[tok(N_HEADS * HP), _const_spec((N_HEADS * HP, d)), tok(d), tok(d), tok(d), mod,
                  _const_spec((d, d)), _const_spec((1, d)), mod, mod],
        out_specs=[tok(d), tok(d)],
        out_shape=[jax.ShapeDtypeStruct((g, t, d), F32), jax.ShapeDtypeStruct((g, t, d), BF16)],
        compiler_params=_cparams(("arbitrary", "arbitrary")),
        name="merge",
    )(oa, lw["w_br_a"], bg0, part, x, gate_m, lw["w_out"], lw["g_norm_ffn"], scale_f, shift_f)


def _router_kernel(h_ref, wrt_ref, brt_ref, g_ref, idx_ref, w_ref):
    tm = h_ref.shape[0]
    logits = lax.dot_general(wrt_ref[...], h_ref[...], (((1,), (1,)), ((), ())), preferred_element_type=F32)
    scores = jax.nn.sigmoid(logits)
    choice = scores + brt_ref[...]
    neg = -jnp.inf

    gsc = []
    for g in range(N_EXPERT_GROUPS):
        xg = choice[g * GROUP_SIZE:(g + 1) * GROUP_SIZE, :]
        m1 = jnp.max(xg, axis=0, keepdims=True)
        cnt = jnp.sum(jnp.where(xg == m1, 1.0, 0.0), axis=0, keepdims=True)
        m2 = jnp.max(jnp.where(xg < m1, xg, neg), axis=0, keepdims=True)
        gsc.append(m1 + jnp.where(cnt >= 2.0, m1, m2))

    cands = []
    for g in range(N_EXPERT_GROUPS):
        rank = jnp.zeros((1, tm), F32)
        for o in range(N_EXPERT_GROUPS):
            if o == g:
                continue
            beats = (gsc[o] > gsc[g]) if o > g else (gsc[o] >= gsc[g])
            rank = rank + jnp.where(beats, 1.0, 0.0)
        keep = rank < float(TOPK_GROUPS)
        cands.append(jnp.where(keep, choice[g * GROUP_SIZE:(g + 1) * GROUP_SIZE, :], neg))
    cand = jnp.concatenate(cands, axis=0)

    ridx = lax.broadcasted_iota(jnp.int32, (N_EXPERTS, tm), 0)
    picked = jnp.zeros((N_EXPERTS, tm), F32)
    idxs, ws = [], []
    for _ in range(TOP_K):
        m = jnp.max(cand, axis=0, keepdims=True)
        idx = jnp.min(jnp.where(cand == m, ridx, N_EXPERTS), axis=0, keepdims=True)
        hit = ridx == idx
        picked = jnp.where(hit, 1.0, picked)
        cand = jnp.where(hit, neg, cand)
        idxs.append(idx)
        ws.append(jnp.sum(jnp.where(hit, scores, 0.0), axis=0, keepdims=True))

    w = jnp.where(picked > 0.0, scores, 0.0)
    norm = ROUTED_SCALE / jnp.sum(w, axis=0, keepdims=True)
    gt = w * norm
    g_ref[...] = jnp.concatenate([gt, jnp.zeros((LANES - N_EXPERTS, tm), F32)], axis=0).T
    idx_ref[...] = jnp.concatenate(idxs, axis=0)
    w_ref[...] = jnp.concatenate(ws, axis=0) * norm


def _router(h2, lw, tm):
    m, d = h2.shape
    return pl.pallas_call(
        _router_kernel,
        grid=(m // tm,),
        in_specs=[pl.BlockSpec((tm, d), lambda i: (i, 0)), _const_spec((N_EXPERTS, d)),
                  _const_spec((N_EXPERTS, 1))],
        out_specs=[pl.BlockSpec((tm, LANES), lambda i: (i, 0)), pl.BlockSpec((TOP_K, tm), lambda i: (0, i)),
                   pl.BlockSpec((TOP_K, tm), lambda i: (0, i))],
        out_shape=[jax.ShapeDtypeStruct((m, LANES), F32), jax.ShapeDtypeStruct((TOP_K, m), jnp.int32),
                   jax.ShapeDtypeStruct((TOP_K, m), F32)],
        compiler_params=_cparams(("arbitrary",)),
        name="router",
    )(h2, lw["w_router_t"], lw["b_router_t"])


def _swiglu(x, wgu):
    gu = _dot(x, wgu)
    return _silu(gu[:, :D_EXPERT]) * gu[:, D_EXPERT:]


def _expert_ffn(x, wg, wu, wd):
    h = _silu(_dot(x, wg.astype(BF16))) * _dot(x, wu.astype(BF16))
    return h, wd.astype(BF16)


def _moe_kernel(h_ref, g_ref, wg_ref, wu_ref, wd_ref, wsgu_ref, wsd_ref, x1_ref, gf_ref, o_ref, acc_ref, *, epb):
    e = pl.program_id(2)
    h = h_ref[...]
    tm = h.shape[0]

    @pl.when(e == 0)
    def _():
        acc_ref[...] = _dot(_swiglu(h, wsgu_ref[...]).astype(BF16), wsd_ref[...])

    lane = lax.broadcasted_iota(jnp.int32, (tm, LANES), 1)
    gates = g_ref[...]
    for jj in range(epb):
        gcol = jnp.sum(jnp.where(lane == e * epb + jj, gates, 0.0), axis=-1, keepdims=True)
        hh, wd = _expert_ffn(h, wg_ref[jj], wu_ref[jj], wd_ref[jj])
        acc_ref[...] += _dot((hh * gcol).astype(BF16), wd)

    @pl.when(e == pl.num_programs(2) - 1)
    def _():
        o_ref[...] = x1_ref[...] + gf_ref[...] * acc_ref[...]


def _moe(h2, gates, x1, gate_f, lw, tm):
    g, t, d = x1.shape
    mt = gate_f.shape[1]
    epb = EXPERTS_PER_STEP
    layer = lw["layer"]
    tok = lambda w: pl.BlockSpec((None, tm, w), lambda i, j, e: (i, j, 0))
    if mt == 1:
        mod = pl.BlockSpec((None, 1, d), lambda i, j, e: (i, 0, 0))
    else:
        mod = pl.BlockSpec((None, tm, d), lambda i, j, e: (i, j, 0))
    return pl.pallas_call(
        functools.partial(_moe_kernel, epb=epb),
        grid=(g, t // tm, N_EXPERTS // epb),
        in_specs=[tok(d), tok(LANES),
                  pl.BlockSpec((None, epb, d, D_EXPERT), lambda i, j, e: (layer, e, 0, 0)),
                  pl.BlockSpec((None, epb, d, D_EXPERT), lambda i, j, e: (layer, e, 0, 0)),
                  pl.BlockSpec((None, epb, D_EXPERT, d), lambda i, j, e: (layer, e, 0, 0)),
                  _const_spec((d, 2 * D_EXPERT)), _const_spec((D_EXPERT, d)),
                  tok(d), mod],
        out_specs=tok(d),
        out_shape=jax.ShapeDtypeStruct((g, t, d), F32),
        scratch_shapes=[pltpu.VMEM((tm, d), F32)],
        compiler_params=_cparams(("arbitrary", "arbitrary", "arbitrary")),
        name="moe",
    )(h2, gates, lw["w_e_gate"], lw["w_e_up"], lw["w_e_down"], lw["w_sh_gu"], lw["w_sh_down"], x1, gate_f)


def _dispatch_kernel(x_ref, idx_ref, xs_ref, posk_ref, nch_ref, off_ref, *, rc):
    x = x_ref[...]
    idx = idx_ref[...]
    nb = x.shape[0]

    eiota = lax.broadcasted_iota(jnp.int32, (N_EXPERTS, nb), 0)
    hits = [eiota == idx[k:k + 1, :] for k in range(TOP_K)]
    sel = jnp.zeros((N_EXPERTS, nb), F32)
    for hit in hits:
        sel = sel + jnp.where(hit, 1.0, 0.0)
    cnt = jnp.sum(sel, axis=1, keepdims=True)
    nch = jnp.floor((cnt + (MOE_CHUNK - 1.0)) * (1.0 / MOE_CHUNK))
    nch_b = jnp.broadcast_to(nch, (N_EXPERTS, LANES))
    earlier = (lax.broadcasted_iota(jnp.int32, (N_EXPERTS, N_EXPERTS), 1)
               < lax.broadcasted_iota(jnp.int32, (N_EXPERTS, N_EXPERTS), 0))
    off_b = _dot(jnp.where(earlier, 1.0, 0.0).astype(BF16), nch_b.astype(BF16))
    before = (lax.broadcasted_iota(jnp.int32, (nb, nb), 0) < lax.broadcasted_iota(jnp.int32, (nb, nb), 1))
    rank = _dot(sel.astype(BF16), jnp.where(before, 1.0, 0.0).astype(BF16))
    pos = off_b[:, 0:1] * float(MOE_CHUNK) + rank
    posk = jnp.concatenate([jnp.sum(jnp.where(hit, pos, 0.0), axis=0, keepdims=True) for hit in hits],
                           axis=0).astype(jnp.int32)
    posk_ref[...] = posk
    nch_ref[...] = nch_b.astype(jnp.int32)
    off_ref[...] = off_b.astype(jnp.int32)

    posk16 = posk.astype(jnp.int16)
    used_rows = jnp.sum(nch).astype(jnp.int32) * MOE_CHUNK

    def fill(c):
        riota = (c * rc + lax.broadcasted_iota(jnp.int32, (rc, nb), 0)).astype(jnp.int16)
        p = jnp.zeros((rc, nb), BF16)
        for k in range(TOP_K):
            p = jnp.where(riota == posk16[k:k + 1, :], jnp.ones((), BF16), p)
        xs_ref[c * rc:(c + 1) * rc, :] = _dot(p, x).astype(BF16)

    for c in range(MOE_ROWS // rc):
        if (c + 1) * rc <= nb * TOP_K:
            fill(c)
        else:
            pl.when(c * rc < used_rows)(functools.partial(fill, c))

            @pl.when(c * rc >= used_rows)
            def _():
                xs_ref[c * rc:(c + 1) * rc, :] = jnp.zeros((rc, xs_ref.shape[-1]), BF16)


def _dispatch(h2, idx_t):
    nblk, nb, d = h2.shape
    small = pl.BlockSpec((None, N_EXPERTS, LANES), lambda i: (i, 0, 0))
    return pl.pallas_call(
        functools.partial(_dispatch_kernel, rc=512),
        grid=(nblk,),
        in_specs=[pl.BlockSpec((None, nb, d), lambda i: (i, 0, 0)),
                  pl.BlockSpec((TOP_K, nb), lambda i: (0, i))],
        out_specs=[pl.BlockSpec((None, MOE_ROWS, d), lambda i: (i, 0, 0)),
                   pl.BlockSpec((None, TOP_K, nb), lambda i: (i, 0, 0)), small, small],
        out_shape=[jax.ShapeDtypeStruct((nblk, MOE_ROWS, d), BF16),
                   jax.ShapeDtypeStruct((nblk, TOP_K, nb), jnp.int32),
                   jax.ShapeDtypeStruct((nblk, N_EXPERTS, LANES), jnp.int32),
                   jax.ShapeDtypeStruct((nblk, N_EXPERTS, LANES), jnp.int32)],
        compiler_params=_cparams(("arbitrary",)),
        name="moe_dispatch",
    )(h2, idx_t)


def _start_all(copies):
    for cp in copies:
        cp.start()


def _expert_kernel(te_ref, src_ref, nact_ref, xs_ref, wg_ref, wu_ref, wd_ref, ys_ref, xbuf, sem):
    t = pl.program_id(0)
    nact = nact_ref[0]
    ahead = EXP_BUFFERS - 1
    slot = t % EXP_BUFFERS

    def copies(tile, buf_slot):
        return [pltpu.make_async_copy(xs_ref.at[src_ref[tile * EXP_CHUNKS + c]], xbuf.at[buf_slot, c],
                                      sem.at[buf_slot]) for c in range(EXP_CHUNKS)]

    @pl.when(t == 0)
    def _():
        for a in range(ahead):
            @pl.when(a < nact)
            def _():
                _start_all(copies(a, a))

    @pl.when(t + ahead < nact)
    def _():
        _start_all(copies(t + ahead, (t + ahead) % EXP_BUFFERS))

    @pl.when(t < nact)
    def _():
        for cp in copies(t, slot):
            cp.wait()
        x = xbuf[slot].reshape(EXP_CHUNKS * MOE_CHUNK, xbuf.shape[-1])
        hh, wd = _expert_ffn(x, wg_ref[...], wu_ref[...], wd_ref[...])
        ys_ref[...] = _dot(hh.astype(BF16), wd).astype(BF16)

    @pl.when(t >= nact)
    def _():
        ys_ref[...] = jnp.zeros(ys_ref.shape, BF16)


def _experts(xs_chunks, tile_expert, src_chunk, n_active, lw):
    d = xs_chunks.shape[-1]
    n_tiles = tile_expert.shape[0]
    tm = EXP_CHUNKS * MOE_CHUNK
    layer = lw["layer"]
    grid_spec = pltpu.PrefetchScalarGridSpec(
        num_scalar_prefetch=3,
        grid=(n_tiles,),
        in_specs=[pl.BlockSpec(memory_space=pl.ANY),
                  pl.BlockSpec((None, None, d, D_EXPERT), lambda t, te, src, na: (layer, te[t], 0, 0)),
                  pl.BlockSpec((None, None, d, D_EXPERT), lambda t, te, src, na: (layer, te[t], 0, 0)),
                  pl.BlockSpec((None, None, D_EXPERT, d), lambda t, te, src, na: (layer, te[t], 0, 0))],
        out_specs=pl.BlockSpec((tm, d), lambda t, te, src, na: (t, 0)),
        scratch_shapes=[pltpu.VMEM((EXP_BUFFERS, EXP_CHUNKS, MOE_CHUNK, d), BF16),
                        pltpu.SemaphoreType.DMA((EXP_BUFFERS,))],
    )
    return pl.pallas_call(
        _expert_kernel,
        grid_spec=grid_spec,
        out_shape=jax.ShapeDtypeStruct((n_tiles * tm, d), BF16),
        compiler_params=_cparams(("arbitrary",)),
        name="moe_experts",
    )(tile_expert, src_chunk, n_active, xs_chunks, lw["w_e_gate"], lw["w_e_up"], lw["w_e_down"])


def _combine_kernel(src_ref, used_ref, ys_ref, pos_ref, w_ref, h_ref, x1_ref, gf_ref, wsgu_ref, wsd_ref,
                    o_ref, ybuf, sem, acc_ref, *, cc):
    b = pl.program_id(0)
    last = pl.num_programs(0) - 1
    ahead = COMBINE_BUFFERS - 1
    slot = b % COMBINE_BUFFERS
    nxt_slot = (b + ahead) % COMBINE_BUFFERS
    nxt = jnp.minimum(b + ahead, last)
    nch = MOE_ROWS // MOE_CHUNK

    def copies(blk, buf_slot, lo=0, hi=nch):
        return [pltpu.make_async_copy(ys_ref.at[src_ref[blk * nch + c]], ybuf.at[buf_slot, c], sem.at[buf_slot])
                for c in range(lo, hi)]

    @pl.when(b == 0)
    def _():
        for a in range(ahead):
            _start_all(copies(jnp.minimum(a, last), a))

    acc_ref[...] = _dot(_swiglu(h_ref[...], wsgu_ref[...]).astype(BF16), wsd_ref[...])

    for cp in copies(b, slot):
        cp.wait()
    pos = pos_ref[...].astype(jnp.int16)
    w = w_ref[...].astype(BF16)
    nb = pos.shape[0]
    per = cc // MOE_CHUNK
    used_rows = used_ref[b]

    def fold(c):
        liota = (c * cc + lax.broadcasted_iota(jnp.int32, (nb, cc), 1)).astype(jnp.int16)
        pw = jnp.zeros((nb, cc), BF16)
        for k in range(TOP_K):
            pw = jnp.where(liota == pos[:, k:k + 1], w[:, k:k + 1], pw)
        y = ybuf[slot, c * per:(c + 1) * per].reshape(cc, ybuf.shape[-1])
        acc_ref[...] += _dot(pw, y)

    for c in range(MOE_ROWS // cc):
        if (c + 1) * cc <= nb * TOP_K:
            fold(c)
        else:
            pl.when(c * cc < used_rows)(functools.partial(fold, c))
        _start_all(copies(nxt, nxt_slot, c * per, (c + 1) * per))
    o_ref[...] = x1_ref[...] + gf_ref[...] * acc_ref[...]

    @pl.when(b == last)
    def _():
        for a in range(1, COMBINE_BUFFERS):
            for cp in copies(nxt, (b + a) % COMBINE_BUFFERS):
                cp.wait()


def _combine(ys_chunks, src_chunk, used_rows, posk, wk, h2, x1, gate_f, lw, blocks_per_seq):
    nblk, nb, d = h2.shape
    nch = MOE_ROWS // MOE_CHUNK
    blk = lambda w: pl.BlockSpec((None, nb, w), lambda i, src, used: (i, 0, 0))
    const = lambda *s: pl.BlockSpec(s, lambda i, src, used: (0,) * len(s))
    grid_spec = pltpu.PrefetchScalarGridSpec(
        num_scalar_prefetch=2,
        grid=(nblk,),
        in_specs=[pl.BlockSpec(memory_space=pl.ANY), blk(TOP_K), blk(TOP_K), blk(d), blk(d),
                  pl.BlockSpec((None, 1, d), lambda i, src, used: (i // blocks_per_seq, 0, 0)),
                  const(d, 2 * D_EXPERT), const(D_EXPERT, d)],
        out_specs=blk(d),
        scratch_shapes=[pltpu.VMEM((COMBINE_BUFFERS, nch, MOE_CHUNK, d), BF16),
                        pltpu.SemaphoreType.DMA((COMBINE_BUFFERS,)), pltpu.VMEM((nb, d), F32)],
    )
    return pl.pallas_call(
        functools.partial(_combine_kernel, cc=512),
        grid_spec=grid_spec,
        out_shape=jax.ShapeDtypeStruct((nblk, nb, d), F32),
        compiler_params=_cparams(("arbitrary",)),
        name="moe_combine",
    )(src_chunk, used_rows, ys_chunks, posk, wk, h2, x1, gate_f, lw["w_sh_gu"], lw["w_sh_down"])


def _excl_cumsum(x, axis):
    n = x.shape[axis]
    earlier = jnp.arange(n)[:, None] > jnp.arange(n)[None, :]
    xm = jnp.moveaxis(x, axis, -1)
    out = jnp.sum(jnp.where(earlier, xm[..., None, :], 0), axis=-1)
    return jnp.moveaxis(out, -1, axis)


def _pick(onehot, table):
    return jnp.sum(jnp.where(onehot, table, 0), axis=-1)


def _moe_plan(nch, off, n_tokens):
    nblk = nch.shape[0]
    i32 = jnp.int32
    cpb = MOE_ROWS // MOE_CHUNK
    experts = jnp.arange(N_EXPERTS, dtype=i32)
    tot = jnp.sum(nch, axis=1)
    ech = jnp.sum(nch, axis=0)
    tiles_e = (ech + EXP_CHUNKS - 1) // EXP_CHUNKS
    tile_start = _excl_cumsum(tiles_e, 0)
    tile_end = tile_start + tiles_e
    n_active = jnp.sum(tiles_e)
    cumb = _excl_cumsum(nch, 0)

    n_tiles = _max_expert_tiles(n_tokens, nblk)
    tid = jnp.arange(n_tiles, dtype=i32)
    te = jnp.sum((tid[:, None] >= tile_end[None, :]).astype(i32), axis=1)
    te_last = jnp.sum((n_active - 1 >= tile_end).astype(i32))
    te = jnp.where(tid < n_active, te, te_last)
    is_e = te[:, None] == experts[None, :]
    ts_t = _pick(is_e, tile_start[None, :])
    ech_t = _pick(is_e, ech[None, :])
    ends_t = _pick(is_e[:, None, :], (cumb + nch)[None, :, :])
    off_t = _pick(is_e[:, None, :], off[None, :, :])
    cumb_t = _pick(is_e[:, None, :], cumb[None, :, :])
    c = (tid - ts_t)[:, None] * EXP_CHUNKS + jnp.arange(EXP_CHUNKS, dtype=i32)[None, :]
    valid = jnp.logical_and(c < ech_t[:, None], (tid < n_active)[:, None])
    sb = jnp.minimum(jnp.sum((c[:, :, None] >= ends_t[:, None, :]).astype(i32), axis=2), nblk - 1)
    is_b = sb[:, :, None] == jnp.arange(nblk, dtype=i32)[None, None, :]
    src_e = jnp.where(valid, sb * cpb + _pick(is_b, off_t[:, None, :]) + c - _pick(is_b, cumb_t[:, None, :]), 0)

    j = jnp.arange(cpb, dtype=i32)
    ce = jnp.minimum(jnp.sum((j[None, :, None] >= (off + nch)[:, None, :]).astype(i32), axis=2), N_EXPERTS - 1)
    is_ce = ce[:, :, None] == experts[None, None, :]
    g = (_pick(is_ce, tile_start[None, None, :]) * EXP_CHUNKS + _pick(is_ce, cumb[:, None, :])
         + j[None, :] - _pick(is_ce, off[:, None, :]))
    src_c = jnp.where(j[None, :] < tot[:, None], g, 0)
    return te, src_e.reshape(-1), n_active.reshape(1), src_c.reshape(-1), tot * MOE_CHUNK


def _max_expert_tiles(t, nblk):
    max_chunks = t * TOP_K // MOE_CHUNK + nblk * N_EXPERTS
    return max_chunks // EXP_CHUNKS + N_EXPERTS


def _moe_sparse(h2, idx_t, w_t, x1, gate_f, lw):
    b, t, d = x1.shape
    nblk = b * t // MOE_BLOCK
    h2b = h2.reshape(nblk, MOE_BLOCK, d)
    xs, posk_t, nch, off = _dispatch(h2b, idx_t)
    te, src_e, n_active, src_c, used_rows = _moe_plan(nch[:, :, 0], off[:, :, 0], b * t)
    ys = _experts(xs.reshape(nblk * MOE_ROWS // MOE_CHUNK, MOE_CHUNK, d), te, src_e, n_active, lw)
    wk = w_t.reshape(TOP_K, nblk, MOE_BLOCK).transpose(1, 2, 0)
    out = _combine(ys.reshape(-1, MOE_CHUNK, d), src_c, used_rows, posk_t.transpose(0, 2, 1), wk,
                   h2b, x1.reshape(nblk, MOE_BLOCK, d), gate_f, lw, t // MOE_BLOCK)
    return out.reshape(b, t, d)


def _pad_heads(w, width):
    pad = [(0, 0)] * (w.ndim - 1) + [(0, HP - width)]
    wp = jnp.pad(w, pad)
    return wp.reshape(w.shape[:-2] + (w.shape[-2] * HP,))


def _rot_cols(w):
    half = QK_ROPE // 2
    return jnp.concatenate([-w[..., half:], w[..., :half]], axis=-1)


def _prep_layer(l, p):
    d = D_MODEL
    w_in = p["w_in"][l]
    off_kv = Q_LORA
    off_kr = off_kv + KV_LORA
    off_cb = off_kr + QK_ROPE
    off_g = off_cb + 3 * CONV_DIM + 2 * GM_DIM
    wkr = w_in[:, off_kr:off_cb]
    lead = jnp.zeros((d, QK_NOPE), F32)
    trail = jnp.zeros((d, HP - QK_DIM), F32)
    w2 = jnp.concatenate([
        w_in[:, :off_kr],
        lead, wkr, trail,
        lead, _rot_cols(wkr), trail,
        w_in[:, off_cb:],
    ], axis=1).astype(BF16)
    assert w2.shape[1] == S_END and off_g + N_BRANCH * d == w_in.shape[1]

    wuq = p["w_uq"][l]
    wuq_rot = jnp.concatenate([jnp.zeros_like(wuq[..., :QK_NOPE]), _rot_cols(wuq[..., QK_NOPE:])], axis=-1)
    wuq2 = jnp.concatenate([_pad_heads(wuq, QK_DIM), _pad_heads(wuq_rot, QK_DIM)], axis=-1).astype(BF16)

    def pad_gain(gq):
        return jnp.pad(gq, (0, HP - QK_DIM)).reshape(1, HP)

    w_br_a = jnp.pad(p["w_br_a"][l].reshape(N_HEADS, V_DIM, d), ((0, 0), (0, HP - V_DIM), (0, 0)))
    w_sp = p["w_spatial"][l]
    b_sp = jnp.repeat(p["b_spatial"][l].T, GM_GROUP_DIM, axis=1)
    return dict(
        g_norm_mix=p["g_norm_mix"][l].reshape(1, d), g_norm_ffn=p["g_norm_ffn"][l].reshape(1, d),
        w2=w2, g_q_lat=p["g_q_lat"][l].reshape(1, Q_LORA), wuq2=wuq2,
        g_qk_q=pad_gain(p["g_qk_q"][l]), g_qk_k=pad_gain(p["g_qk_k"][l]),
        g_kv_lat=p["g_kv_lat"][l].reshape(1, KV_LORA),
        wuk=_pad_heads(p["w_uk"][l], QK_NOPE).astype(BF16),
        wkt=p["w_uk"][l].transpose(2, 1, 0).reshape(QK_NOPE * N_HEADS, KV_LORA).astype(BF16),
        wukt_pad=jnp.pad(p["w_uk"][l].transpose(1, 2, 0), ((0, 0), (0, HP - QK_NOPE), (0, 0))).astype(BF16),
        wuv=_pad_heads(p["w_uv"][l], V_DIM).astype(BF16),
        w_br_a=w_br_a.reshape(N_HEADS * HP, d).astype(BF16),
        w_conv=p["w_conv"][l], w_br_b=p["w_br_b"][l].astype(BF16),
        g_v_ln=p["g_v_ln"][l].reshape(1, GM_DIM),
        w_sp=w_sp.reshape(GM_GROUPS * CHUNK, CHUNK), b_sp=b_sp,
        sp_coef=jnp.repeat(w_sp[:, 0, 0], GM_GROUP_DIM).reshape(1, GM_DIM),
        w_br_c=p["w_br_c"][l].astype(BF16), w_out=p["w_out"][l].astype(BF16),
        w_router_t=p["w_router"][l].T.astype(BF16),
        b_router_t=p["b_router"][l].reshape(N_EXPERTS, 1),
        w_sh_gu=jnp.concatenate([p["w_sh_gate"][l], p["w_sh_up"][l]], axis=-1).astype(BF16),
        w_sh_down=p["w_sh_down"][l].astype(BF16),
    )


def _rope_tables(pos):
    inv_freq = ROPE_THETA ** (-jnp.arange(0, QK_ROPE, 2, dtype=F32) / QK_ROPE)
    ang = pos.astype(F32)[:, None] * inv_freq[None, :]
    c, s = jnp.cos(ang), jnp.sin(ang)
    n = pos.shape[0]
    cos = jnp.concatenate([jnp.ones((n, QK_NOPE), F32), c, c, jnp.ones((n, HP - QK_DIM), F32)], axis=1)
    sin = jnp.concatenate([jnp.zeros((n, QK_NOPE), F32), s, s, jnp.zeros((n, HP - QK_DIM), F32)], axis=1)
    return cos, sin


def _split_mod(m):
    return [m[:, i * D_MODEL:(i + 1) * D_MODEL] for i in range(6)]


def _prompt_layer(x, mod, lw, cos, sin):
    b, t, d = x.shape
    shift_m, scale_m, gate_m, shift_f, scale_f, gate_f = [a.reshape(b, 1, d) for a in _split_mod(mod)]
    q, k, v, ckv, krp, tail, cv, bg0, part = _inproj_prompt(x, scale_m, shift_m, lw, cos, sin)
    oa = _flash_attention(q, k, v)
    x1, h2 = _merge(oa, bg0, part, x, gate_m, scale_f, shift_f, lw, TM_MERGE)
    _, idx_t, w_t = _router(h2.reshape(b * t, d), lw, TM_ROUTE)
    y = _moe_sparse(h2, idx_t, w_t, x1, gate_f, lw)
    state = (ckv, krp[:, :, QK_NOPE:QK_DIM], tail[:, 8 - (CONV_W - 1):], cv)
    return y, state


def _sample_layer(layer, x, mod, lw, cos, sin, hist, pool_c, pool_krt, page_table):
    n, d = x.shape
    shift_m, scale_m, gate_m, shift_f, scale_f, gate_f = _split_mod(mod)
    qg, ql, ckv, krp, z, cv, bg0, part = _inproj_sample(x, scale_m, shift_m, lw, cos, sin, hist[:, 0], hist[:, 1])
    head_pad = ((0, 0), (0, HEAD_ROWS - N_HEADS), (0, 0))
    qlat = jnp.pad(ql.transpose(1, 0, 2), head_pad).astype(BF16)
    qr = jnp.pad(qg[:, :, QK_NOPE:QK_DIM].transpose(1, 0, 2), head_pad).astype(BF16)
    kr = krp[:, QK_NOPE:QK_DIM]
    krnew_t = jnp.pad(kr[:, :, None], ((0, 0), (0, 0), (0, LANES - 1)))
    olat = _paged_attention(layer, page_table, lw["wkt"], qlat, qr, ckv.reshape(n, 1, KV_LORA), krnew_t,
                            pool_c, pool_krt)
    oa = _uv_project(olat.reshape(n, N_HEADS * KV_LORA), lw["wuv"])
    g3 = lambda a: a.reshape(1, n, -1)
    x1, h2 = _merge(g3(oa), g3(bg0), g3(part), g3(x), g3(gate_m), g3(scale_f), g3(shift_f), lw, n)
    gates, _, _ = _router(h2.reshape(n, d), lw, n)
    y = _moe(h2, g3(gates), x1, g3(gate_f), lw, n).reshape(n, d)
    state = (ckv.reshape(n, 1, KV_LORA), kr.reshape(n, 1, QK_ROPE),
             jnp.stack([hist[:, 1], z], axis=1), cv.reshape(n, 1, GM_DIM))
    return y, state


def kernel(x_prompt, x_sample, cache_kv_latent, cache_k_rope, state_conv, page_table, c_prompt, c_sample,
           w_ada, b_ada, g_norm_mix, g_norm_ffn, w_in, g_q_lat, w_uq, g_kv_lat, w_uk, w_uv, g_qk_q, g_qk_k,
           w_br_a, w_conv, w_br_b, g_v_ln, w_spatial, b_spatial, w_br_c, w_out, w_router, b_router,
           w_e_gate, w_e_up, w_e_down, w_sh_gate, w_sh_up, w_sh_down):
    params = dict(w_ada=w_ada, b_ada=b_ada, g_norm_mix=g_norm_mix, g_norm_ffn=g_norm_ffn, w_in=w_in,
                  g_q_lat=g_q_lat, w_uq=w_uq, g_kv_lat=g_kv_lat, w_uk=w_uk, w_uv=w_uv, g_qk_q=g_qk_q,
                  g_qk_k=g_qk_k, w_br_a=w_br_a, w_conv=w_conv, w_br_b=w_br_b, g_v_ln=g_v_ln,
                  w_spatial=w_spatial, b_spatial=b_spatial, w_br_c=w_br_c, w_out=w_out, w_router=w_router,
                  b_router=b_router, w_e_gate=w_e_gate, w_e_up=w_e_up, w_e_down=w_e_down,
                  w_sh_gate=w_sh_gate, w_sh_up=w_sh_up, w_sh_down=w_sh_down)
    depth = w_in.shape[0]
    nb, t, d = x_prompt.shape
    ns = x_sample.shape[0]
    assert x_sample.shape[1] == 1 and t % TQ == 0 and t % MOE_BLOCK == 0
    past_len = page_table.shape[1] * PAGE_SIZE

    cos_p, sin_p = _rope_tables(jnp.arange(t))
    cos_s, sin_s = _rope_tables(past_len + jnp.arange(1))
    c_all = jnp.concatenate([c_prompt, c_sample], axis=0)
    cache_krt = jnp.swapaxes(cache_k_rope, 2, 3)
    b_ada3 = b_ada.reshape(depth, 1, -1)

    hp, hs = x_prompt, x_sample.reshape(ns, d)
    st_p, st_s = [], []
    for l in range(depth):
        lw = _prep_layer(l, params)
        lw.update(layer=l, w_e_gate=w_e_gate, w_e_up=w_e_up, w_e_down=w_e_down)
        mod = _adaln(l, c_all, w_ada, b_ada3)
        hp, sp = _prompt_layer(hp, mod[:nb], lw, cos_p, sin_p)
        hs, ss = _sample_layer(l, hs, mod[nb:], lw, cos_s, sin_s, state_conv[l], cache_kv_latent, cache_krt,
                               page_table)
        st_p.append(sp)
        st_s.append(ss)

    stack = lambda sts, i: jnp.stack([s[i] for s in sts])
    return (hp, hs.reshape(ns, 1, d),
            stack(st_p, 0), stack(st_p, 1), stack(st_p, 2), stack(st_p, 3),
            stack(st_s, 0), stack(st_s, 1), stack(st_s, 2), stack(st_s, 3))
```

```python
import functools

import jax
import jax.numpy as jnp
from jax import lax
from jax.experimental import pallas as pl
from jax.experimental.pallas import tpu as pltpu

F32 = jnp.float32
BF16 = jnp.bfloat16

D_MODEL = 1024
N_HEADS = 8
QK_NOPE = 64
QK_ROPE = 32
QK_DIM = QK_NOPE + QK_ROPE
V_DIM = 64
Q_LORA = 384
KV_LORA = 256
ROPE_THETA = 10000.0
CONV_W = 3
CONV_DIM = 512
CHUNK = 128
GM_GROUPS = 8
GM_DIM = 512
GM_GROUP_DIM = GM_DIM // GM_GROUPS
N_EXPERTS = 64
TOP_K = 8
N_EXPERT_GROUPS = 8
GROUP_SIZE = N_EXPERTS // N_EXPERT_GROUPS
TOPK_GROUPS = 4
D_EXPERT = 256
ROUTED_SCALE = 2.5
N_BRANCH = 3
EPS = 1e-6
PAGE_SIZE = 128

LANES = 128
HP = LANES
VMEM_LIMIT = 56 * 1024 * 1024

S_Q = 0
S_KV = S_Q + Q_LORA
S_KRM = S_KV + KV_LORA
S_KRR = S_KRM + HP
S_CB = S_KRR + HP
S_CC = S_CB + CONV_DIM
S_CH = S_CC + CONV_DIM
S_U = S_CH + CONV_DIM
S_V = S_U + GM_DIM
S_G = S_V + GM_DIM
S_END = S_G + N_BRANCH * D_MODEL

TM_IN = 512
TQ = 512
TK = 512
TM_MERGE = 512
TM_ROUTE = 512
TM_MOE = 1024
EXPERTS_PER_STEP = 2
MOE_BLOCK = 256
MOE_CHUNK = 16
EXP_CHUNKS = 64
EXP_BUFFERS = 3
COMBINE_BUFFERS = 3
MOE_ROWS = -(-(MOE_BLOCK * TOP_K + N_EXPERTS * (MOE_CHUNK - 1)) // 512) * 512
PAGES_PER_STEP = 64
PAGES_PER_CHUNK = 16
PAGE_BUFFERS = 3
HEAD_ROWS = 16


def _cparams(sem):
    return pltpu.CompilerParams(dimension_semantics=sem, vmem_limit_bytes=VMEM_LIMIT)


def _const_spec(shape):
    nd = len(shape)
    return pl.BlockSpec(shape, lambda *_: (0,) * nd, pipeline_mode=pl.Buffered(1))


def _dot(a, b):
    return jnp.dot(a, b, preferred_element_type=F32)


def _rms_rows(x):
    return x * lax.rsqrt(jnp.mean(x * x, axis=-1, keepdims=True) + EPS)


def _silu(x):
    return x * jax.nn.sigmoid(x)


def _adaln_kernel(c_ref, w_ref, b_ref, o_ref):
    s = _silu(c_ref[...]).astype(BF16)
    o_ref[...] = _dot(s, w_ref[...].astype(BF16)) + b_ref[...]


def _adaln(layer, c_all, w_ada, b_ada):
    n, d = c_all.shape
    e = w_ada.shape[2]
    tn = 1536
    return pl.pallas_call(
        _adaln_kernel,
        grid=(e // tn,),
        in_specs=[pl.BlockSpec((n, d), lambda j: (0, 0)),
                  pl.BlockSpec((None, d, tn), lambda j: (layer, 0, j)),
                  pl.BlockSpec((None, 1, tn), lambda j: (layer, 0, j))],
        out_specs=pl.BlockSpec((n, tn), lambda j: (0, j)),
        out_shape=jax.ShapeDtypeStruct((n, e), F32),
        compiler_params=_cparams(("arbitrary",)),
        name="adaln",
    )(c_all, w_ada, b_ada)


def _inproj_common(x_ref, sc_ref, sh_ref, gmix_ref, w2_ref):
    h = (_rms_rows(x_ref[...]) * gmix_ref[...] * (1.0 + sc_ref[...]) + sh_ref[...]).astype(BF16)

    def seg(a, b):
        return _dot(h, w2_ref[:, a:b])

    return seg


def _heads_q(seg, gq_ref, wuq_ref, gqq_ref, cos, sin):
    cqn = (_rms_rows(seg(S_Q, S_KV)) * gq_ref[...]).astype(BF16)
    q2 = _dot(cqn, wuq_ref[...])
    out = []
    for hh in range(N_HEADS):
        qm = q2[:, hh * HP:(hh + 1) * HP]
        qr = q2[:, (N_HEADS + hh) * HP:(N_HEADS + hh + 1) * HP]
        qh = qm * cos + qr * sin
        inv = lax.rsqrt(jnp.sum(qh * qh, axis=-1, keepdims=True) * (1.0 / QK_DIM) + EPS)
        out.append(qh * inv * gqq_ref[...] * (QK_DIM ** -0.5))
    return out


def _latent_kv(seg, gkv_ref, cos, sin):
    ckv = _rms_rows(seg(S_KV, S_KRM)) * gkv_ref[...]
    krp = seg(S_KRM, S_KRR) * cos + seg(S_KRR, S_CB) * sin
    return ckv, krp


def _gates_and_partial(seg, brb_in, brc_in, wbrb_ref, wbrc_ref, bg0_ref, part_ref):
    brb = _dot(brb_in.astype(BF16), wbrb_ref[...])
    brc = _dot(brc_in.astype(BF16), wbrc_ref[...])
    bg = jax.nn.sigmoid(seg(S_G, S_END))
    bg0_ref[...] = bg[:, :D_MODEL].astype(BF16)
    part_ref[...] = bg[:, D_MODEL:2 * D_MODEL] * brb + bg[:, 2 * D_MODEL:] * brc


def _layernorm_rows(v, g):
    xc = v - jnp.mean(v, axis=-1, keepdims=True)
    return xc * lax.rsqrt(jnp.mean(xc * xc, axis=-1, keepdims=True) + EPS) * g


def _inproj_prompt_kernel(x_ref, sc_ref, sh_ref, gmix_ref, w2_ref, gq_ref, wuq_ref, gqq_ref,
                          gkv_ref, wuk_ref, gqk_ref, wuv_ref, cos_ref, sin_ref, wconv_ref,
                          wbrb_ref, gvln_ref, wsp_ref, bsp_ref, wbrc_ref,
                          q_ref, k_ref, v_ref, ckv_ref, kr_ref, tail_ref, cv_ref, bg0_ref, part_ref,
                          zbuf_ref, *, tm):
    t = pl.program_id(1)
    seg = _inproj_common(x_ref, sc_ref, sh_ref, gmix_ref, w2_ref)
    cos = cos_ref[...]
    sin = sin_ref[...]

    qs = _heads_q(seg, gq_ref, wuq_ref, gqq_ref, cos, sin)
    for hh in range(N_HEADS):
        q_ref[hh] = qs[hh].astype(BF16)
    ckv, krp = _latent_kv(seg, gkv_ref, cos, sin)
    ckv_ref[...] = ckv
    kr_ref[...] = krp
    ckv_b = ckv.astype(BF16)
    k2 = _dot(ckv_b, wuk_ref[...])
    v2 = _dot(ckv_b, wuv_ref[...])
    ones_lane = jnp.where(lax.broadcasted_iota(jnp.int32, (1, HP), 1) == V_DIM, 1.0, 0.0)
    for hh in range(N_HEADS):
        kh = k2[:, hh * HP:(hh + 1) * HP] + krp
        inv = lax.rsqrt(jnp.sum(kh * kh, axis=-1, keepdims=True) * (1.0 / QK_DIM) + EPS)
        k_ref[hh] = (kh * inv * gqk_ref[...]).astype(BF16)
        v_ref[hh] = (v2[:, hh * HP:(hh + 1) * HP] + ones_lane).astype(BF16)

    gate_b = seg(S_CB, S_CC)
    z = seg(S_CC, S_CH) * seg(S_CH, S_U)

    @pl.when(t == 0)
    def _():
        zbuf_ref[0:8, :] = jnp.zeros((8, CONV_DIM), F32)

    zbuf_ref[8:8 + tm, :] = z
    z1 = zbuf_ref[7:7 + tm, :]
    z2 = zbuf_ref[6:6 + tm, :]
    wc = wconv_ref[...]
    y = wc[0:1, :] * z2 + wc[1:2, :] * z1 + wc[2:3, :] * z
    zbuf_ref[0:8, :] = z[tm - 8:tm, :]
    tail_ref[...] = z[tm - 8:tm, :]

    u = seg(S_U, S_V)
    vn = _layernorm_rows(seg(S_V, S_G), gvln_ref[...])
    cv_ref[...] = vn[tm - CHUNK:tm, :]
    vnb = vn.astype(BF16)
    rows = lax.broadcasted_iota(jnp.int32, (GM_GROUPS * CHUNK, CHUNK), 0) % CHUNK
    cols = lax.broadcasted_iota(jnp.int32, (GM_GROUPS * CHUNK, CHUNK), 1)
    wsp = jnp.where(cols <= rows, wsp_ref[...], 0.0).astype(BF16)
    lane_grp = lax.broadcasted_iota(jnp.int32, (CHUNK, GM_DIM), 1) // GM_GROUP_DIM
    sgs = []
    for c in range(tm // CHUNK):
        r = _dot(wsp, vnb[c * CHUNK:(c + 1) * CHUNK, :])
        s = bsp_ref[...]
        for g in range(GM_GROUPS):
            s = s + jnp.where(lane_grp == g, r[g * CHUNK:(g + 1) * CHUNK, :], 0.0)
        sgs.append(u[c * CHUNK:(c + 1) * CHUNK, :] * s)
    sg = jnp.concatenate(sgs, axis=0)

    _gates_and_partial(seg, gate_b * y, sg, wbrb_ref, wbrc_ref, bg0_ref, part_ref)


def _inproj_sample_kernel(x_ref, sc_ref, sh_ref, gmix_ref, w2_ref, gq_ref, wuq_ref, gqq_ref,
                          gkv_ref, gqk_ref, wukt_ref, cos_ref, sin_ref, wconv_ref, h0_ref, h1_ref,
                          wbrb_ref, gvln_ref, coef_ref, bias_ref, wbrc_ref,
                          qg_ref, ql_ref, ckv_ref, kr_ref, z_ref, cv_ref, bg0_ref, part_ref):
    seg = _inproj_common(x_ref, sc_ref, sh_ref, gmix_ref, w2_ref)
    cos = cos_ref[...]
    sin = sin_ref[...]
    qs = _heads_q(seg, gq_ref, wuq_ref, gqq_ref, cos, sin)
    for hh in range(N_HEADS):
        qg = qs[hh] * gqk_ref[...]
        qg_ref[hh] = qg
        ql_ref[hh] = _dot(qg.astype(BF16), wukt_ref[hh])
    ckv, krp = _latent_kv(seg, gkv_ref, cos, sin)
    ckv_ref[...] = ckv
    kr_ref[...] = krp

    gate_b = seg(S_CB, S_CC)
    z = seg(S_CC, S_CH) * seg(S_CH, S_U)
    wc = wconv_ref[...]
    y = wc[0:1, :] * h0_ref[...] + wc[1:2, :] * h1_ref[...] + wc[2:3, :] * z
    z_ref[...] = z

    u = seg(S_U, S_V)
    vn = _layernorm_rows(seg(S_V, S_G), gvln_ref[...])
    cv_ref[...] = vn
    sg = u * (vn * coef_ref[...] + bias_ref[...])

    _gates_and_partial(seg, gate_b * y, sg, wbrb_ref, wbrc_ref, bg0_ref, part_ref)


def _inproj_prompt(x, scale_m, shift_m, lw, cos, sin):
    b, t, d = x.shape
    tm = TM_IN
    nt = t // tm
    tok = lambda w: pl.BlockSpec((None, tm, w), lambda i, j: (i, j, 0))
    mod = pl.BlockSpec((None, 1, d), lambda i, j: (i, 0, 0))
    head = pl.BlockSpec((None, N_HEADS, tm, HP), lambda i, j: (i, 0, j, 0))
    in_specs = [
        tok(d), mod, mod, _const_spec((1, d)), _const_spec((d, S_END)),
        _const_spec((1, Q_LORA)), _const_spec((Q_LORA, 2 * N_HEADS * HP)), _const_spec((1, HP)),
        _const_spec((1, KV_LORA)), _const_spec((KV_LORA, N_HEADS * HP)), _const_spec((1, HP)),
        _const_spec((KV_LORA, N_HEADS * HP)),
        pl.BlockSpec((tm, HP), lambda i, j: (j, 0)), pl.BlockSpec((tm, HP), lambda i, j: (j, 0)),
        _const_spec((CONV_W, CONV_DIM)), _const_spec((CONV_DIM, d)), _const_spec((1, GM_DIM)),
        _const_spec((GM_GROUPS * CHUNK, CHUNK)), _const_spec((CHUNK, GM_DIM)), _const_spec((GM_DIM, d)),
    ]
    out_specs = [
        head, head, head, tok(KV_LORA), tok(HP),
        pl.BlockSpec((None, 8, CONV_DIM), lambda i, j: (i, 0, 0)),
        pl.BlockSpec((None, CHUNK, GM_DIM), lambda i, j: (i, 0, 0)),
        tok(d), tok(d),
    ]
    out_shape = [
        jax.ShapeDtypeStruct((b, N_HEADS, t, HP), BF16),
        jax.ShapeDtypeStruct((b, N_HEADS, t, HP), BF16),
        jax.ShapeDtypeStruct((b, N_HEADS, t, HP), BF16),
        jax.ShapeDtypeStruct((b, t, KV_LORA), F32),
        jax.ShapeDtypeStruct((b, t, HP), F32),
        jax.ShapeDtypeStruct((b, 8, CONV_DIM), F32),
        jax.ShapeDtypeStruct((b, CHUNK, GM_DIM), F32),
        jax.ShapeDtypeStruct((b, t, d), BF16),
        jax.ShapeDtypeStruct((b, t, d), F32),
    ]
    return pl.pallas_call(
        functools.partial(_inproj_prompt_kernel, tm=tm),
        grid=(b, nt),
        in_specs=in_specs, out_specs=out_specs, out_shape=out_shape,
        scratch_shapes=[pltpu.VMEM((tm + 8, CONV_DIM), F32)],
        compiler_params=_cparams(("arbitrary", "arbitrary")),
        name="inproj_prompt",
    )(x, scale_m, shift_m, lw["g_norm_mix"], lw["w2"], lw["g_q_lat"], lw["wuq2"], lw["g_qk_q"],
      lw["g_kv_lat"], lw["wuk"], lw["g_qk_k"], lw["wuv"], cos, sin, lw["w_conv"],
      lw["w_br_b"], lw["g_v_ln"], lw["w_sp"], lw["b_sp"], lw["w_br_c"])


def _inproj_sample(x, scale_m, shift_m, lw, cos, sin, hist0, hist1):
    n, d = x.shape
    full = lambda *s: pl.BlockSpec(s, lambda i: (0,) * len(s))
    in_specs = [
        full(n, d), full(n, d), full(n, d), full(1, d), full(d, S_END),
        full(1, Q_LORA), full(Q_LORA, 2 * N_HEADS * HP), full(1, HP),
        full(1, KV_LORA), full(1, HP), full(N_HEADS, HP, KV_LORA), full(1, HP), full(1, HP),
        full(CONV_W, CONV_DIM), full(n, CONV_DIM), full(n, CONV_DIM),
        full(CONV_DIM, d), full(1, GM_DIM), full(1, GM_DIM), full(1, GM_DIM), full(GM_DIM, d),
    ]
    out_specs = [full(N_HEADS, n, HP), full(N_HEADS, n, KV_LORA), full(n, KV_LORA), full(n, HP),
                 full(n, CONV_DIM), full(n, GM_DIM), full(n, d), full(n, d)]
    out_shape = [
        jax.ShapeDtypeStruct((N_HEADS, n, HP), F32),
        jax.ShapeDtypeStruct((N_HEADS, n, KV_LORA), F32),
        jax.ShapeDtypeStruct((n, KV_LORA), F32),
        jax.ShapeDtypeStruct((n, HP), F32),
        jax.ShapeDtypeStruct((n, CONV_DIM), F32),
        jax.ShapeDtypeStruct((n, GM_DIM), F32),
        jax.ShapeDtypeStruct((n, d), BF16),
        jax.ShapeDtypeStruct((n, d), F32),
    ]
    return pl.pallas_call(
        _inproj_sample_kernel,
        grid=(1,),
        in_specs=in_specs, out_specs=out_specs, out_shape=out_shape,
        compiler_params=_cparams(("arbitrary",)),
        name="inproj_sample",
    )(x, scale_m, shift_m, lw["g_norm_mix"], lw["w2"], lw["g_q_lat"], lw["wuq2"], lw["g_qk_q"],
      lw["g_kv_lat"], lw["g_qk_k"], lw["wukt_pad"], cos, sin, lw["w_conv"], hist0, hist1,
      lw["w_br_b"], lw["g_v_ln"], lw["sp_coef"], lw["b_sp"][0:1], lw["w_br_c"])


def _flash_kernel(q_ref, k_ref, v_ref, o_ref, *, tq, tk):
    assert tq == tk
    qi = pl.program_id(2)
    q = q_ref[...]

    def step(j, carry, masked):
        m, acc = carry
        start = pl.multiple_of(j * tk, tk)
        k = k_ref[pl.ds(start, tk), :]
        v = v_ref[pl.ds(start, tk), :]
        s = lax.dot_general(q, k, (((1,), (1,)), ((), ())), preferred_element_type=F32)
        if masked:
            row = lax.broadcasted_iota(jnp.int32, (tq, tk), 0)
            col = lax.broadcasted_iota(jnp.int32, (tq, tk), 1)
            s = jnp.where(col <= row, s, -jnp.inf)
        m_new = jnp.maximum(m, jnp.max(s, axis=-1, keepdims=True))
        alpha = jnp.exp(m - m_new)
        p = jnp.exp(s - m_new)
        acc = alpha * acc + _dot(p.astype(BF16), v)
        return m_new, acc

    m0 = jnp.full((tq, 1), -jnp.inf, F32)
    a0 = jnp.zeros((tq, HP), F32)
    carry = lax.fori_loop(0, qi, lambda j, c: step(j, c, False), (m0, a0))
    _, acc = step(qi, carry, True)
    lane = lax.broadcasted_iota(jnp.int32, (tq, HP), 1)
    l = jnp.sum(jnp.where(lane == V_DIM, acc, 0.0), axis=-1, keepdims=True)
    o_ref[...] = (acc / l).astype(BF16)


def _flash_attention(q, k, v):
    b, h, t, _ = q.shape
    tq, tk = TQ, TK
    return pl.pallas_call(
        functools.partial(_flash_kernel, tq=tq, tk=tk),
        grid=(b, h, t // tq),
        in_specs=[pl.BlockSpec((None, None, tq, HP), lambda i, j, n: (i, j, n, 0)),
                  pl.BlockSpec((None, None, t, HP), lambda i, j, n: (i, j, 0, 0)),
                  pl.BlockSpec((None, None, t, HP), lambda i, j, n: (i, j, 0, 0))],
        out_specs=pl.BlockSpec((None, tq, HP), lambda i, j, n: (i, n, j)),
        out_shape=jax.ShapeDtypeStruct((b, t, h * HP), BF16),
        compiler_params=_cparams(("arbitrary", "arbitrary", "arbitrary")),
        name="flash_prompt",
    )(q, k, v)


def _paged_kernel(pt_ref, wkt_ref, ql_ref, qr_ref, cnew_ref, krnew_ref, poolc_ref, poolk_ref, o_ref,
                  cbuf, kbuf, csem, ksem, lhs_ref, m_ref, l_ref, acc_ref, *, layer, n_pages, n_pg, n_chunk):
    i = pl.program_id(0)
    j = pl.program_id(1)
    nj = pl.num_programs(1)
    nk = N_HEADS * QK_NOPE
    step = i * nj + j
    last_step = pl.num_programs(0) * nj - 1
    ahead = PAGE_BUFFERS - 1
    slot = step % PAGE_BUFFERS
    nxt_slot = (step + ahead) % PAGE_BUFFERS

    def step_base(t):
        t = jnp.minimum(t, last_step)
        return (t // nj) * n_pages + (t % nj) * n_pg

    nxt_base = step_base(step + ahead)

    def page_copies(base, buf_slot, p):
        pid = pt_ref[base + p]
        return (pltpu.make_async_copy(poolc_ref.at[layer, pid], cbuf.at[buf_slot, p], csem.at[buf_slot]),
                pltpu.make_async_copy(poolk_ref.at[layer, pid], kbuf.at[buf_slot, p], ksem.at[buf_slot]))

    def start_page(base, buf_slot, p):
        for cp in page_copies(base, buf_slot, p):
            cp.start()

    @pl.when(step == 0)
    def _():
        for a in range(ahead):
            for p in range(n_pg):
                start_page(step_base(a), a, p)

    for p in range(n_pg):
        for cp in page_copies(i * n_pages + j * n_pg, slot, p):
            cp.wait()

    @pl.when(j == 0)
    def _():
        lhs_ref[0:nk, :] = wkt_ref[...]
        lhs_ref[nk:nk + HEAD_ROWS, :] = ql_ref[...]
        m_ref[...] = jnp.full(m_ref.shape, -jnp.inf, F32)
        l_ref[...] = jnp.zeros(l_ref.shape, F32)
        acc_ref[...] = jnp.zeros(acc_ref.shape, F32)

    qr = qr_ref[...]

    def update(cb, krt, valid_keys):
        kx = lax.dot_general(lhs_ref[...], cb, (((1,), (1,)), ((), ())), preferred_element_type=F32)
        sq = kx[:nk, :] * kx[:nk, :]
        ssq = jnp.sum(sq.reshape(QK_NOPE, N_HEADS, sq.shape[-1]), axis=0)
        ssq = ssq + jnp.sum(krt * krt, axis=0, keepdims=True)
        inv = lax.rsqrt(ssq * (1.0 / QK_DIM) + EPS)
        st = (kx[nk:, :] + _dot(qr, krt.astype(BF16))) * jnp.concatenate([inv, inv], axis=0)
        if valid_keys is not None:
            kcol = lax.broadcasted_iota(jnp.int32, st.shape, 1)
            st = jnp.where(kcol < valid_keys, st, -jnp.inf)
        m_old = m_ref[...]
        m_new = jnp.maximum(m_old, jnp.max(st, axis=-1, keepdims=True))
        alpha = jnp.exp(m_old - m_new)
        p = jnp.exp(st - m_new)
        l_ref[...] = alpha * l_ref[...] + jnp.sum(p, axis=-1, keepdims=True)
        acc_ref[...] = alpha * acc_ref[...] + _dot(p.astype(BF16), cb)
        m_ref[...] = m_new

    per = n_pg // n_chunk
    for ch in range(n_chunk):
        cb = cbuf[slot, ch * per:(ch + 1) * per].reshape(per * PAGE_SIZE, KV_LORA).astype(BF16)
        krt = jnp.concatenate([kbuf[slot, p] for p in range(ch * per, (ch + 1) * per)], axis=1)
        update(cb, krt, None)
        for p in range(ch * per, (ch + 1) * per):
            start_page(nxt_base, nxt_slot, p)

    @pl.when(j == nj - 1)
    def _():
        first = lax.broadcasted_iota(jnp.int32, (LANES, KV_LORA), 0) == 0
        cn = jnp.where(first, jnp.broadcast_to(cnew_ref[...], (LANES, KV_LORA)), 0.0).astype(BF16)
        update(cn, krnew_ref[...], 1)
        o_ref[...] = (acc_ref[...] / l_ref[...])[:N_HEADS, :]

    @pl.when(step == last_step)
    def _():
        for a in range(1, PAGE_BUFFERS):
            for p in range(n_pg):
                for cp in page_copies(nxt_base, (step + a) % PAGE_BUFFERS, p):
                    cp.wait()


def _paged_attention(layer, page_table, wkt, qlat, qr, cnew, krnew_t, pool_c, pool_krt):
    s, n_pages = page_table.shape
    n_pg = min(PAGES_PER_STEP, n_pages)
    assert n_pages % n_pg == 0 and n_pg % PAGES_PER_CHUNK == 0
    nj = n_pages // n_pg
    pt = page_table.reshape(-1)
    nk = N_HEADS * QK_NOPE

    per_seq = lambda *shape: pl.BlockSpec((None,) + shape, lambda i, j, pt_ref: (i,) + (0,) * len(shape))
    const = lambda *shape: pl.BlockSpec(shape, lambda i, j, pt_ref: (0,) * len(shape))
    in_specs = [const(nk, KV_LORA), per_seq(HEAD_ROWS, KV_LORA), per_seq(HEAD_ROWS, QK_ROPE),
                per_seq(1, KV_LORA), per_seq(QK_ROPE, LANES),
                pl.BlockSpec(memory_space=pl.ANY), pl.BlockSpec(memory_space=pl.ANY)]
    grid_spec = pltpu.PrefetchScalarGridSpec(
        num_scalar_prefetch=1,
        grid=(s, nj),
        in_specs=in_specs,
        out_specs=per_seq(N_HEADS, KV_LORA),
        scratch_shapes=[pltpu.VMEM((PAGE_BUFFERS, n_pg, PAGE_SIZE, KV_LORA), F32),
                        pltpu.VMEM((PAGE_BUFFERS, n_pg, QK_ROPE, PAGE_SIZE), F32),
                        pltpu.SemaphoreType.DMA((PAGE_BUFFERS,)), pltpu.SemaphoreType.DMA((PAGE_BUFFERS,)),
                        pltpu.VMEM((nk + HEAD_ROWS, KV_LORA), BF16), pltpu.VMEM((HEAD_ROWS, 1), F32),
                        pltpu.VMEM((HEAD_ROWS, 1), F32), pltpu.VMEM((HEAD_ROWS, KV_LORA), F32)],
    )
    return pl.pallas_call(
        functools.partial(_paged_kernel, layer=layer, n_pages=n_pages, n_pg=n_pg, n_chunk=n_pg // PAGES_PER_CHUNK),
        grid_spec=grid_spec,
        out_shape=jax.ShapeDtypeStruct((s, N_HEADS, KV_LORA), F32),
        compiler_params=_cparams(("arbitrary", "arbitrary")),
        name="paged_sample",
    )(pt, wkt, qlat, qr, cnew, krnew_t, pool_c, pool_krt)


def _uv_kernel(ol_ref, wuv_ref, o_ref):
    for hh in range(N_HEADS):
        ol = ol_ref[:, hh * KV_LORA:(hh + 1) * KV_LORA].astype(BF16)
        o_ref[:, hh * HP:(hh + 1) * HP] = _dot(ol, wuv_ref[:, hh * HP:(hh + 1) * HP]).astype(BF16)


def _uv_project(olat, wuv):
    n = olat.shape[0]
    full = lambda *s: pl.BlockSpec(s, lambda i: (0,) * len(s))
    return pl.pallas_call(
        _uv_kernel, grid=(1,),
        in_specs=[full(n, N_HEADS * KV_LORA), full(KV_LORA, N_HEADS * HP)],
        out_specs=full(n, N_HEADS * HP),
        out_shape=jax.ShapeDtypeStruct((n, N_HEADS * HP), BF16),
        compiler_params=_cparams(("arbitrary",)),
        name="uv_sample",
    )(olat, wuv)


def _merge_kernel(oa_ref, wbra_ref, bg0_ref, part_ref, x_ref, gm_ref, wout_ref, gffn_ref, scf_ref, shf_ref,
                  x1_ref, h2_ref):
    bra = _dot(oa_ref[...], wbra_ref[...])
    merged = bg0_ref[...].astype(F32) * bra + part_ref[...]
    x1 = x_ref[...] + gm_ref[...] * _dot(merged.astype(BF16), wout_ref[...])
    x1_ref[...] = x1
    h2_ref[...] = (_rms_rows(x1) * gffn_ref[...] * (1.0 + scf_ref[...]) + shf_ref[...]).astype(BF16)


def _merge(oa, bg0, part, x, gate_m, scale_f, shift_f, lw, tm):
    g, t, d = x.shape
    mt = gate_m.shape[1]
    tok = lambda w: pl.BlockSpec((None, tm, w), lambda i, j: (i, j, 0))
    if mt == 1:
        mod = pl.BlockSpec((None, 1, d), lambda i, j: (i, 0, 0))
    else:
        mod = pl.BlockSpec((None, tm, d), lambda i, j: (i, j, 0))
    return pl.pallas_call(
        _merge_kernel,
        grid=(g, t // tm),
        in_specs=[tok(N_HEADS * HP), _const_spec((N_HEADS * HP, d)), tok(d), tok(d), tok(d), mod,
                  _const_spec((d, d)), _const_spec((1, d)), mod, mod],
        out_specs=[tok(d), tok(d)],
        out_shape=[jax.ShapeDtypeStruct((g, t, d), F32), jax.ShapeDtypeStruct((g, t, d), BF16)],
        compiler_params=_cparams(("arbitrary", "arbitrary")),
        name="merge",
    )(oa, lw["w_br_a"], bg0, part, x, gate_m, lw["w_out"], lw["g_norm_ffn"], scale_f, shift_f)


def _router_kernel(h_ref, wrt_ref, brt_ref, g_ref, idx_ref, w_ref):
    tm = h_ref.shape[0]
    logits = lax.dot_general(wrt_ref[...], h_ref[...], (((1,), (1,)), ((), ())), preferred_element_type=F32)
    scores = jax.nn.sigmoid(logits)
    choice = scores + brt_ref[...]
    neg = -jnp.inf

    gsc = []
    for g in range(N_EXPERT_GROUPS):
        xg = choice[g * GROUP_SIZE:(g + 1) * GROUP_SIZE, :]
        m1 = jnp.max(xg, axis=0, keepdims=True)
        cnt = jnp.sum(jnp.where(xg == m1, 1.0, 0.0), axis=0, keepdims=True)
        m2 = jnp.max(jnp.where(xg < m1, xg, neg), axis=0, keepdims=True)
        gsc.append(m1 + jnp.where(cnt >= 2.0, m1, m2))

    cands = []
    for g in range(N_EXPERT_GROUPS):
        rank = jnp.zeros((1, tm), F32)
        for o in range(N_EXPERT_GROUPS):
            if o == g:
                continue
            beats = (gsc[o] > gsc[g]) if o > g else (gsc[o] >= gsc[g])
            rank = rank + jnp.where(beats, 1.0, 0.0)
        keep = rank < float(TOPK_GROUPS)
        cands.append(jnp.where(keep, choice[g * GROUP_SIZE:(g + 1) * GROUP_SIZE, :], neg))
    cand = jnp.concatenate(cands, axis=0)

    ridx = lax.broadcasted_iota(jnp.int32, (N_EXPERTS, tm), 0)
    picked = jnp.zeros((N_EXPERTS, tm), F32)
    idxs, ws = [], []
    for _ in range(TOP_K):
        m = jnp.max(cand, axis=0, keepdims=True)
        idx = jnp.min(jnp.where(cand == m, ridx, N_EXPERTS), axis=0, keepdims=True)
        hit = ridx == idx
        picked = jnp.where(hit, 1.0, picked)
        cand = jnp.where(hit, neg, cand)
        idxs.append(idx)
        ws.append(jnp.sum(jnp.where(hit, scores, 0.0), axis=0, keepdims=True))

    w = jnp.where(picked > 0.0, scores, 0.0)
    norm = ROUTED_SCALE / jnp.sum(w, axis=0, keepdims=True)
    gt = w * norm
    g_ref[...] = jnp.concatenate([gt, jnp.zeros((LANES - N_EXPERTS, tm), F32)], axis=0).T
    idx_ref[...] = jnp.concatenate(idxs, axis=0)
    w_ref[...] = jnp.concatenate(ws, axis=0) * norm


def _router(h2, lw, tm):
    m, d = h2.shape
    return pl.pallas_call(
        _router_kernel,
        grid=(m // tm,),
        in_specs=[pl.BlockSpec((tm, d), lambda i: (i, 0)), _const_spec((N_EXPERTS, d)),
                  _const_spec((N_EXPERTS, 1))],
        out_specs=[pl.BlockSpec((tm, LANES), lambda i: (i, 0)), pl.BlockSpec((TOP_K, tm), lambda i: (0, i)),
                   pl.BlockSpec((TOP_K, tm), lambda i: (0, i))],
        out_shape=[jax.ShapeDtypeStruct((m, LANES), F32), jax.ShapeDtypeStruct((TOP_K, m), jnp.int32),
                   jax.ShapeDtypeStruct((TOP_K, m), F32)],
        compiler_params=_cparams(("arbitrary",)),
        name="router",
    )(h2, lw["w_router_t"], lw["b_router_t"])


def _swiglu(x, wgu):
    gu = _dot(x, wgu)
    return _silu(gu[:, :D_EXPERT]) * gu[:, D_EXPERT:]


def _expert_ffn(x, wg, wu, wd):
    h = _silu(_dot(x, wg.astype(BF16))) * _dot(x, wu.astype(BF16))
    return h, wd.astype(BF16)


def _moe_kernel(h_ref, g_ref, wg_ref, wu_ref, wd_ref, wsgu_ref, wsd_ref, x1_ref, gf_ref, o_ref, acc_ref, *, epb):
    e = pl.program_id(2)
    h = h_ref[...]
    tm = h.shape[0]

    @pl.when(e == 0)
    def _():
        acc_ref[...] = _dot(_swiglu(h, wsgu_ref[...]).astype(BF16), wsd_ref[...])

    lane = lax.broadcasted_iota(jnp.int32, (tm, LANES), 1)
    gates = g_ref[...]
    for jj in range(epb):
        gcol = jnp.sum(jnp.where(lane == e * epb + jj, gates, 0.0), axis=-1, keepdims=True)
        hh, wd = _expert_ffn(h, wg_ref[jj], wu_ref[jj], wd_ref[jj])
        acc_ref[...] += _dot((hh * gcol).astype(BF16), wd)

    @pl.when(e == pl.num_programs(2) - 1)
    def _():
        o_ref[...] = x1_ref[...] + gf_ref[...] * acc_ref[...]


def _moe(h2, gates, x1, gate_f, lw, tm):
    g, t, d = x1.shape
    mt = gate_f.shape[1]
    epb = EXPERTS_PER_STEP
    layer = lw["layer"]
    tok = lambda w: pl.BlockSpec((None, tm, w), lambda i, j, e: (i, j, 0))
    if mt == 1:
        mod = pl.BlockSpec((None, 1, d), lambda i, j, e: (i, 0, 0))
    else:
        mod = pl.BlockSpec((None, tm, d), lambda i, j, e: (i, j, 0))
    return pl.pallas_call(
        functools.partial(_moe_kernel, epb=epb),
        grid=(g, t // tm, N_EXPERTS // epb),
        in_specs=[tok(d), tok(LANES),
                  pl.BlockSpec((None, epb, d, D_EXPERT), lambda i, j, e: (layer, e, 0, 0)),
                  pl.BlockSpec((None, epb, d, D_EXPERT), lambda i, j, e: (layer, e, 0, 0)),
                  pl.BlockSpec((None, epb, D_EXPERT, d), lambda i, j, e: (layer, e, 0, 0)),
                  _const_spec((d, 2 * D_EXPERT)), _const_spec((D_EXPERT, d)),
                  tok(d), mod],
        out_specs=tok(d),
        out_shape=jax.ShapeDtypeStruct((g, t, d), F32),
        scratch_shapes=[pltpu.VMEM((tm, d), F32)],
        compiler_params=_cparams(("arbitrary", "arbitrary", "arbitrary")),
        name="moe",
    )(h2, gates, lw["w_e_gate"], lw["w_e_up"], lw["w_e_down"], lw["w_sh_gu"], lw["w_sh_down"], x1, gate_f)


def _dispatch_kernel(x_ref, idx_ref, xs_ref, posk_ref, nch_ref, off_ref, *, rc):
    x = x_ref[...]
    idx = idx_ref[...]
    nb = x.shape[0]

    eiota = lax.broadcasted_iota(jnp.int32, (N_EXPERTS, nb), 0)
    hits = [eiota == idx[k:k + 1, :] for k in range(TOP_K)]
    sel = jnp.zeros((N_EXPERTS, nb), F32)
    for hit in hits:
        sel = sel + jnp.where(hit, 1.0, 0.0)
    cnt = jnp.sum(sel, axis=1, keepdims=True)
    nch = jnp.floor((cnt + (MOE_CHUNK - 1.0)) * (1.0 / MOE_CHUNK))
    nch_b = jnp.broadcast_to(nch, (N_EXPERTS, LANES))
    earlier = (lax.broadcasted_iota(jnp.int32, (N_EXPERTS, N_EXPERTS), 1)
               < lax.broadcasted_iota(jnp.int32, (N_EXPERTS, N_EXPERTS), 0))
    off_b = _dot(jnp.where(earlier, 1.0, 0.0).astype(BF16), nch_b.astype(BF16))
    before = (lax.broadcasted_iota(jnp.int32, (nb, nb), 0) < lax.broadcasted_iota(jnp.int32, (nb, nb), 1))
    rank = _dot(sel.astype(BF16), jnp.where(before, 1.0, 0.0).astype(BF16))
    pos = off_b[:, 0:1] * float(MOE_CHUNK) + rank
    posk = jnp.concatenate([jnp.sum(jnp.where(hit, pos, 0.0), axis=0, keepdims=True) for hit in hits],
                           axis=0).astype(jnp.int32)
    posk_ref[...] = posk
    nch_ref[...] = nch_b.astype(jnp.int32)
    off_ref[...] = off_b.astype(jnp.int32)

    posk16 = posk.astype(jnp.int16)
    for c in range(MOE_ROWS // rc):
        riota = (c * rc + lax.broadcasted_iota(jnp.int32, (rc, nb), 0)).astype(jnp.int16)
        p = jnp.zeros((rc, nb), BF16)
        for k in range(TOP_K):
            p = jnp.where(riota == posk16[k:k + 1, :], jnp.ones((), BF16), p)
        xs_ref[c * rc:(c + 1) * rc, :] = _dot(p, x).astype(BF16)


def _dispatch(h2, idx_t):
    nblk, nb, d = h2.shape
    small = pl.BlockSpec((None, N_EXPERTS, LANES), lambda i: (i, 0, 0))
    return pl.pallas_call(
        functools.partial(_dispatch_kernel, rc=512),
        grid=(nblk,),
        in_specs=[pl.BlockSpec((None, nb, d), lambda i: (i, 0, 0)),
                  pl.BlockSpec((TOP_K, nb), lambda i: (0, i))],
        out_specs=[pl.BlockSpec((None, MOE_ROWS, d), lambda i: (i, 0, 0)),
                   pl.BlockSpec((None, TOP_K, nb), lambda i: (i, 0, 0)), small, small],
        out_shape=[jax.ShapeDtypeStruct((nblk, MOE_ROWS, d), BF16),
                   jax.ShapeDtypeStruct((nblk, TOP_K, nb), jnp.int32),
                   jax.ShapeDtypeStruct((nblk, N_EXPERTS, LANES), jnp.int32),
                   jax.ShapeDtypeStruct((nblk, N_EXPERTS, LANES), jnp.int32)],
        compiler_params=_cparams(("arbitrary",)),
        name="moe_dispatch",
    )(h2, idx_t)


def _start_all(copies):
    for cp in copies:
        cp.start()


def _expert_kernel(te_ref, src_ref, nact_ref, xs_ref, wg_ref, wu_ref, wd_ref, ys_ref, xbuf, sem):
    t = pl.program_id(0)
    nact = nact_ref[0]
    ahead = EXP_BUFFERS - 1
    slot = t % EXP_BUFFERS

    def copies(tile, buf_slot):
        return [pltpu.make_async_copy(xs_ref.at[src_ref[tile * EXP_CHUNKS + c]], xbuf.at[buf_slot, c],
                                      sem.at[buf_slot]) for c in range(EXP_CHUNKS)]

    @pl.when(t == 0)
    def _():
        for a in range(ahead):
            @pl.when(a < nact)
            def _():
                _start_all(copies(a, a))

    @pl.when(t + ahead < nact)
    def _():
        _start_all(copies(t + ahead, (t + ahead) % EXP_BUFFERS))

    @pl.when(t < nact)
    def _():
        for cp in copies(t, slot):
            cp.wait()
        x = xbuf[slot].reshape(EXP_CHUNKS * MOE_CHUNK, xbuf.shape[-1])
        hh, wd = _expert_ffn(x, wg_ref[...], wu_ref[...], wd_ref[...])
        ys_ref[...] = _dot(hh.astype(BF16), wd).astype(BF16)

    @pl.when(t >= nact)
    def _():
        ys_ref[...] = jnp.zeros(ys_ref.shape, BF16)


def _experts(xs_chunks, tile_expert, src_chunk, n_active, lw):
    d = xs_chunks.shape[-1]
    n_tiles = tile_expert.shape[0]
    tm = EXP_CHUNKS * MOE_CHUNK
    layer = lw["layer"]
    grid_spec = pltpu.PrefetchScalarGridSpec(
        num_scalar_prefetch=3,
        grid=(n_tiles,),
        in_specs=[pl.BlockSpec(memory_space=pl.ANY),
                  pl.BlockSpec((None, None, d, D_EXPERT), lambda t, te, src, na: (layer, te[t], 0, 0)),
                  pl.BlockSpec((None, None, d, D_EXPERT), lambda t, te, src, na: (layer, te[t], 0, 0)),
                  pl.BlockSpec((None, None, D_EXPERT, d), lambda t, te, src, na: (layer, te[t], 0, 0))],
        out_specs=pl.BlockSpec((tm, d), lambda t, te, src, na: (t, 0)),
        scratch_shapes=[pltpu.VMEM((EXP_BUFFERS, EXP_CHUNKS, MOE_CHUNK, d), BF16),
                        pltpu.SemaphoreType.DMA((EXP_BUFFERS,))],
    )
    return pl.pallas_call(
        _expert_kernel,
        grid_spec=grid_spec,
        out_shape=jax.ShapeDtypeStruct((n_tiles * tm, d), BF16),
        compiler_params=_cparams(("arbitrary",)),
        name="moe_experts",
    )(tile_expert, src_chunk, n_active, xs_chunks, lw["w_e_gate"], lw["w_e_up"], lw["w_e_down"])


def _combine_kernel(src_ref, ys_ref, pos_ref, w_ref, h_ref, x1_ref, gf_ref, wsgu_ref, wsd_ref,
                    o_ref, ybuf, sem, *, cc):
    b = pl.program_id(0)
    last = pl.num_programs(0) - 1
    ahead = COMBINE_BUFFERS - 1
    slot = b % COMBINE_BUFFERS
    nxt_slot = (b + ahead) % COMBINE_BUFFERS
    nxt = jnp.minimum(b + ahead, last)
    nch = MOE_ROWS // MOE_CHUNK

    def copies(blk, buf_slot, lo=0, hi=nch):
        return [pltpu.make_async_copy(ys_ref.at[src_ref[blk * nch + c]], ybuf.at[buf_slot, c], sem.at[buf_slot])
                for c in range(lo, hi)]

    @pl.when(b == 0)
    def _():
        for a in range(ahead):
            _start_all(copies(jnp.minimum(a, last), a))

    acc = _dot(_swiglu(h_ref[...], wsgu_ref[...]).astype(BF16), wsd_ref[...])

    for cp in copies(b, slot):
        cp.wait()
    pos = pos_ref[...].astype(jnp.int16)
    w = w_ref[...].astype(BF16)
    nb = pos.shape[0]
    per = cc // MOE_CHUNK
    for c in range(MOE_ROWS // cc):
        liota = (c * cc + lax.broadcasted_iota(jnp.int32, (nb, cc), 1)).astype(jnp.int16)
        pw = jnp.zeros((nb, cc), BF16)
        for k in range(TOP_K):
            pw = jnp.where(liota == pos[:, k:k + 1], w[:, k:k + 1], pw)
        y = ybuf[slot, c * per:(c + 1) * per].reshape(cc, ybuf.shape[-1])
        acc = acc + _dot(pw, y)
        _start_all(copies(nxt, nxt_slot, c * per, (c + 1) * per))
    o_ref[...] = x1_ref[...] + gf_ref[...] * acc

    @pl.when(b == last)
    def _():
        for a in range(1, COMBINE_BUFFERS):
            for cp in copies(nxt, (b + a) % COMBINE_BUFFERS):
                cp.wait()


def _combine(ys_chunks, src_chunk, posk, wk, h2, x1, gate_f, lw, blocks_per_seq):
    nblk, nb, d = h2.shape
    nch = MOE_ROWS // MOE_CHUNK
    blk = lambda w: pl.BlockSpec((None, nb, w), lambda i, src: (i, 0, 0))
    const = lambda *s: pl.BlockSpec(s, lambda i, src: (0,) * len(s))
    grid_spec = pltpu.PrefetchScalarGridSpec(
        num_scalar_prefetch=1,
        grid=(nblk,),
        in_specs=[pl.BlockSpec(memory_space=pl.ANY), blk(TOP_K), blk(TOP_K), blk(d), blk(d),
                  pl.BlockSpec((None, 1, d), lambda i, src: (i // blocks_per_seq, 0, 0)),
                  const(d, 2 * D_EXPERT), const(D_EXPERT, d)],
        out_specs=blk(d),
        scratch_shapes=[pltpu.VMEM((COMBINE_BUFFERS, nch, MOE_CHUNK, d), BF16),
                        pltpu.SemaphoreType.DMA((COMBINE_BUFFERS,))],
    )
    return pl.pallas_call(
        functools.partial(_combine_kernel, cc=512),
        grid_spec=grid_spec,
        out_shape=jax.ShapeDtypeStruct((nblk, nb, d), F32),
        compiler_params=_cparams(("arbitrary",)),
        name="moe_combine",
    )(src_chunk, ys_chunks, posk, wk, h2, x1, gate_f, lw["w_sh_gu"], lw["w_sh_down"])


def _excl_cumsum(x, axis):
    n = x.shape[axis]
    earlier = jnp.arange(n)[:, None] > jnp.arange(n)[None, :]
    xm = jnp.moveaxis(x, axis, -1)
    out = jnp.sum(jnp.where(earlier, xm[..., None, :], 0), axis=-1)
    return jnp.moveaxis(out, -1, axis)


def _pick(onehot, table):
    return jnp.sum(jnp.where(onehot, table, 0), axis=-1)


def _moe_plan(nch, off, n_tokens):
    nblk = nch.shape[0]
    i32 = jnp.int32
    cpb = MOE_ROWS // MOE_CHUNK
    experts = jnp.arange(N_EXPERTS, dtype=i32)
    tot = jnp.sum(nch, axis=1)
    ech = jnp.sum(nch, axis=0)
    tiles_e = (ech + EXP_CHUNKS - 1) // EXP_CHUNKS
    tile_start = _excl_cumsum(tiles_e, 0)
    tile_end = tile_start + tiles_e
    n_active = jnp.sum(tiles_e)
    cumb = _excl_cumsum(nch, 0)

    n_tiles = _max_expert_tiles(n_tokens, nblk)
    tid = jnp.arange(n_tiles, dtype=i32)
    te = jnp.sum((tid[:, None] >= tile_end[None, :]).astype(i32), axis=1)
    te_last = jnp.sum((n_active - 1 >= tile_end).astype(i32))
    te = jnp.where(tid < n_active, te, te_last)
    is_e = te[:, None] == experts[None, :]
    ts_t = _pick(is_e, tile_start[None, :])
    ech_t = _pick(is_e, ech[None, :])
    ends_t = _pick(is_e[:, None, :], (cumb + nch)[None, :, :])
    off_t = _pick(is_e[:, None, :], off[None, :, :])
    cumb_t = _pick(is_e[:, None, :], cumb[None, :, :])
    c = (tid - ts_t)[:, None] * EXP_CHUNKS + jnp.arange(EXP_CHUNKS, dtype=i32)[None, :]
    valid = jnp.logical_and(c < ech_t[:, None], (tid < n_active)[:, None])
    sb = jnp.minimum(jnp.sum((c[:, :, None] >= ends_t[:, None, :]).astype(i32), axis=2), nblk - 1)
    is_b = sb[:, :, None] == jnp.arange(nblk, dtype=i32)[None, None, :]
    src_e = jnp.where(valid, sb * cpb + _pick(is_b, off_t[:, None, :]) + c - _pick(is_b, cumb_t[:, None, :]), 0)

    j = jnp.arange(cpb, dtype=i32)
    ce = jnp.minimum(jnp.sum((j[None, :, None] >= (off + nch)[:, None, :]).astype(i32), axis=2), N_EXPERTS - 1)
    is_ce = ce[:, :, None] == experts[None, None, :]
    g = (_pick(is_ce, tile_start[None, None, :]) * EXP_CHUNKS + _pick(is_ce, cumb[:, None, :])
         + j[None, :] - _pick(is_ce, off[:, None, :]))
    src_c = jnp.where(j[None, :] < tot[:, None], g, 0)
    return te, src_e.reshape(-1), n_active.reshape(1), src_c.reshape(-1)


def _max_expert_tiles(t, nblk):
    max_chunks = t * TOP_K // MOE_CHUNK + nblk * N_EXPERTS
    return max_chunks // EXP_CHUNKS + N_EXPERTS


def _moe_sparse(h2, idx_t, w_t, x1, gate_f, lw):
    b, t, d = x1.shape
    nblk = b * t // MOE_BLOCK
    h2b = h2.reshape(nblk, MOE_BLOCK, d)
    xs, posk_t, nch, off = _dispatch(h2b, idx_t)
    te, src_e, n_active, src_c = _moe_plan(nch[:, :, 0], off[:, :, 0], b * t)
    ys = _experts(xs.reshape(nblk * MOE_ROWS // MOE_CHUNK, MOE_CHUNK, d), te, src_e, n_active, lw)
    wk = w_t.reshape(TOP_K, nblk, MOE_BLOCK).transpose(1, 2, 0)
    out = _combine(ys.reshape(-1, MOE_CHUNK, d), src_c, posk_t.transpose(0, 2, 1), wk,
                   h2b, x1.reshape(nblk, MOE_BLOCK, d), gate_f, lw, t // MOE_BLOCK)
    return out.reshape(b, t, d)


def _pad_heads(w, width):
    pad = [(0, 0)] * (w.ndim - 1) + [(0, HP - width)]
    wp = jnp.pad(w, pad)
    return wp.reshape(w.shape[:-2] + (w.shape[-2] * HP,))


def _rot_cols(w):
    half = QK_ROPE // 2
    return jnp.concatenate([-w[..., half:], w[..., :half]], axis=-1)


def _prep_layer(l, p):
    d = D_MODEL
    w_in = p["w_in"][l]
    off_kv = Q_LORA
    off_kr = off_kv + KV_LORA
    off_cb = off_kr + QK_ROPE
    off_g = off_cb + 3 * CONV_DIM + 2 * GM_DIM
    wkr = w_in[:, off_kr:off_cb]
    lead = jnp.zeros((d, QK_NOPE), F32)
    trail = jnp.zeros((d, HP - QK_DIM), F32)
    w2 = jnp.concatenate([
        w_in[:, :off_kr],
        lead, wkr, trail,
        lead, _rot_cols(wkr), trail,
        w_in[:, off_cb:],
    ], axis=1).astype(BF16)
    assert w2.shape[1] == S_END and off_g + N_BRANCH * d == w_in.shape[1]

    wuq = p["w_uq"][l]
    wuq_rot = jnp.concatenate([jnp.zeros_like(wuq[..., :QK_NOPE]), _rot_cols(wuq[..., QK_NOPE:])], axis=-1)
    wuq2 = jnp.concatenate([_pad_heads(wuq, QK_DIM), _pad_heads(wuq_rot, QK_DIM)], axis=-1).astype(BF16)

    def pad_gain(gq):
        return jnp.pad(gq, (0, HP - QK_DIM)).reshape(1, HP)

    w_br_a = jnp.pad(p["w_br_a"][l].reshape(N_HEADS, V_DIM, d), ((0, 0), (0, HP - V_DIM), (0, 0)))
    w_sp = p["w_spatial"][l]
    b_sp = jnp.repeat(p["b_spatial"][l].T, GM_GROUP_DIM, axis=1)
    return dict(
        g_norm_mix=p["g_norm_mix"][l].reshape(1, d), g_norm_ffn=p["g_norm_ffn"][l].reshape(1, d),
        w2=w2, g_q_lat=p["g_q_lat"][l].reshape(1, Q_LORA), wuq2=wuq2,
        g_qk_q=pad_gain(p["g_qk_q"][l]), g_qk_k=pad_gain(p["g_qk_k"][l]),
        g_kv_lat=p["g_kv_lat"][l].reshape(1, KV_LORA),
        wuk=_pad_heads(p["w_uk"][l], QK_NOPE).astype(BF16),
        wkt=p["w_uk"][l].transpose(2, 1, 0).reshape(QK_NOPE * N_HEADS, KV_LORA).astype(BF16),
        wukt_pad=jnp.pad(p["w_uk"][l].transpose(1, 2, 0), ((0, 0), (0, HP - QK_NOPE), (0, 0))).astype(BF16),
        wuv=_pad_heads(p["w_uv"][l], V_DIM).astype(BF16),
        w_br_a=w_br_a.reshape(N_HEADS * HP, d).astype(BF16),
        w_conv=p["w_conv"][l], w_br_b=p["w_br_b"][l].astype(BF16),
        g_v_ln=p["g_v_ln"][l].reshape(1, GM_DIM),
        w_sp=w_sp.reshape(GM_GROUPS * CHUNK, CHUNK), b_sp=b_sp,
        sp_coef=jnp.repeat(w_sp[:, 0, 0], GM_GROUP_DIM).reshape(1, GM_DIM),
        w_br_c=p["w_br_c"][l].astype(BF16), w_out=p["w_out"][l].astype(BF16),
        w_router_t=p["w_router"][l].T.astype(BF16),
        b_router_t=p["b_router"][l].reshape(N_EXPERTS, 1),
        w_sh_gu=jnp.concatenate([p["w_sh_gate"][l], p["w_sh_up"][l]], axis=-1).astype(BF16),
        w_sh_down=p["w_sh_down"][l].astype(BF16),
    )


def _rope_tables(pos):
    inv_freq = ROPE_THETA ** (-jnp.arange(0, QK_ROPE, 2, dtype=F32) / QK_ROPE)
    ang = pos.astype(F32)[:, None] * inv_freq[None, :]
    c, s = jnp.cos(ang), jnp.sin(ang)
    n = pos.shape[0]
    cos = jnp.concatenate([jnp.ones((n, QK_NOPE), F32), c, c, jnp.ones((n, HP - QK_DIM), F32)], axis=1)
    sin = jnp.concatenate([jnp.zeros((n, QK_NOPE), F32), s, s, jnp.zeros((n, HP - QK_DIM), F32)], axis=1)
    return cos, sin


def _split_mod(m):
    return [m[:, i * D_MODEL:(i + 1) * D_MODEL] for i in range(6)]


def _prompt_layer(x, mod, lw, cos, sin):
    b, t, d = x.shape
    shift_m, scale_m, gate_m, shift_f, scale_f, gate_f = [a.reshape(b, 1, d) for a in _split_mod(mod)]
    q, k, v, ckv, krp, tail, cv, bg0, part = _inproj_prompt(x, scale_m, shift_m, lw, cos, sin)
    oa = _flash_attention(q, k, v)
    x1, h2 = _merge(oa, bg0, part, x, gate_m, scale_f, shift_f, lw, TM_MERGE)
    _, idx_t, w_t = _router(h2.reshape(b * t, d), lw, TM_ROUTE)
    y = _moe_sparse(h2, idx_t, w_t, x1, gate_f, lw)
    state = (ckv, krp[:, :, QK_NOPE:QK_DIM], tail[:, 8 - (CONV_W - 1):], cv)
    return y, state


def _sample_layer(layer, x, mod, lw, cos, sin, hist, pool_c, pool_krt, page_table):
    n, d = x.shape
    shift_m, scale_m, gate_m, shift_f, scale_f, gate_f = _split_mod(mod)
    qg, ql, ckv, krp, z, cv, bg0, part = _inproj_sample(x, scale_m, shift_m, lw, cos, sin, hist[:, 0], hist[:, 1])
    head_pad = ((0, 0), (0, HEAD_ROWS - N_HEADS), (0, 0))
    qlat = jnp.pad(ql.transpose(1, 0, 2), head_pad).astype(BF16)
    qr = jnp.pad(qg[:, :, QK_NOPE:QK_DIM].transpose(1, 0, 2), head_pad).astype(BF16)
    kr = krp[:, QK_NOPE:QK_DIM]
    krnew_t = jnp.pad(kr[:, :, None], ((0, 0), (0, 0), (0, LANES - 1)))
    olat = _paged_attention(layer, page_table, lw["wkt"], qlat, qr, ckv.reshape(n, 1, KV_LORA), krnew_t,
                            pool_c, pool_krt)
    oa = _uv_project(olat.reshape(n, N_HEADS * KV_LORA), lw["wuv"])
    g3 = lambda a: a.reshape(1, n, -1)
    x1, h2 = _merge(g3(oa), g3(bg0), g3(part), g3(x), g3(gate_m), g3(scale_f), g3(shift_f), lw, n)
    gates, _, _ = _router(h2.reshape(n, d), lw, n)
    y = _moe(h2, g3(gates), x1, g3(gate_f), lw, n).reshape(n, d)
    state = (ckv.reshape(n, 1, KV_LORA), kr.reshape(n, 1, QK_ROPE),
             jnp.stack([hist[:, 1], z], axis=1), cv.reshape(n, 1, GM_DIM))
    return y, state


def kernel(x_prompt, x_sample, cache_kv_latent, cache_k_rope, state_conv, page_table, c_prompt, c_sample,
           w_ada, b_ada, g_norm_mix, g_norm_ffn, w_in, g_q_lat, w_uq, g_kv_lat, w_uk, w_uv, g_qk_q, g_qk_k,
           w_br_a, w_conv, w_br_b, g_v_ln, w_spatial, b_spatial, w_br_c, w_out, w_router, b_router,
           w_e_gate, w_e_up, w_e_down, w_sh_gate, w_sh_up, w_sh_down):
    params = dict(w_ada=w_ada, b_ada=b_ada, g_norm_mix=g_norm_mix, g_norm_ffn=g_norm_ffn, w_in=w_in,
                  g_q_lat=g_q_lat, w_uq=w_uq, g_kv_lat=g_kv_lat, w_uk=w_uk, w_uv=w_uv, g_qk_q=g_qk_q,
                  g_qk_k=g_qk_k, w_br_a=w_br_a, w_conv=w_conv, w_br_b=w_br_b, g_v_ln=g_v_ln,
                  w_spatial=w_spatial, b_spatial=b_spatial, w_br_c=w_br_c, w_out=w_out, w_router=w_router,
                  b_router=b_router, w_e_gate=w_e_gate, w_e_up=w_e_up, w_e_down=w_e_down,
                  w_sh_gate=w_sh_gate, w_sh_up=w_sh_up, w_sh_down=w_sh_down)
    depth = w_in.shape[0]
    nb, t, d = x_prompt.shape
    ns = x_sample.shape[0]
    assert x_sample.shape[1] == 1 and t % TQ == 0 and t % MOE_BLOCK == 0
    past_len = page_table.shape[1] * PAGE_SIZE

    cos_p, sin_p = _rope_tables(jnp.arange(t))
    cos_s, sin_s = _rope_tables(past_len + jnp.arange(1))
    c_all = jnp.concatenate([c_prompt, c_sample], axis=0)
    cache_krt = jnp.swapaxes(cache_k_rope, 2, 3)
    b_ada3 = b_ada.reshape(depth, 1, -1)

    hp, hs = x_prompt, x_sample.reshape(ns, d)
    st_p, st_s = [], []
    for l in range(depth):
        lw = _prep_layer(l, params)
        lw.update(layer=l, w_e_gate=w_e_gate, w_e_up=w_e_up, w_e_down=w_e_down)
        mod = _adaln(l, c_all, w_ada, b_ada3)
        hp, sp = _prompt_layer(hp, mod[:nb], lw, cos_p, sin_p)
        hs, ss = _sample_layer(l, hs, mod[nb:], lw, cos_s, sin_s, state_conv[l], cache_kv_latent, cache_krt,
                               page_table)
        st_p.append(sp)
        st_s.append(ss)

    stack = lambda sts, i: jnp.stack([s[i] for s in sts])
    return (hp, hs.reshape(ns, 1, d),
            stack(st_p, 0), stack(st_p, 1), stack(st_p, 2), stack(st_p, 3),
            stack(st_s, 0), stack(st_s, 1), stack(st_s, 2), stack(st_s, 3))
```

```python
import functools

import jax
import jax.numpy as jnp
from jax import lax
from jax.experimental import pallas as pl
from jax.experimental.pallas import tpu as pltpu

F32 = jnp.float32
BF16 = jnp.bfloat16

D_MODEL = 1024
N_HEADS = 8
QK_NOPE = 64
QK_ROPE = 32
QK_DIM = QK_NOPE + QK_ROPE
V_DIM = 64
Q_LORA = 384
KV_LORA = 256
ROPE_THETA = 10000.0
CONV_W = 3
CONV_DIM = 512
CHUNK = 128
GM_GROUPS = 8
GM_DIM = 512
GM_GROUP_DIM = GM_DIM // GM_GROUPS
N_EXPERTS = 64
TOP_K = 8
N_EXPERT_GROUPS = 8
GROUP_SIZE = N_EXPERTS // N_EXPERT_GROUPS
TOPK_GROUPS = 4
D_EXPERT = 256
ROUTED_SCALE = 2.5
N_BRANCH = 3
EPS = 1e-6
PAGE_SIZE = 128

LANES = 128
HP = LANES
VMEM_LIMIT = 56 * 1024 * 1024

S_Q = 0
S_KV = S_Q + Q_LORA
S_KRM = S_KV + KV_LORA
S_KRR = S_KRM + HP
S_CB = S_KRR + HP
S_CC = S_CB + CONV_DIM
S_CH = S_CC + CONV_DIM
S_U = S_CH + CONV_DIM
S_V = S_U + GM_DIM
S_G = S_V + GM_DIM
S_END = S_G + N_BRANCH * D_MODEL

TM_IN = 512
TQ = 512
TK = 512
FLASH_HEADS = 2
TM_MERGE = 512
TM_ROUTE = 512
TM_MOE = 1024
EXPERTS_PER_STEP = 2
MOE_BLOCK = 256
MOE_CHUNK = 16
EXP_CHUNKS = 64
EXP_BUFFERS = 3
COMBINE_BUFFERS = 3
MOE_ROWS = -(-(MOE_BLOCK * TOP_K + N_EXPERTS * (MOE_CHUNK - 1)) // 512) * 512
PAGES_PER_STEP = 64
PAGES_PER_CHUNK = 16
PAGE_BUFFERS = 3
HEAD_ROWS = 16


def _cparams(sem):
    return pltpu.CompilerParams(dimension_semantics=sem, vmem_limit_bytes=VMEM_LIMIT)


def _const_spec(shape):
    nd = len(shape)
    return pl.BlockSpec(shape, lambda *_: (0,) * nd, pipeline_mode=pl.Buffered(1))


def _dot(a, b):
    return jnp.dot(a, b, preferred_element_type=F32)


def _rms_rows(x):
    return x * lax.rsqrt(jnp.mean(x * x, axis=-1, keepdims=True) + EPS)


def _silu(x):
    return x * jax.nn.sigmoid(x)


def _adaln_kernel(c_ref, w_ref, b_ref, o_ref):
    s = _silu(c_ref[...]).astype(BF16)
    o_ref[...] = _dot(s, w_ref[...].astype(BF16)) + b_ref[...]


def _adaln(layer, c_all, w_ada, b_ada):
    n, d = c_all.shape
    e = w_ada.shape[2]
    tn = 1536
    return pl.pallas_call(
        _adaln_kernel,
        grid=(e // tn,),
        in_specs=[pl.BlockSpec((n, d), lambda j: (0, 0)),
                  pl.BlockSpec((None, d, tn), lambda j: (layer, 0, j)),
                  pl.BlockSpec((None, 1, tn), lambda j: (layer, 0, j))],
        out_specs=pl.BlockSpec((n, tn), lambda j: (0, j)),
        out_shape=jax.ShapeDtypeStruct((n, e), F32),
        compiler_params=_cparams(("arbitrary",)),
        name="adaln",
    )(c_all, w_ada, b_ada)


def _inproj_common(x_ref, sc_ref, sh_ref, gmix_ref, w2_ref):
    h = (_rms_rows(x_ref[...]) * gmix_ref[...] * (1.0 + sc_ref[...]) + sh_ref[...]).astype(BF16)

    def seg(a, b):
        return _dot(h, w2_ref[:, a:b])

    return seg


def _heads_q(seg, gq_ref, wuq_ref, gqq_ref, cos, sin):
    cqn = (_rms_rows(seg(S_Q, S_KV)) * gq_ref[...]).astype(BF16)
    q2 = _dot(cqn, wuq_ref[...])
    out = []
    for hh in range(N_HEADS):
        qm = q2[:, hh * HP:(hh + 1) * HP]
        qr = q2[:, (N_HEADS + hh) * HP:(N_HEADS + hh + 1) * HP]
        qh = qm * cos + qr * sin
        inv = lax.rsqrt(jnp.sum(qh * qh, axis=-1, keepdims=True) * (1.0 / QK_DIM) + EPS)
        out.append(qh * inv * gqq_ref[...] * (QK_DIM ** -0.5))
    return out


def _latent_kv(seg, gkv_ref, cos, sin):
    ckv = _rms_rows(seg(S_KV, S_KRM)) * gkv_ref[...]
    krp = seg(S_KRM, S_KRR) * cos + seg(S_KRR, S_CB) * sin
    return ckv, krp


def _gates_and_partial(seg, brb_in, brc_in, wbrb_ref, wbrc_ref, bg0_ref, part_ref):
    brb = _dot(brb_in.astype(BF16), wbrb_ref[...])
    brc = _dot(brc_in.astype(BF16), wbrc_ref[...])
    bg = jax.nn.sigmoid(seg(S_G, S_END))
    bg0_ref[...] = bg[:, :D_MODEL].astype(BF16)
    part_ref[...] = bg[:, D_MODEL:2 * D_MODEL] * brb + bg[:, 2 * D_MODEL:] * brc


def _layernorm_rows(v, g):
    xc = v - jnp.mean(v, axis=-1, keepdims=True)
    return xc * lax.rsqrt(jnp.mean(xc * xc, axis=-1, keepdims=True) + EPS) * g


def _inproj_prompt_kernel(x_ref, sc_ref, sh_ref, gmix_ref, w2_ref, gq_ref, wuq_ref, gqq_ref,
                          gkv_ref, wuk_ref, gqk_ref, wuv_ref, cos_ref, sin_ref, wconv_ref,
                          wbrb_ref, gvln_ref, wsp_ref, bsp_ref, wbrc_ref,
                          q_ref, k_ref, v_ref, ckv_ref, kr_ref, tail_ref, cv_ref, bg0_ref, part_ref,
                          zbuf_ref, *, tm):
    t = pl.program_id(1)
    seg = _inproj_common(x_ref, sc_ref, sh_ref, gmix_ref, w2_ref)
    cos = cos_ref[...]
    sin = sin_ref[...]

    qs = _heads_q(seg, gq_ref, wuq_ref, gqq_ref, cos, sin)
    for hh in range(N_HEADS):
        q_ref[hh] = qs[hh].astype(BF16)
    ckv, krp = _latent_kv(seg, gkv_ref, cos, sin)
    ckv_ref[...] = ckv
    kr_ref[...] = krp
    ckv_b = ckv.astype(BF16)
    k2 = _dot(ckv_b, wuk_ref[...])
    v2 = _dot(ckv_b, wuv_ref[...])
    ones_lane = jnp.where(lax.broadcasted_iota(jnp.int32, (1, HP), 1) == V_DIM, 1.0, 0.0)
    for hh in range(N_HEADS):
        kh = k2[:, hh * HP:(hh + 1) * HP] + krp
        inv = lax.rsqrt(jnp.sum(kh * kh, axis=-1, keepdims=True) * (1.0 / QK_DIM) + EPS)
        k_ref[hh] = (kh * inv * gqk_ref[...]).astype(BF16)
        v_ref[hh] = (v2[:, hh * HP:(hh + 1) * HP] + ones_lane).astype(BF16)

    gate_b = seg(S_CB, S_CC)
    z = seg(S_CC, S_CH) * seg(S_CH, S_U)

    @pl.when(t == 0)
    def _():
        zbuf_ref[0:8, :] = jnp.zeros((8, CONV_DIM), F32)

    zbuf_ref[8:8 + tm, :] = z
    z1 = zbuf_ref[7:7 + tm, :]
    z2 = zbuf_ref[6:6 + tm, :]
    wc = wconv_ref[...]
    y = wc[0:1, :] * z2 + wc[1:2, :] * z1 + wc[2:3, :] * z
    zbuf_ref[0:8, :] = z[tm - 8:tm, :]
    tail_ref[...] = z[tm - 8:tm, :]

    u = seg(S_U, S_V)
    vn = _layernorm_rows(seg(S_V, S_G), gvln_ref[...])
    cv_ref[...] = vn[tm - CHUNK:tm, :]
    vnb = vn.astype(BF16)
    rows = lax.broadcasted_iota(jnp.int32, (GM_GROUPS * CHUNK, CHUNK), 0) % CHUNK
    cols = lax.broadcasted_iota(jnp.int32, (GM_GROUPS * CHUNK, CHUNK), 1)
    wsp = jnp.where(cols <= rows, wsp_ref[...], 0.0).astype(BF16)
    lane_grp = lax.broadcasted_iota(jnp.int32, (CHUNK, GM_DIM), 1) // GM_GROUP_DIM
    sgs = []
    for c in range(tm // CHUNK):
        r = _dot(wsp, vnb[c * CHUNK:(c + 1) * CHUNK, :])
        s = bsp_ref[...]
        for g in range(GM_GROUPS):
            s = s + jnp.where(lane_grp == g, r[g * CHUNK:(g + 1) * CHUNK, :], 0.0)
        sgs.append(u[c * CHUNK:(c + 1) * CHUNK, :] * s)
    sg = jnp.concatenate(sgs, axis=0)

    _gates_and_partial(seg, gate_b * y, sg, wbrb_ref, wbrc_ref, bg0_ref, part_ref)


def _inproj_sample_kernel(x_ref, sc_ref, sh_ref, gmix_ref, w2_ref, gq_ref, wuq_ref, gqq_ref,
                          gkv_ref, gqk_ref, wukt_ref, cos_ref, sin_ref, wconv_ref, h0_ref, h1_ref,
                          wbrb_ref, gvln_ref, coef_ref, bias_ref, wbrc_ref,
                          qg_ref, ql_ref, ckv_ref, kr_ref, z_ref, cv_ref, bg0_ref, part_ref):
    seg = _inproj_common(x_ref, sc_ref, sh_ref, gmix_ref, w2_ref)
    cos = cos_ref[...]
    sin = sin_ref[...]
    qs = _heads_q(seg, gq_ref, wuq_ref, gqq_ref, cos, sin)
    for hh in range(N_HEADS):
        qg = qs[hh] * gqk_ref[...]
        qg_ref[hh] = qg
        ql_ref[hh] = _dot(qg.astype(BF16), wukt_ref[hh])
    ckv, krp = _latent_kv(seg, gkv_ref, cos, sin)
    ckv_ref[...] = ckv
    kr_ref[...] = krp

    gate_b = seg(S_CB, S_CC)
    z = seg(S_CC, S_CH) * seg(S_CH, S_U)
    wc = wconv_ref[...]
    y = wc[0:1, :] * h0_ref[...] + wc[1:2, :] * h1_ref[...] + wc[2:3, :] * z
    z_ref[...] = z

    u = seg(S_U, S_V)
    vn = _layernorm_rows(seg(S_V, S_G), gvln_ref[...])
    cv_ref[...] = vn
    sg = u * (vn * coef_ref[...] + bias_ref[...])

    _gates_and_partial(seg, gate_b * y, sg, wbrb_ref, wbrc_ref, bg0_ref, part_ref)


def _inproj_prompt(x, scale_m, shift_m, lw, cos, sin):
    b, t, d = x.shape
    tm = TM_IN
    nt = t // tm
    tok = lambda w: pl.BlockSpec((None, tm, w), lambda i, j: (i, j, 0))
    mod = pl.BlockSpec((None, 1, d), lambda i, j: (i, 0, 0))
    head = pl.BlockSpec((None, N_HEADS, tm, HP), lambda i, j: (i, 0, j, 0))
    in_specs = [
        tok(d), mod, mod, _const_spec((1, d)), _const_spec((d, S_END)),
        _const_spec((1, Q_LORA)), _const_spec((Q_LORA, 2 * N_HEADS * HP)), _const_spec((1, HP)),
        _const_spec((1, KV_LORA)), _const_spec((KV_LORA, N_HEADS * HP)), _const_spec((1, HP)),
        _const_spec((KV_LORA, N_HEADS * HP)),
        pl.BlockSpec((tm, HP), lambda i, j: (j, 0)), pl.BlockSpec((tm, HP), lambda i, j: (j, 0)),
        _const_spec((CONV_W, CONV_DIM)), _const_spec((CONV_DIM, d)), _const_spec((1, GM_DIM)),
        _const_spec((GM_GROUPS * CHUNK, CHUNK)), _const_spec((CHUNK, GM_DIM)), _const_spec((GM_DIM, d)),
    ]
    out_specs = [
        head, head, head, tok(KV_LORA), tok(HP),
        pl.BlockSpec((None, 8, CONV_DIM), lambda i, j: (i, 0, 0)),
        pl.BlockSpec((None, CHUNK, GM_DIM), lambda i, j: (i, 0, 0)),
        tok(d), tok(d),
    ]
    out_shape = [
        jax.ShapeDtypeStruct((b, N_HEADS, t, HP), BF16),
        jax.ShapeDtypeStruct((b, N_HEADS, t, HP), BF16),
        jax.ShapeDtypeStruct((b, N_HEADS, t, HP), BF16),
        jax.ShapeDtypeStruct((b, t, KV_LORA), F32),
        jax.ShapeDtypeStruct((b, t, HP), F32),
        jax.ShapeDtypeStruct((b, 8, CONV_DIM), F32),
        jax.ShapeDtypeStruct((b, CHUNK, GM_DIM), F32),
        jax.ShapeDtypeStruct((b, t, d), BF16),
        jax.ShapeDtypeStruct((b, t, d), F32),
    ]
    return pl.pallas_call(
        functools.partial(_inproj_prompt_kernel, tm=tm),
        grid=(b, nt),
        in_specs=in_specs, out_specs=out_specs, out_shape=out_shape,
        scratch_shapes=[pltpu.VMEM((tm + 8, CONV_DIM), F32)],
        compiler_params=_cparams(("arbitrary", "arbitrary")),
        name="inproj_prompt",
    )(x, scale_m, shift_m, lw["g_norm_mix"], lw["w2"], lw["g_q_lat"], lw["wuq2"], lw["g_qk_q"],
      lw["g_kv_lat"], lw["wuk"], lw["g_qk_k"], lw["wuv"], cos, sin, lw["w_conv"],
      lw["w_br_b"], lw["g_v_ln"], lw["w_sp"], lw["b_sp"], lw["w_br_c"])


def _inproj_sample(x, scale_m, shift_m, lw, cos, sin, hist0, hist1):
    n, d = x.shape
    full = lambda *s: pl.BlockSpec(s, lambda i: (0,) * len(s))
    in_specs = [
        full(n, d), full(n, d), full(n, d), full(1, d), full(d, S_END),
        full(1, Q_LORA), full(Q_LORA, 2 * N_HEADS * HP), full(1, HP),
        full(1, KV_LORA), full(1, HP), full(N_HEADS, HP, KV_LORA), full(1, HP), full(1, HP),
        full(CONV_W, CONV_DIM), full(n, CONV_DIM), full(n, CONV_DIM),
        full(CONV_DIM, d), full(1, GM_DIM), full(1, GM_DIM), full(1, GM_DIM), full(GM_DIM, d),
    ]
    out_specs = [full(N_HEADS, n, HP), full(N_HEADS, n, KV_LORA), full(n, KV_LORA), full(n, HP),
                 full(n, CONV_DIM), full(n, GM_DIM), full(n, d), full(n, d)]
    out_shape = [
        jax.ShapeDtypeStruct((N_HEADS, n, HP), F32),
        jax.ShapeDtypeStruct((N_HEADS, n, KV_LORA), F32),
        jax.ShapeDtypeStruct((n, KV_LORA), F32),
        jax.ShapeDtypeStruct((n, HP), F32),
        jax.ShapeDtypeStruct((n, CONV_DIM), F32),
        jax.ShapeDtypeStruct((n, GM_DIM), F32),
        jax.ShapeDtypeStruct((n, d), BF16),
        jax.ShapeDtypeStruct((n, d), F32),
    ]
    return pl.pallas_call(
        _inproj_sample_kernel,
        grid=(1,),
        in_specs=in_specs, out_specs=out_specs, out_shape=out_shape,
        compiler_params=_cparams(("arbitrary",)),
        name="inproj_sample",
    )(x, scale_m, shift_m, lw["g_norm_mix"], lw["w2"], lw["g_q_lat"], lw["wuq2"], lw["g_qk_q"],
      lw["g_kv_lat"], lw["g_qk_k"], lw["wukt_pad"], cos, sin, lw["w_conv"], hist0, hist1,
      lw["w_br_b"], lw["g_v_ln"], lw["sp_coef"], lw["b_sp"][0:1], lw["w_br_c"])


def _flash_kernel(q_ref, k_ref, v_ref, o_ref, *, tq, tk):
    assert tq == tk
    qi = pl.program_id(2)
    nh = q_ref.shape[0]
    qs = [q_ref[hh] for hh in range(nh)]

    def step(j, carry, masked):
        start = pl.multiple_of(j * tk, tk)
        out = []
        for hh in range(nh):
            m, acc = carry[hh]
            k = k_ref[hh, pl.ds(start, tk), :]
            v = v_ref[hh, pl.ds(start, tk), :]
            s = lax.dot_general(qs[hh], k, (((1,), (1,)), ((), ())), preferred_element_type=F32)
            if masked:
                row = lax.broadcasted_iota(jnp.int32, (tq, tk), 0)
                col = lax.broadcasted_iota(jnp.int32, (tq, tk), 1)
                s = jnp.where(col <= row, s, -jnp.inf)
            m_new = jnp.maximum(m, jnp.max(s, axis=-1, keepdims=True))
            alpha = jnp.exp(m - m_new)
            p = jnp.exp(s - m_new)
            out.append((m_new, alpha * acc + _dot(p.astype(BF16), v)))
        return tuple(out)

    m0 = jnp.full((tq, 1), -jnp.inf, F32)
    a0 = jnp.zeros((tq, HP), F32)
    carry = lax.fori_loop(0, qi, lambda j, c: step(j, c, False), tuple((m0, a0) for _ in range(nh)))
    carry = step(qi, carry, True)
    lane = lax.broadcasted_iota(jnp.int32, (tq, HP), 1)
    for hh in range(nh):
        acc = carry[hh][1]
        l = jnp.sum(jnp.where(lane == V_DIM, acc, 0.0), axis=-1, keepdims=True)
        o_ref[:, hh * HP:(hh + 1) * HP] = (acc / l).astype(BF16)


def _flash_attention(q, k, v):
    b, h, t, _ = q.shape
    tq, tk = TQ, TK
    nh = FLASH_HEADS
    return pl.pallas_call(
        functools.partial(_flash_kernel, tq=tq, tk=tk),
        grid=(b, h // nh, t // tq),
        in_specs=[pl.BlockSpec((None, nh, tq, HP), lambda i, j, n: (i, j, n, 0)),
                  pl.BlockSpec((None, nh, t, HP), lambda i, j, n: (i, j, 0, 0)),
                  pl.BlockSpec((None, nh, t, HP), lambda i, j, n: (i, j, 0, 0))],
        out_specs=pl.BlockSpec((None, tq, nh * HP), lambda i, j, n: (i, n, j)),
        out_shape=jax.ShapeDtypeStruct((b, t, h * HP), BF16),
        compiler_params=_cparams(("arbitrary", "arbitrary", "arbitrary")),
        name="flash_prompt",
    )(q, k, v)


def _paged_kernel(pt_ref, wkt_ref, ql_ref, qr_ref, cnew_ref, krnew_ref, poolc_ref, poolk_ref, o_ref,
                  cbuf, kbuf, csem, ksem, lhs_ref, m_ref, l_ref, acc_ref, *, layer, n_pages, n_pg, n_chunk):
    i = pl.program_id(0)
    j = pl.program_id(1)
    nj = pl.num_programs(1)
    nk = N_HEADS * QK_NOPE
    step = i * nj + j
    last_step = pl.num_programs(0) * nj - 1
    ahead = PAGE_BUFFERS - 1
    slot = step % PAGE_BUFFERS
    nxt_slot = (step + ahead) % PAGE_BUFFERS

    def step_base(t):
        t = jnp.minimum(t, last_step)
        return (t // nj) * n_pages + (t % nj) * n_pg

    nxt_base = step_base(step + ahead)

    def page_copies(base, buf_slot, p):
        pid = pt_ref[base + p]
        return (pltpu.make_async_copy(poolc_ref.at[layer, pid], cbuf.at[buf_slot, p], csem.at[buf_slot]),
                pltpu.make_async_copy(poolk_ref.at[layer, pid], kbuf.at[buf_slot, p], ksem.at[buf_slot]))

    def start_page(base, buf_slot, p):
        for cp in page_copies(base, buf_slot, p):
            cp.start()

    @pl.when(step == 0)
    def _():
        for a in range(ahead):
            for p in range(n_pg):
                start_page(step_base(a), a, p)

    for p in range(n_pg):
        for cp in page_copies(i * n_pages + j * n_pg, slot, p):
            cp.wait()

    def at_step(cond):
        return (lambda f: f()) if n_pg == n_pages else pl.when(cond)

    @at_step(j == 0)
    def _():
        lhs_ref[0:nk, :] = wkt_ref[...]
        lhs_ref[nk:nk + HEAD_ROWS, :] = ql_ref[...]
        m_ref[...] = jnp.full(m_ref.shape, -jnp.inf, F32)
        l_ref[...] = jnp.zeros(l_ref.shape, F32)
        acc_ref[...] = jnp.zeros(acc_ref.shape, F32)

    qr = qr_ref[...]

    def update(cb, krt, valid_keys):
        kx = lax.dot_general(lhs_ref[...], cb, (((1,), (1,)), ((), ())), preferred_element_type=F32)
        sq = kx[:nk, :] * kx[:nk, :]
        ssq = jnp.sum(sq.reshape(QK_NOPE, N_HEADS, sq.shape[-1]), axis=0)
        ssq = ssq + jnp.sum(krt * krt, axis=0, keepdims=True)
        inv = lax.rsqrt(ssq * (1.0 / QK_DIM) + EPS)
        st = (kx[nk:, :] + _dot(qr, krt.astype(BF16))) * jnp.concatenate([inv, inv], axis=0)
        if valid_keys is not None:
            kcol = lax.broadcasted_iota(jnp.int32, st.shape, 1)
            st = jnp.where(kcol < valid_keys, st, -jnp.inf)
        m_old = m_ref[...]
        m_new = jnp.maximum(m_old, jnp.max(st, axis=-1, keepdims=True))
        alpha = jnp.exp(m_old - m_new)
        p = jnp.exp(st - m_new)
        l_ref[...] = alpha * l_ref[...] + jnp.sum(p, axis=-1, keepdims=True)
        acc_ref[...] = alpha * acc_ref[...] + _dot(p.astype(BF16), cb)
        m_ref[...] = m_new

    per = n_pg // n_chunk
    for ch in range(n_chunk):
        cb = cbuf[slot, ch * per:(ch + 1) * per].reshape(per * PAGE_SIZE, KV_LORA).astype(BF16)
        krt = jnp.concatenate([kbuf[slot, p] for p in range(ch * per, (ch + 1) * per)], axis=1)
        update(cb, krt, None)
        for p in range(ch * per, (ch + 1) * per):
            start_page(nxt_base, nxt_slot, p)

    @at_step(j == nj - 1)
    def _():
        first = lax.broadcasted_iota(jnp.int32, (LANES, KV_LORA), 0) == 0
        cn = jnp.where(first, jnp.broadcast_to(cnew_ref[...], (LANES, KV_LORA)), 0.0).astype(BF16)
        update(cn, krnew_ref[...], 1)
        o_ref[...] = (acc_ref[...] / l_ref[...])[:N_HEADS, :]

    @pl.when(step == last_step)
    def _():
        for a in range(1, PAGE_BUFFERS):
            for p in range(n_pg):
                for cp in page_copies(nxt_base, (step + a) % PAGE_BUFFERS, p):
                    cp.wait()


def _paged_attention(layer, page_table, wkt, qlat, qr, cnew, krnew_t, pool_c, pool_krt):
    s, n_pages = page_table.shape
    n_pg = min(PAGES_PER_STEP, n_pages)
    assert n_pages % n_pg == 0 and n_pg % PAGES_PER_CHUNK == 0
    nj = n_pages // n_pg
    pt = page_table.reshape(-1)
    nk = N_HEADS * QK_NOPE

    per_seq = lambda *shape: pl.BlockSpec((None,) + shape, lambda i, j, pt_ref: (i,) + (0,) * len(shape))
    const = lambda *shape: pl.BlockSpec(shape, lambda i, j, pt_ref: (0,) * len(shape))
    in_specs = [const(nk, KV_LORA), per_seq(HEAD_ROWS, KV_LORA), per_seq(HEAD_ROWS, QK_ROPE),
                per_seq(1, KV_LORA), per_seq(QK_ROPE, LANES),
                pl.BlockSpec(memory_space=pl.ANY), pl.BlockSpec(memory_space=pl.ANY)]
    grid_spec = pltpu.PrefetchScalarGridSpec(
        num_scalar_prefetch=1,
        grid=(s, nj),
        in_specs=in_specs,
        out_specs=per_seq(N_HEADS, KV_LORA),
        scratch_shapes=[pltpu.VMEM((PAGE_BUFFERS, n_pg, PAGE_SIZE, KV_LORA), F32),
                        pltpu.VMEM((PAGE_BUFFERS, n_pg, QK_ROPE, PAGE_SIZE), F32),
                        pltpu.SemaphoreType.DMA((PAGE_BUFFERS,)), pltpu.SemaphoreType.DMA((PAGE_BUFFERS,)),
                        pltpu.VMEM((nk + HEAD_ROWS, KV_LORA), BF16), pltpu.VMEM((HEAD_ROWS, 1), F32),
                        pltpu.VMEM((HEAD_ROWS, 1), F32), pltpu.VMEM((HEAD_ROWS, KV_LORA), F32)],
    )
    return pl.pallas_call(
        functools.partial(_paged_kernel, layer=layer, n_pages=n_pages, n_pg=n_pg, n_chunk=n_pg // PAGES_PER_CHUNK),
        grid_spec=grid_spec,
        out_shape=jax.ShapeDtypeStruct((s, N_HEADS, KV_LORA), F32),
        compiler_params=_cparams(("arbitrary", "arbitrary")),
        name="paged_sample",
    )(pt, wkt, qlat, qr, cnew, krnew_t, pool_c, pool_krt)


def _uv_kernel(ol_ref, wuv_ref, o_ref):
    for hh in range(N_HEADS):
        ol = ol_ref[:, hh * KV_LORA:(hh + 1) * KV_LORA].astype(BF16)
        o_ref[:, hh * HP:(hh + 1) * HP] = _dot(ol, wuv_ref[:, hh * HP:(hh + 1) * HP]).astype(BF16)


def _uv_project(olat, wuv):
    n = olat.shape[0]
    full = lambda *s: pl.BlockSpec(s, lambda i: (0,) * len(s))
    return pl.pallas_call(
        _uv_kernel, grid=(1,),
        in_specs=[full(n, N_HEADS * KV_LORA), full(KV_LORA, N_HEADS * HP)],
        out_specs=full(n, N_HEADS * HP),
        out_shape=jax.ShapeDtypeStruct((n, N_HEADS * HP), BF16),
        compiler_params=_cparams(("arbitrary",)),
        name="uv_sample",
    )(olat, wuv)


def _merge_kernel(oa_ref, wbra_ref, bg0_ref, part_ref, x_ref, gm_ref, wout_ref, gffn_ref, scf_ref, shf_ref,
                  x1_ref, h2_ref):
    bra = _dot(oa_ref[...], wbra_ref[...])
    merged = bg0_ref[...].astype(F32) * bra + part_ref[...]
    x1 = x_ref[...] + gm_ref[...] * _dot(merged.astype(BF16), wout_ref[...])
    x1_ref[...] = x1
    h2_ref[...] = (_rms_rows(x1) * gffn_ref[...] * (1.0 + scf_ref[...]) + shf_ref[...]).astype(BF16)


def _merge(oa, bg0, part, x, gate_m, scale_f, shift_f, lw, tm):
    g, t, d = x.shape
    mt = gate_m.shape[1]
    tok = lambda w: pl.BlockSpec((None, tm, w), lambda i, j: (i, j, 0))
    if mt == 1:
        mod = pl.BlockSpec((None, 1, d), lambda i, j: (i, 0, 0))
    else:
        mod = pl.BlockSpec((None, tm, d), lambda i, j: (i, j, 0))
    return pl.pallas_call(
        _merge_kernel,
        grid=(g, t // tm),
        in_specs=[tok(N_HEADS * HP), _const_spec((N_HEADS * HP, d)), tok(d), tok(d), tok(d), mod,
                  _const_spec((d, d)), _const_spec((1, d)), mod, mod],
        out_specs=[tok(d), tok(d)],
        out_shape=[jax.ShapeDtypeStruct((g, t, d), F32), jax.ShapeDtypeStruct((g, t, d), BF16)],
        compiler_params=_cparams(("arbitrary", "arbitrary")),
        name="merge",
    )(oa, lw["w_br_a"], bg0, part, x, gate_m, lw["w_out"], lw["g_norm_ffn"], scale_f, shift_f)


def _router_kernel(h_ref, wrt_ref, brt_ref, g_ref, idx_ref, w_ref):
    tm = h_ref.shape[0]
    logits = lax.dot_general(wrt_ref[...], h_ref[...], (((1,), (1,)), ((), ())), preferred_element_type=F32)
    scores = jax.nn.sigmoid(logits)
    choice = scores + brt_ref[...]
    neg = -jnp.inf

    gsc = []
    for g in range(N_EXPERT_GROUPS):
        xg = choice[g * GROUP_SIZE:(g + 1) * GROUP_SIZE, :]
        m1 = jnp.max(xg, axis=0, keepdims=True)
        cnt = jnp.sum(jnp.where(xg == m1, 1.0, 0.0), axis=0, keepdims=True)
        m2 = jnp.max(jnp.where(xg < m1, xg, neg), axis=0, keepdims=True)
        gsc.append(m1 + jnp.where(cnt >= 2.0, m1, m2))

    cands = []
    for g in range(N_EXPERT_GROUPS):
        rank = jnp.zeros((1, tm), F32)
        for o in range(N_EXPERT_GROUPS):
            if o == g:
                continue
            beats = (gsc[o] > gsc[g]) if o > g else (gsc[o] >= gsc[g])
            rank = rank + jnp.where(beats, 1.0, 0.0)
        keep = rank < float(TOPK_GROUPS)
        cands.append(jnp.where(keep, choice[g * GROUP_SIZE:(g + 1) * GROUP_SIZE, :], neg))
    cand = jnp.concatenate(cands, axis=0)

    ridx = lax.broadcasted_iota(jnp.int32, (N_EXPERTS, tm), 0)
    picked = jnp.zeros((N_EXPERTS, tm), F32)
    idxs, ws = [], []
    for _ in range(TOP_K):
        m = jnp.max(cand, axis=0, keepdims=True)
        idx = jnp.min(jnp.where(cand == m, ridx, N_EXPERTS), axis=0, keepdims=True)
        hit = ridx == idx
        picked = jnp.where(hit, 1.0, picked)
        cand = jnp.where(hit, neg, cand)
        idxs.append(idx)
        ws.append(jnp.sum(jnp.where(hit, scores, 0.0), axis=0, keepdims=True))

    w = jnp.where(picked > 0.0, scores, 0.0)
    norm = ROUTED_SCALE / jnp.sum(w, axis=0, keepdims=True)
    gt = w * norm
    g_ref[...] = jnp.concatenate([gt, jnp.zeros((LANES - N_EXPERTS, tm), F32)], axis=0).T
    idx_ref[...] = jnp.concatenate(idxs, axis=0)
    w_ref[...] = jnp.concatenate(ws, axis=0) * norm


def _router(h2, lw, tm):
    m, d = h2.shape
    return pl.pallas_call(
        _router_kernel,
        grid=(m // tm,),
        in_specs=[pl.BlockSpec((tm, d), lambda i: (i, 0)), _const_spec((N_EXPERTS, d)),
                  _const_spec((N_EXPERTS, 1))],
        out_specs=[pl.BlockSpec((tm, LANES), lambda i: (i, 0)), pl.BlockSpec((TOP_K, tm), lambda i: (0, i)),
                   pl.BlockSpec((TOP_K, tm), lambda i: (0, i))],
        out_shape=[jax.ShapeDtypeStruct((m, LANES), F32), jax.ShapeDtypeStruct((TOP_K, m), jnp.int32),
                   jax.ShapeDtypeStruct((TOP_K, m), F32)],
        compiler_params=_cparams(("arbitrary",)),
        name="router",
    )(h2, lw["w_router_t"], lw["b_router_t"])


def _swiglu(x, wgu):
    gu = _dot(x, wgu)
    return _silu(gu[:, :D_EXPERT]) * gu[:, D_EXPERT:]


def _expert_ffn(x, wg, wu, wd):
    h = _silu(_dot(x, wg.astype(BF16))) * _dot(x, wu.astype(BF16))
    return h, wd.astype(BF16)


def _moe_kernel(h_ref, g_ref, wg_ref, wu_ref, wd_ref, wsgu_ref, wsd_ref, x1_ref, gf_ref, o_ref, acc_ref, *, epb):
    e = pl.program_id(2)
    h = h_ref[...]
    tm = h.shape[0]

    @pl.when(e == 0)
    def _():
        acc_ref[...] = _dot(_swiglu(h, wsgu_ref[...]).astype(BF16), wsd_ref[...])

    lane = lax.broadcasted_iota(jnp.int32, (tm, LANES), 1)
    gates = g_ref[...]
    for jj in range(epb):
        gcol = jnp.sum(jnp.where(lane == e * epb + jj, gates, 0.0), axis=-1, keepdims=True)
        hh, wd = _expert_ffn(h, wg_ref[jj], wu_ref[jj], wd_ref[jj])
        acc_ref[...] += _dot((hh * gcol).astype(BF16), wd)

    @pl.when(e == pl.num_programs(2) - 1)
    def _():
        o_ref[...] = x1_ref[...] + gf_ref[...] * acc_ref[...]


def _moe(h2, gates, x1, gate_f, lw, tm):
    g, t, d = x1.shape
    mt = gate_f.shape[1]
    epb = EXPERTS_PER_STEP
    layer = lw["layer"]
    tok = lambda w: pl.BlockSpec((None, tm, w), lambda i, j, e: (i, j, 0))
    if mt == 1:
        mod = pl.BlockSpec((None, 1, d), lambda i, j, e: (i, 0, 0))
    else:
        mod = pl.BlockSpec((None, tm, d), lambda i, j, e: (i, j, 0))
    return pl.pallas_call(
        functools.partial(_moe_kernel, epb=epb),
        grid=(g, t // tm, N_EXPERTS // epb),
        in_specs=[tok(d), tok(LANES),
                  pl.BlockSpec((None, epb, d, D_EXPERT), lambda i, j, e: (layer, e, 0, 0)),
                  pl.BlockSpec((None, epb, d, D_EXPERT), lambda i, j, e: (layer, e, 0, 0)),
                  pl.BlockSpec((None, epb, D_EXPERT, d), lambda i, j, e: (layer, e, 0, 0)),
                  _const_spec((d, 2 * D_EXPERT)), _const_spec((D_EXPERT, d)),
                  tok(d), mod],
        out_specs=tok(d),
        out_shape=jax.ShapeDtypeStruct((g, t, d), F32),
        scratch_shapes=[pltpu.VMEM((tm, d), F32)],
        compiler_params=_cparams(("arbitrary", "arbitrary", "arbitrary")),
        name="moe",
    )(h2, gates, lw["w_e_gate"], lw["w_e_up"], lw["w_e_down"], lw["w_sh_gu"], lw["w_sh_down"], x1, gate_f)


def _dispatch_kernel(x_ref, idx_ref, xs_ref, posk_ref, nch_ref, off_ref, *, rc):
    x = x_ref[...]
    idx = idx_ref[...]
    nb = x.shape[0]

    eiota = lax.broadcasted_iota(jnp.int32, (N_EXPERTS, nb), 0)
    hits = [eiota == idx[k:k + 1, :] for k in range(TOP_K)]
    sel = jnp.zeros((N_EXPERTS, nb), F32)
    for hit in hits:
        sel = sel + jnp.where(hit, 1.0, 0.0)
    cnt = jnp.sum(sel, axis=1, keepdims=True)
    nch = jnp.floor((cnt + (MOE_CHUNK - 1.0)) * (1.0 / MOE_CHUNK))
    nch_b = jnp.broadcast_to(nch, (N_EXPERTS, LANES))
    earlier = (lax.broadcasted_iota(jnp.int32, (N_EXPERTS, N_EXPERTS), 1)
               < lax.broadcasted_iota(jnp.int32, (N_EXPERTS, N_EXPERTS), 0))
    off_b = _dot(jnp.where(earlier, 1.0, 0.0).astype(BF16), nch_b.astype(BF16))
    before = (lax.broadcasted_iota(jnp.int32, (nb, nb), 0) < lax.broadcasted_iota(jnp.int32, (nb, nb), 1))
    rank = _dot(sel.astype(BF16), jnp.where(before, 1.0, 0.0).astype(BF16))
    pos = off_b[:, 0:1] * float(MOE_CHUNK) + rank
    posk = jnp.concatenate([jnp.sum(jnp.where(hit, pos, 0.0), axis=0, keepdims=True) for hit in hits],
                           axis=0).astype(jnp.int32)
    posk_ref[...] = posk
    nch_ref[...] = nch_b.astype(jnp.int32)
    off_ref[...] = off_b.astype(jnp.int32)

    posk16 = posk.astype(jnp.int16)
    for c in range(MOE_ROWS // rc):
        riota = (c * rc + lax.broadcasted_iota(jnp.int32, (rc, nb), 0)).astype(jnp.int16)
        p = jnp.zeros((rc, nb), BF16)
        for k in range(TOP_K):
            p = jnp.where(riota == posk16[k:k + 1, :], jnp.ones((), BF16), p)
        xs_ref[c * rc:(c + 1) * rc, :] = _dot(p, x).astype(BF16)


def _dispatch(h2, idx_t):
    nblk, nb, d = h2.shape
    small = pl.BlockSpec((None, N_EXPERTS, LANES), lambda i: (i, 0, 0))
    return pl.pallas_call(
        functools.partial(_dispatch_kernel, rc=512),
        grid=(nblk,),
        in_specs=[pl.BlockSpec((None, nb, d), lambda i: (i, 0, 0)),
                  pl.BlockSpec((TOP_K, nb), lambda i: (0, i))],
        out_specs=[pl.BlockSpec((None, MOE_ROWS, d), lambda i: (i, 0, 0)),
                   pl.BlockSpec((None, TOP_K, nb), lambda i: (i, 0, 0)), small, small],
        out_shape=[jax.ShapeDtypeStruct((nblk, MOE_ROWS, d), BF16),
                   jax.ShapeDtypeStruct((nblk, TOP_K, nb), jnp.int32),
                   jax.ShapeDtypeStruct((nblk, N_EXPERTS, LANES), jnp.int32),
                   jax.ShapeDtypeStruct((nblk, N_EXPERTS, LANES), jnp.int32)],
        compiler_params=_cparams(("arbitrary",)),
        name="moe_dispatch",
    )(h2, idx_t)


def _start_all(copies):
    for cp in copies:
        cp.start()


def _expert_kernel(te_ref, src_ref, nact_ref, xs_ref, wg_ref, wu_ref, wd_ref, ys_ref, xbuf, sem):
    t = pl.program_id(0)
    nact = nact_ref[0]
    ahead = EXP_BUFFERS - 1
    slot = t % EXP_BUFFERS

    def copies(tile, buf_slot):
        return [pltpu.make_async_copy(xs_ref.at[src_ref[tile * EXP_CHUNKS + c]], xbuf.at[buf_slot, c],
                                      sem.at[buf_slot]) for c in range(EXP_CHUNKS)]

    @pl.when(t == 0)
    def _():
        for a in range(ahead):
            @pl.when(a < nact)
            def _():
                _start_all(copies(a, a))

    @pl.when(t + ahead < nact)
    def _():
        _start_all(copies(t + ahead, (t + ahead) % EXP_BUFFERS))

    @pl.when(t < nact)
    def _():
        for cp in copies(t, slot):
            cp.wait()
        x = xbuf[slot].reshape(EXP_CHUNKS * MOE_CHUNK, xbuf.shape[-1])
        hh, wd = _expert_ffn(x, wg_ref[...], wu_ref[...], wd_ref[...])
        ys_ref[...] = _dot(hh.astype(BF16), wd).astype(BF16)

    @pl.when(t >= nact)
    def _():
        ys_ref[...] = jnp.zeros(ys_ref.shape, BF16)


def _experts(xs_chunks, tile_expert, src_chunk, n_active, lw):
    d = xs_chunks.shape[-1]
    n_tiles = tile_expert.shape[0]
    tm = EXP_CHUNKS * MOE_CHUNK
    layer = lw["layer"]
    grid_spec = pltpu.PrefetchScalarGridSpec(
        num_scalar_prefetch=3,
        grid=(n_tiles,),
        in_specs=[pl.BlockSpec(memory_space=pl.ANY),
                  pl.BlockSpec((None, None, d, D_EXPERT), lambda t, te, src, na: (layer, te[t], 0, 0)),
                  pl.BlockSpec((None, None, d, D_EXPERT), lambda t, te, src, na: (layer, te[t], 0, 0)),
                  pl.BlockSpec((None, None, D_EXPERT, d), lambda t, te, src, na: (layer, te[t], 0, 0))],
        out_specs=pl.BlockSpec((tm, d), lambda t, te, src, na: (t, 0)),
        scratch_shapes=[pltpu.VMEM((EXP_BUFFERS, EXP_CHUNKS, MOE_CHUNK, d), BF16),
                        pltpu.SemaphoreType.DMA((EXP_BUFFERS,))],
    )
    return pl.pallas_call(
        _expert_kernel,
        grid_spec=grid_spec,
        out_shape=jax.ShapeDtypeStruct((n_tiles * tm, d), BF16),
        compiler_params=_cparams(("arbitrary",)),
        name="moe_experts",
    )(tile_expert, src_chunk, n_active, xs_chunks, lw["w_e_gate"], lw["w_e_up"], lw["w_e_down"])


def _combine_kernel(src_ref, ys_ref, pos_ref, w_ref, h_ref, x1_ref, gf_ref, wsgu_ref, wsd_ref,
                    o_ref, ybuf, sem, *, cc):
    b = pl.program_id(0)
    last = pl.num_programs(0) - 1
    ahead = COMBINE_BUFFERS - 1
    slot = b % COMBINE_BUFFERS
    nxt_slot = (b + ahead) % COMBINE_BUFFERS
    nxt = jnp.minimum(b + ahead, last)
    nch = MOE_ROWS // MOE_CHUNK

    def copies(blk, buf_slot, lo=0, hi=nch):
        return [pltpu.make_async_copy(ys_ref.at[src_ref[blk * nch + c]], ybuf.at[buf_slot, c], sem.at[buf_slot])
                for c in range(lo, hi)]

    @pl.when(b == 0)
    def _():
        for a in range(ahead):
            _start_all(copies(jnp.minimum(a, last), a))

    acc = _dot(_swiglu(h_ref[...], wsgu_ref[...]).astype(BF16), wsd_ref[...])

    for cp in copies(b, slot):
        cp.wait()
    pos = pos_ref[...].astype(jnp.int16)
    w = w_ref[...].astype(BF16)
    nb = pos.shape[0]
    per = cc // MOE_CHUNK
    for c in range(MOE_ROWS // cc):
        liota = (c * cc + lax.broadcasted_iota(jnp.int32, (nb, cc), 1)).astype(jnp.int16)
        pw = jnp.zeros((nb, cc), BF16)
        for k in range(TOP_K):
            pw = jnp.where(liota == pos[:, k:k + 1], w[:, k:k + 1], pw)
        y = ybuf[slot, c * per:(c + 1) * per].reshape(cc, ybuf.shape[-1])
        acc = acc + _dot(pw, y)
        _start_all(copies(nxt, nxt_slot, c * per, (c + 1) * per))
    o_ref[...] = x1_ref[...] + gf_ref[...] * acc

    @pl.when(b == last)
    def _():
        for a in range(1, COMBINE_BUFFERS):
            for cp in copies(nxt, (b + a) % COMBINE_BUFFERS):
                cp.wait()


def _combine(ys_chunks, src_chunk, posk, wk, h2, x1, gate_f, lw, blocks_per_seq):
    nblk, nb, d = h2.shape
    nch = MOE_ROWS // MOE_CHUNK
    blk = lambda w: pl.BlockSpec((None, nb, w), lambda i, src: (i, 0, 0))
    const = lambda *s: pl.BlockSpec(s, lambda i, src: (0,) * len(s))
    grid_spec = pltpu.PrefetchScalarGridSpec(
        num_scalar_prefetch=1,
        grid=(nblk,),
        in_specs=[pl.BlockSpec(memory_space=pl.ANY), blk(TOP_K), blk(TOP_K), blk(d), blk(d),
                  pl.BlockSpec((None, 1, d), lambda i, src: (i // blocks_per_seq, 0, 0)),
                  const(d, 2 * D_EXPERT), const(D_EXPERT, d)],
        out_specs=blk(d),
        scratch_shapes=[pltpu.VMEM((COMBINE_BUFFERS, nch, MOE_CHUNK, d), BF16),
                        pltpu.SemaphoreType.DMA((COMBINE_BUFFERS,))],
    )
    return pl.pallas_call(
        functools.partial(_combine_kernel, cc=512),
        grid_spec=grid_spec,
        out_shape=jax.ShapeDtypeStruct((nblk, nb, d), F32),
        compiler_params=_cparams(("arbitrary",)),
        name="moe_combine",
    )(src_chunk, ys_chunks, posk, wk, h2, x1, gate_f, lw["w_sh_gu"], lw["w_sh_down"])


def _excl_cumsum(x, axis):
    n = x.shape[axis]
    earlier = jnp.arange(n)[:, None] > jnp.arange(n)[None, :]
    xm = jnp.moveaxis(x, axis, -1)
    out = jnp.sum(jnp.where(earlier, xm[..., None, :], 0), axis=-1)
    return jnp.moveaxis(out, -1, axis)


def _pick(onehot, table):
    return jnp.sum(jnp.where(onehot, table, 0), axis=-1)


def _moe_plan(nch, off, n_tokens):
    nblk = nch.shape[0]
    i32 = jnp.int32
    cpb = MOE_ROWS // MOE_CHUNK
    experts = jnp.arange(N_EXPERTS, dtype=i32)
    tot = jnp.sum(nch, axis=1)
    ech = jnp.sum(nch, axis=0)
    tiles_e = (ech + EXP_CHUNKS - 1) // EXP_CHUNKS
    tile_start = _excl_cumsum(tiles_e, 0)
    tile_end = tile_start + tiles_e
    n_active = jnp.sum(tiles_e)
    cumb = _excl_cumsum(nch, 0)

    n_tiles = _max_expert_tiles(n_tokens, nblk)
    tid = jnp.arange(n_tiles, dtype=i32)
    te = jnp.sum((tid[:, None] >= tile_end[None, :]).astype(i32), axis=1)
    te_last = jnp.sum((n_active - 1 >= tile_end).astype(i32))
    te = jnp.where(tid < n_active, te, te_last)
    is_e = te[:, None] == experts[None, :]
    ts_t = _pick(is_e, tile_start[None, :])
    ech_t = _pick(is_e, ech[None, :])
    ends_t = _pick(is_e[:, None, :], (cumb + nch)[None, :, :])
    off_t = _pick(is_e[:, None, :], off[None, :, :])
    cumb_t = _pick(is_e[:, None, :], cumb[None, :, :])
    c = (tid - ts_t)[:, None] * EXP_CHUNKS + jnp.arange(EXP_CHUNKS, dtype=i32)[None, :]
    valid = jnp.logical_and(c < ech_t[:, None], (tid < n_active)[:, None])
    sb = jnp.minimum(jnp.sum((c[:, :, None] >= ends_t[:, None, :]).astype(i32), axis=2), nblk - 1)
    is_b = sb[:, :, None] == jnp.arange(nblk, dtype=i32)[None, None, :]
    src_e = jnp.where(valid, sb * cpb + _pick(is_b, off_t[:, None, :]) + c - _pick(is_b, cumb_t[:, None, :]), 0)

    j = jnp.arange(cpb, dtype=i32)
    ce = jnp.minimum(jnp.sum((j[None, :, None] >= (off + nch)[:, None, :]).astype(i32), axis=2), N_EXPERTS - 1)
    is_ce = ce[:, :, None] == experts[None, None, :]
    g = (_pick(is_ce, tile_start[None, None, :]) * EXP_CHUNKS + _pick(is_ce, cumb[:, None, :])
         + j[None, :] - _pick(is_ce, off[:, None, :]))
    src_c = jnp.where(j[None, :] < tot[:, None], g, 0)
    return te, src_e.reshape(-1), n_active.reshape(1), src_c.reshape(-1)


def _max_expert_tiles(t, nblk):
    max_chunks = t * TOP_K // MOE_CHUNK + nblk * N_EXPERTS
    return max_chunks // EXP_CHUNKS + N_EXPERTS


def _moe_sparse(h2, idx_t, w_t, x1, gate_f, lw):
    b, t, d = x1.shape
    nblk = b * t // MOE_BLOCK
    h2b = h2.reshape(nblk, MOE_BLOCK, d)
    xs, posk_t, nch, off = _dispatch(h2b, idx_t)
    te, src_e, n_active, src_c = _moe_plan(nch[:, :, 0], off[:, :, 0], b * t)
    ys = _experts(xs.reshape(nblk * MOE_ROWS // MOE_CHUNK, MOE_CHUNK, d), te, src_e, n_active, lw)
    wk = w_t.reshape(TOP_K, nblk, MOE_BLOCK).transpose(1, 2, 0)
    out = _combine(ys.reshape(-1, MOE_CHUNK, d), src_c, posk_t.transpose(0, 2, 1), wk,
                   h2b, x1.reshape(nblk, MOE_BLOCK, d), gate_f, lw, t // MOE_BLOCK)
    return out.reshape(b, t, d)


def _pad_heads(w, width):
    pad = [(0, 0)] * (w.ndim - 1) + [(0, HP - width)]
    wp = jnp.pad(w, pad)
    return wp.reshape(w.shape[:-2] + (w.shape[-2] * HP,))


def _rot_cols(w):
    half = QK_ROPE // 2
    return jnp.concatenate([-w[..., half:], w[..., :half]], axis=-1)


def _prep_layer(l, p):
    d = D_MODEL
    w_in = p["w_in"][l]
    off_kv = Q_LORA
    off_kr = off_kv + KV_LORA
    off_cb = off_kr + QK_ROPE
    off_g = off_cb + 3 * CONV_DIM + 2 * GM_DIM
    wkr = w_in[:, off_kr:off_cb]
    lead = jnp.zeros((d, QK_NOPE), F32)
    trail = jnp.zeros((d, HP - QK_DIM), F32)
    w2 = jnp.concatenate([
        w_in[:, :off_kr],
        lead, wkr, trail,
        lead, _rot_cols(wkr), trail,
        w_in[:, off_cb:],
    ], axis=1).astype(BF16)
    assert w2.shape[1] == S_END and off_g + N_BRANCH * d == w_in.shape[1]

    wuq = p["w_uq"][l]
    wuq_rot = jnp.concatenate([jnp.zeros_like(wuq[..., :QK_NOPE]), _rot_cols(wuq[..., QK_NOPE:])], axis=-1)
    wuq2 = jnp.concatenate([_pad_heads(wuq, QK_DIM), _pad_heads(wuq_rot, QK_DIM)], axis=-1).astype(BF16)

    def pad_gain(gq):
        return jnp.pad(gq, (0, HP - QK_DIM)).reshape(1, HP)

    w_br_a = jnp.pad(p["w_br_a"][l].reshape(N_HEADS, V_DIM, d), ((0, 0), (0, HP - V_DIM), (0, 0)))
    w_sp = p["w_spatial"][l]
    b_sp = jnp.repeat(p["b_spatial"][l].T, GM_GROUP_DIM, axis=1)
    return dict(
        g_norm_mix=p["g_norm_mix"][l].reshape(1, d), g_norm_ffn=p["g_norm_ffn"][l].reshape(1, d),
        w2=w2, g_q_lat=p["g_q_lat"][l].reshape(1, Q_LORA), wuq2=wuq2,
        g_qk_q=pad_gain(p["g_qk_q"][l]), g_qk_k=pad_gain(p["g_qk_k"][l]),
        g_kv_lat=p["g_kv_lat"][l].reshape(1, KV_LORA),
        wuk=_pad_heads(p["w_uk"][l], QK_NOPE).astype(BF16),
        wkt=p["w_uk"][l].transpose(2, 1, 0).reshape(QK_NOPE * N_HEADS, KV_LORA).astype(BF16),
        wukt_pad=jnp.pad(p["w_uk"][l].transpose(1, 2, 0), ((0, 0), (0, HP - QK_NOPE), (0, 0))).astype(BF16),
        wuv=_pad_heads(p["w_uv"][l], V_DIM).astype(BF16),
        w_br_a=w_br_a.reshape(N_HEADS * HP, d).astype(BF16),
        w_conv=p["w_conv"][l], w_br_b=p["w_br_b"][l].astype(BF16),
        g_v_ln=p["g_v_ln"][l].reshape(1, GM_DIM),
        w_sp=w_sp.reshape(GM_GROUPS * CHUNK, CHUNK), b_sp=b_sp,
        sp_coef=jnp.repeat(w_sp[:, 0, 0], GM_GROUP_DIM).reshape(1, GM_DIM),
        w_br_c=p["w_br_c"][l].astype(BF16), w_out=p["w_out"][l].astype(BF16),
        w_router_t=p["w_router"][l].T.astype(BF16),
        b_router_t=p["b_router"][l].reshape(N_EXPERTS, 1),
        w_sh_gu=jnp.concatenate([p["w_sh_gate"][l], p["w_sh_up"][l]], axis=-1).astype(BF16),
        w_sh_down=p["w_sh_down"][l].astype(BF16),
    )


def _rope_tables(pos):
    inv_freq = ROPE_THETA ** (-jnp.arange(0, QK_ROPE, 2, dtype=F32) / QK_ROPE)
    ang = pos.astype(F32)[:, None] * inv_freq[None, :]
    c, s = jnp.cos(ang), jnp.sin(ang)
    n = pos.shape[0]
    cos = jnp.concatenate([jnp.ones((n, QK_NOPE), F32), c, c, jnp.ones((n, HP - QK_DIM), F32)], axis=1)
    sin = jnp.concatenate([jnp.zeros((n, QK_NOPE), F32), s, s, jnp.zeros((n, HP - QK_DIM), F32)], axis=1)
    return cos, sin


def _split_mod(m):
    return [m[:, i * D_MODEL:(i + 1) * D_MODEL] for i in range(6)]


def _prompt_layer(x, mod, lw, cos, sin):
    b, t, d = x.shape
    shift_m, scale_m, gate_m, shift_f, scale_f, gate_f = [a.reshape(b, 1, d) for a in _split_mod(mod)]
    q, k, v, ckv, krp, tail, cv, bg0, part = _inproj_prompt(x, scale_m, shift_m, lw, cos, sin)
    oa = _flash_attention(q, k, v)
    x1, h2 = _merge(oa, bg0, part, x, gate_m, scale_f, shift_f, lw, TM_MERGE)
    _, idx_t, w_t = _router(h2.reshape(b * t, d), lw, TM_ROUTE)
    y = _moe_sparse(h2, idx_t, w_t, x1, gate_f, lw)
    state = (ckv, krp[:, :, QK_NOPE:QK_DIM], tail[:, 8 - (CONV_W - 1):], cv)
    return y, state


def _sample_layer(layer, x, mod, lw, cos, sin, hist, pool_c, pool_krt, page_table):
    n, d = x.shape
    shift_m, scale_m, gate_m, shift_f, scale_f, gate_f = _split_mod(mod)
    qg, ql, ckv, krp, z, cv, bg0, part = _inproj_sample(x, scale_m, shift_m, lw, cos, sin, hist[:, 0], hist[:, 1])
    head_pad = ((0, 0), (0, HEAD_ROWS - N_HEADS), (0, 0))
    qlat = jnp.pad(ql.transpose(1, 0, 2), head_pad).astype(BF16)
    qr = jnp.pad(qg[:, :, QK_NOPE:QK_DIM].transpose(1, 0, 2), head_pad).astype(BF16)
    kr = krp[:, QK_NOPE:QK_DIM]
    krnew_t = jnp.pad(kr[:, :, None], ((0, 0), (0, 0), (0, LANES - 1)))
    olat = _paged_attention(layer, page_table, lw["wkt"], qlat, qr, ckv.reshape(n, 1, KV_LORA), krnew_t,
                            pool_c, pool_krt)
    oa = _uv_project(olat.reshape(n, N_HEADS * KV_LORA), lw["wuv"])
    g3 = lambda a: a.reshape(1, n, -1)
    x1, h2 = _merge(g3(oa), g3(bg0), g3(part), g3(x), g3(gate_m), g3(scale_f), g3(shift_f), lw, n)
    gates, _, _ = _router(h2.reshape(n, d), lw, n)
    y = _moe(h2, g3(gates), x1, g3(gate_f), lw, n).reshape(n, d)
    state = (ckv.reshape(n, 1, KV_LORA), kr.reshape(n, 1, QK_ROPE),
             jnp.stack([hist[:, 1], z], axis=1), cv.reshape(n, 1, GM_DIM))
    return y, state


def kernel(x_prompt, x_sample, cache_kv_latent, cache_k_rope, state_conv, page_table, c_prompt, c_sample,
           w_ada, b_ada, g_norm_mix, g_norm_ffn, w_in, g_q_lat, w_uq, g_kv_lat, w_uk, w_uv, g_qk_q, g_qk_k,
           w_br_a, w_conv, w_br_b, g_v_ln, w_spatial, b_spatial, w_br_c, w_out, w_router, b_router,
           w_e_gate, w_e_up, w_e_down, w_sh_gate, w_sh_up, w_sh_down):
    params = dict(w_ada=w_ada, b_ada=b_ada, g_norm_mix=g_norm_mix, g_norm_ffn=g_norm_ffn, w_in=w_in,
                  g_q_lat=g_q_lat, w_uq=w_uq, g_kv_lat=g_kv_lat, w_uk=w_uk, w_uv=w_uv, g_qk_q=g_qk_q,
                  g_qk_k=g_qk_k, w_br_a=w_br_a, w_conv=w_conv, w_br_b=w_br_b, g_v_ln=g_v_ln,
                  w_spatial=w_spatial, b_spatial=b_spatial, w_br_c=w_br_c, w_out=w_out, w_router=w_router,
                  b_router=b_router, w_e_gate=w_e_gate, w_e_up=w_e_up, w_e_down=w_e_down,
                  w_sh_gate=w_sh_gate, w_sh_up=w_sh_up, w_sh_down=w_sh_down)
    depth = w_in.shape[0]
    nb, t, d = x_prompt.shape
    ns = x_sample.shape[0]
    assert x_sample.shape[1] == 1 and t % TQ == 0 and t % MOE_BLOCK == 0
    past_len = page_table.shape[1] * PAGE_SIZE

    cos_p, sin_p = _rope_tables(jnp.arange(t))
    cos_s, sin_s = _rope_tables(past_len + jnp.arange(1))
    c_all = jnp.concatenate([c_prompt, c_sample], axis=0)
    cache_krt = jnp.swapaxes(cache_k_rope, 2, 3)
    b_ada3 = b_ada.reshape(depth, 1, -1)

    hp, hs = x_prompt, x_sample.reshape(ns, d)
    st_p, st_s = [], []
    for l in range(depth):
        lw = _prep_layer(l, params)
        lw.update(layer=l, w_e_gate=w_e_gate, w_e_up=w_e_up, w_e_down=w_e_down)
        mod = _adaln(l, c_all, w_ada, b_ada3)
        hp, sp = _prompt_layer(hp, mod[:nb], lw, cos_p, sin_p)
        hs, ss = _sample_layer(l, hs, mod[nb:], lw, cos_s, sin_s, state_conv[l], cache_kv_latent, cache_krt,
                               page_table)
        st_p.append(sp)
        st_s.append(ss)

    stack = lambda sts, i: jnp.stack([s[i] for s in sts])
    return (hp, hs.reshape(ns, 1, d),
            stack(st_p, 0), stack(st_p, 1), stack(st_p, 2), stack(st_p, 3),
            stack(st_s, 0), stack(st_s, 1), stack(st_s, 2), stack(st_s, 3))
```

```python
import functools

import jax
import jax.numpy as jnp
from jax import lax
from jax.experimental import pallas as pl
from jax.experimental.pallas import tpu as pltpu

F32 = jnp.float32
BF16 = jnp.bfloat16

D_MODEL = 1024
N_HEADS = 8
QK_NOPE = 64
QK_ROPE = 32
QK_DIM = QK_NOPE + QK_ROPE
V_DIM = 64
Q_LORA = 384
KV_LORA = 256
ROPE_THETA = 10000.0
CONV_W = 3
CONV_DIM = 512
CHUNK = 128
GM_GROUPS = 8
GM_DIM = 512
GM_GROUP_DIM = GM_DIM // GM_GROUPS
N_EXPERTS = 64
TOP_K = 8
N_EXPERT_GROUPS = 8
GROUP_SIZE = N_EXPERTS // N_EXPERT_GROUPS
TOPK_GROUPS = 4
D_EXPERT = 256
ROUTED_SCALE = 2.5
N_BRANCH = 3
EPS = 1e-6
PAGE_SIZE = 128

LANES = 128
HP = LANES
VMEM_LIMIT = 56 * 1024 * 1024

S_Q = 0
S_KV = S_Q + Q_LORA
S_KRM = S_KV + KV_LORA
S_KRR = S_KRM + HP
S_CB = S_KRR + HP
S_CC = S_CB + CONV_DIM
S_CH = S_CC + CONV_DIM
S_U = S_CH + CONV_DIM
S_V = S_U + GM_DIM
S_G = S_V + GM_DIM
S_END = S_G + N_BRANCH * D_MODEL

TM_IN = 512
TQ = 512
TK = 512
FLASH_HEADS = 4
TM_MERGE = 512
TM_ROUTE = 512
TM_MOE = 1024
EXPERTS_PER_STEP = 2
MOE_BLOCK = 256
MOE_CHUNK = 16
EXP_CHUNKS = 64
EXP_BUFFERS = 3
COMBINE_BUFFERS = 3
MOE_ROWS = -(-(MOE_BLOCK * TOP_K + N_EXPERTS * (MOE_CHUNK - 1)) // 512) * 512
PAGES_PER_STEP = 64
PAGES_PER_CHUNK = 16
PAGE_BUFFERS = 3
HEAD_ROWS = 16


def _cparams(sem):
    return pltpu.CompilerParams(dimension_semantics=sem, vmem_limit_bytes=VMEM_LIMIT)


def _const_spec(shape):
    nd = len(shape)
    return pl.BlockSpec(shape, lambda *_: (0,) * nd, pipeline_mode=pl.Buffered(1))


def _dot(a, b):
    return jnp.dot(a, b, preferred_element_type=F32)


def _rms_rows(x):
    return x * lax.rsqrt(jnp.mean(x * x, axis=-1, keepdims=True) + EPS)


def _silu(x):
    return x * jax.nn.sigmoid(x)


def _adaln_kernel(c_ref, w_ref, b_ref, o_ref):
    s = _silu(c_ref[...]).astype(BF16)
    o_ref[...] = _dot(s, w_ref[...].astype(BF16)) + b_ref[...]


def _adaln(layer, c_all, w_ada, b_ada):
    n, d = c_all.shape
    e = w_ada.shape[2]
    tn = 1536
    return pl.pallas_call(
        _adaln_kernel,
        grid=(e // tn,),
        in_specs=[pl.BlockSpec((n, d), lambda j: (0, 0)),
                  pl.BlockSpec((None, d, tn), lambda j: (layer, 0, j)),
                  pl.BlockSpec((None, 1, tn), lambda j: (layer, 0, j))],
        out_specs=pl.BlockSpec((n, tn), lambda j: (0, j)),
        out_shape=jax.ShapeDtypeStruct((n, e), F32),
        compiler_params=_cparams(("arbitrary",)),
        name="adaln",
    )(c_all, w_ada, b_ada)


def _inproj_common(x_ref, sc_ref, sh_ref, gmix_ref, w2_ref):
    h = (_rms_rows(x_ref[...]) * gmix_ref[...] * (1.0 + sc_ref[...]) + sh_ref[...]).astype(BF16)

    def seg(a, b):
        return _dot(h, w2_ref[:, a:b])

    return seg


def _heads_q(seg, gq_ref, wuq_ref, gqq_ref, cos, sin):
    cqn = (_rms_rows(seg(S_Q, S_KV)) * gq_ref[...]).astype(BF16)
    q2 = _dot(cqn, wuq_ref[...])
    out = []
    for hh in range(N_HEADS):
        qm = q2[:, hh * HP:(hh + 1) * HP]
        qr = q2[:, (N_HEADS + hh) * HP:(N_HEADS + hh + 1) * HP]
        qh = qm * cos + qr * sin
        inv = lax.rsqrt(jnp.sum(qh * qh, axis=-1, keepdims=True) * (1.0 / QK_DIM) + EPS)
        out.append(qh * inv * gqq_ref[...] * (QK_DIM ** -0.5))
    return out


def _latent_kv(seg, gkv_ref, cos, sin):
    ckv = _rms_rows(seg(S_KV, S_KRM)) * gkv_ref[...]
    krp = seg(S_KRM, S_KRR) * cos + seg(S_KRR, S_CB) * sin
    return ckv, krp


def _gates_and_partial(seg, brb_in, brc_in, wbrb_ref, wbrc_ref, bg0_ref, part_ref):
    brb = _dot(brb_in.astype(BF16), wbrb_ref[...])
    brc = _dot(brc_in.astype(BF16), wbrc_ref[...])
    bg = jax.nn.sigmoid(seg(S_G, S_END))
    bg0_ref[...] = bg[:, :D_MODEL].astype(BF16)
    part_ref[...] = bg[:, D_MODEL:2 * D_MODEL] * brb + bg[:, 2 * D_MODEL:] * brc


def _layernorm_rows(v, g):
    xc = v - jnp.mean(v, axis=-1, keepdims=True)
    return xc * lax.rsqrt(jnp.mean(xc * xc, axis=-1, keepdims=True) + EPS) * g


def _inproj_prompt_kernel(x_ref, sc_ref, sh_ref, gmix_ref, w2_ref, gq_ref, wuq_ref, gqq_ref,
                          gkv_ref, wuk_ref, gqk_ref, wuv_ref, cos_ref, sin_ref, wconv_ref,
                          wbrb_ref, gvln_ref, wsp_ref, bsp_ref, wbrc_ref,
                          q_ref, k_ref, v_ref, ckv_ref, kr_ref, tail_ref, cv_ref, bg0_ref, part_ref,
                          zbuf_ref, *, tm):
    t = pl.program_id(1)
    seg = _inproj_common(x_ref, sc_ref, sh_ref, gmix_ref, w2_ref)
    cos = cos_ref[...]
    sin = sin_ref[...]

    qs = _heads_q(seg, gq_ref, wuq_ref, gqq_ref, cos, sin)
    for hh in range(N_HEADS):
        q_ref[hh] = qs[hh].astype(BF16)
    ckv, krp = _latent_kv(seg, gkv_ref, cos, sin)
    ckv_ref[...] = ckv
    kr_ref[...] = krp
    ckv_b = ckv.astype(BF16)
    k2 = _dot(ckv_b, wuk_ref[...])
    v2 = _dot(ckv_b, wuv_ref[...])
    ones_lane = jnp.where(lax.broadcasted_iota(jnp.int32, (1, HP), 1) == V_DIM, 1.0, 0.0)
    for hh in range(N_HEADS):
        kh = k2[:, hh * HP:(hh + 1) * HP] + krp
        inv = lax.rsqrt(jnp.sum(kh * kh, axis=-1, keepdims=True) * (1.0 / QK_DIM) + EPS)
        k_ref[hh] = (kh * inv * gqk_ref[...]).astype(BF16)
        v_ref[hh] = (v2[:, hh * HP:(hh + 1) * HP] + ones_lane).astype(BF16)

    gate_b = seg(S_CB, S_CC)
    z = seg(S_CC, S_CH) * seg(S_CH, S_U)

    @pl.when(t == 0)
    def _():
        zbuf_ref[0:8, :] = jnp.zeros((8, CONV_DIM), F32)

    zbuf_ref[8:8 + tm, :] = z
    z1 = zbuf_ref[7:7 + tm, :]
    z2 = zbuf_ref[6:6 + tm, :]
    wc = wconv_ref[...]
    y = wc[0:1, :] * z2 + wc[1:2, :] * z1 + wc[2:3, :] * z
    zbuf_ref[0:8, :] = z[tm - 8:tm, :]
    tail_ref[...] = z[tm - 8:tm, :]

    u = seg(S_U, S_V)
    vn = _layernorm_rows(seg(S_V, S_G), gvln_ref[...])
    cv_ref[...] = vn[tm - CHUNK:tm, :]
    vnb = vn.astype(BF16)
    rows = lax.broadcasted_iota(jnp.int32, (GM_GROUPS * CHUNK, CHUNK), 0) % CHUNK
    cols = lax.broadcasted_iota(jnp.int32, (GM_GROUPS * CHUNK, CHUNK), 1)
    wsp = jnp.where(cols <= rows, wsp_ref[...], 0.0).astype(BF16)
    lane_grp = lax.broadcasted_iota(jnp.int32, (CHUNK, GM_DIM), 1) // GM_GROUP_DIM
    sgs = []
    for c in range(tm // CHUNK):
        r = _dot(wsp, vnb[c * CHUNK:(c + 1) * CHUNK, :])
        s = bsp_ref[...]
        for g in range(GM_GROUPS):
            s = s + jnp.where(lane_grp == g, r[g * CHUNK:(g + 1) * CHUNK, :], 0.0)
        sgs.append(u[c * CHUNK:(c + 1) * CHUNK, :] * s)
    sg = jnp.concatenate(sgs, axis=0)

    _gates_and_partial(seg, gate_b * y, sg, wbrb_ref, wbrc_ref, bg0_ref, part_ref)


def _inproj_sample_kernel(x_ref, sc_ref, sh_ref, gmix_ref, w2_ref, gq_ref, wuq_ref, gqq_ref,
                          gkv_ref, gqk_ref, wukt_ref, cos_ref, sin_ref, wconv_ref, h0_ref, h1_ref,
                          wbrb_ref, gvln_ref, coef_ref, bias_ref, wbrc_ref,
                          qg_ref, ql_ref, ckv_ref, kr_ref, z_ref, cv_ref, bg0_ref, part_ref):
    seg = _inproj_common(x_ref, sc_ref, sh_ref, gmix_ref, w2_ref)
    cos = cos_ref[...]
    sin = sin_ref[...]
    qs = _heads_q(seg, gq_ref, wuq_ref, gqq_ref, cos, sin)
    for hh in range(N_HEADS):
        qg = qs[hh] * gqk_ref[...]
        qg_ref[hh] = qg
        ql_ref[hh] = _dot(qg.astype(BF16), wukt_ref[hh])
    ckv, krp = _latent_kv(seg, gkv_ref, cos, sin)
    ckv_ref[...] = ckv
    kr_ref[...] = krp

    gate_b = seg(S_CB, S_CC)
    z = seg(S_CC, S_CH) * seg(S_CH, S_U)
    wc = wconv_ref[...]
    y = wc[0:1, :] * h0_ref[...] + wc[1:2, :] * h1_ref[...] + wc[2:3, :] * z
    z_ref[...] = z

    u = seg(S_U, S_V)
    vn = _layernorm_rows(seg(S_V, S_G), gvln_ref[...])
    cv_ref[...] = vn
    sg = u * (vn * coef_ref[...] + bias_ref[...])

    _gates_and_partial(seg, gate_b * y, sg, wbrb_ref, wbrc_ref, bg0_ref, part_ref)


def _inproj_prompt(x, scale_m, shift_m, lw, cos, sin):
    b, t, d = x.shape
    tm = TM_IN
    nt = t // tm
    tok = lambda w: pl.BlockSpec((None, tm, w), lambda i, j: (i, j, 0))
    mod = pl.BlockSpec((None, 1, d), lambda i, j: (i, 0, 0))
    head = pl.BlockSpec((None, N_HEADS, tm, HP), lambda i, j: (i, 0, j, 0))
    in_specs = [
        tok(d), mod, mod, _const_spec((1, d)), _const_spec((d, S_END)),
        _const_spec((1, Q_LORA)), _const_spec((Q_LORA, 2 * N_HEADS * HP)), _const_spec((1, HP)),
        _const_spec((1, KV_LORA)), _const_spec((KV_LORA, N_HEADS * HP)), _const_spec((1, HP)),
        _const_spec((KV_LORA, N_HEADS * HP)),
        pl.BlockSpec((tm, HP), lambda i, j: (j, 0)), pl.BlockSpec((tm, HP), lambda i, j: (j, 0)),
        _const_spec((CONV_W, CONV_DIM)), _const_spec((CONV_DIM, d)), _const_spec((1, GM_DIM)),
        _const_spec((GM_GROUPS * CHUNK, CHUNK)), _const_spec((CHUNK, GM_DIM)), _const_spec((GM_DIM, d)),
    ]
    out_specs = [
        head, head, head, tok(KV_LORA), tok(HP),
        pl.BlockSpec((None, 8, CONV_DIM), lambda i, j: (i, 0, 0)),
        pl.BlockSpec((None, CHUNK, GM_DIM), lambda i, j: (i, 0, 0)),
        tok(d), tok(d),
    ]
    out_shape = [
        jax.ShapeDtypeStruct((b, N_HEADS, t, HP), BF16),
        jax.ShapeDtypeStruct((b, N_HEADS, t, HP), BF16),
        jax.ShapeDtypeStruct((b, N_HEADS, t, HP), BF16),
        jax.ShapeDtypeStruct((b, t, KV_LORA), F32),
        jax.ShapeDtypeStruct((b, t, HP), F32),
        jax.ShapeDtypeStruct((b, 8, CONV_DIM), F32),
        jax.ShapeDtypeStruct((b, CHUNK, GM_DIM), F32),
        jax.ShapeDtypeStruct((b, t, d), BF16),
        jax.ShapeDtypeStruct((b, t, d), F32),
    ]
    return pl.pallas_call(
        functools.partial(_inproj_prompt_kernel, tm=tm),
        grid=(b, nt),
        in_specs=in_specs, out_specs=out_specs, out_shape=out_shape,
        scratch_shapes=[pltpu.VMEM((tm + 8, CONV_DIM), F32)],
        compiler_params=_cparams(("arbitrary", "arbitrary")),
        name="inproj_prompt",
    )(x, scale_m, shift_m, lw["g_norm_mix"], lw["w2"], lw["g_q_lat"], lw["wuq2"], lw["g_qk_q"],
      lw["g_kv_lat"], lw["wuk"], lw["g_qk_k"], lw["wuv"], cos, sin, lw["w_conv"],
      lw["w_br_b"], lw["g_v_ln"], lw["w_sp"], lw["b_sp"], lw["w_br_c"])


def _inproj_sample(x, scale_m, shift_m, lw, cos, sin, hist0, hist1):
    n, d = x.shape
    full = lambda *s: pl.BlockSpec(s, lambda i: (0,) * len(s))
    in_specs = [
        full(n, d), full(n, d), full(n, d), full(1, d), full(d, S_END),
        full(1, Q_LORA), full(Q_LORA, 2 * N_HEADS * HP), full(1, HP),
        full(1, KV_LORA), full(1, HP), full(N_HEADS, HP, KV_LORA), full(1, HP), full(1, HP),
        full(CONV_W, CONV_DIM), full(n, CONV_DIM), full(n, CONV_DIM),
        full(CONV_DIM, d), full(1, GM_DIM), full(1, GM_DIM), full(1, GM_DIM), full(GM_DIM, d),
    ]
    out_specs = [full(N_HEADS, n, HP), full(N_HEADS, n, KV_LORA), full(n, KV_LORA), full(n, HP),
                 full(n, CONV_DIM), full(n, GM_DIM), full(n, d), full(n, d)]
    out_shape = [
        jax.ShapeDtypeStruct((N_HEADS, n, HP), F32),
        jax.ShapeDtypeStruct((N_HEADS, n, KV_LORA), F32),
        jax.ShapeDtypeStruct((n, KV_LORA), F32),
        jax.ShapeDtypeStruct((n, HP), F32),
        jax.ShapeDtypeStruct((n, CONV_DIM), F32),
        jax.ShapeDtypeStruct((n, GM_DIM), F32),
        jax.ShapeDtypeStruct((n, d), BF16),
        jax.ShapeDtypeStruct((n, d), F32),
    ]
    return pl.pallas_call(
        _inproj_sample_kernel,
        grid=(1,),
        in_specs=in_specs, out_specs=out_specs, out_shape=out_shape,
        compiler_params=_cparams(("arbitrary",)),
        name="inproj_sample",
    )(x, scale_m, shift_m, lw["g_norm_mix"], lw["w2"], lw["g_q_lat"], lw["wuq2"], lw["g_qk_q"],
      lw["g_kv_lat"], lw["g_qk_k"], lw["wukt_pad"], cos, sin, lw["w_conv"], hist0, hist1,
      lw["w_br_b"], lw["g_v_ln"], lw["sp_coef"], lw["b_sp"][0:1], lw["w_br_c"])


def _flash_kernel(q_ref, k_ref, v_ref, o_ref, *, tq, tk):
    assert tq == tk
    qi = pl.program_id(2)
    nh = q_ref.shape[0]
    qs = [q_ref[hh] for hh in range(nh)]

    def step(j, carry, masked):
        start = pl.multiple_of(j * tk, tk)
        out = []
        for hh in range(nh):
            m, acc = carry[hh]
            k = k_ref[hh, pl.ds(start, tk), :]
            v = v_ref[hh, pl.ds(start, tk), :]
            s = lax.dot_general(qs[hh], k, (((1,), (1,)), ((), ())), preferred_element_type=F32)
            if masked:
                row = lax.broadcasted_iota(jnp.int32, (tq, tk), 0)
                col = lax.broadcasted_iota(jnp.int32, (tq, tk), 1)
                s = jnp.where(col <= row, s, -jnp.inf)
            m_new = jnp.maximum(m, jnp.max(s, axis=-1, keepdims=True))
            alpha = jnp.exp(m - m_new)
            p = jnp.exp(s - m_new)
            out.append((m_new, alpha * acc + _dot(p.astype(BF16), v)))
        return tuple(out)

    m0 = jnp.full((tq, 1), -jnp.inf, F32)
    a0 = jnp.zeros((tq, HP), F32)
    carry = lax.fori_loop(0, qi, lambda j, c: step(j, c, False), tuple((m0, a0) for _ in range(nh)))
    carry = step(qi, carry, True)
    lane = lax.broadcasted_iota(jnp.int32, (tq, HP), 1)
    for hh in range(nh):
        acc = carry[hh][1]
        l = jnp.sum(jnp.where(lane == V_DIM, acc, 0.0), axis=-1, keepdims=True)
        o_ref[:, hh * HP:(hh + 1) * HP] = (acc / l).astype(BF16)


def _flash_attention(q, k, v):
    b, h, t, _ = q.shape
    tq, tk = TQ, TK
    nh = FLASH_HEADS
    return pl.pallas_call(
        functools.partial(_flash_kernel, tq=tq, tk=tk),
        grid=(b, h // nh, t // tq),
        in_specs=[pl.BlockSpec((None, nh, tq, HP), lambda i, j, n: (i, j, n, 0)),
                  pl.BlockSpec((None, nh, t, HP), lambda i, j, n: (i, j, 0, 0)),
                  pl.BlockSpec((None, nh, t, HP), lambda i, j, n: (i, j, 0, 0))],
        out_specs=pl.BlockSpec((None, tq, nh * HP), lambda i, j, n: (i, n, j)),
        out_shape=jax.ShapeDtypeStruct((b, t, h * HP), BF16),
        compiler_params=_cparams(("arbitrary", "arbitrary", "arbitrary")),
        name="flash_prompt",
    )(q, k, v)


def _paged_kernel(pt_ref, wkt_ref, ql_ref, qr_ref, cnew_ref, krnew_ref, poolc_ref, poolk_ref, o_ref,
                  cbuf, kbuf, csem, ksem, lhs_ref, m_ref, l_ref, acc_ref, *, layer, n_pages, n_pg, n_chunk):
    i = pl.program_id(0)
    j = pl.program_id(1)
    nj = pl.num_programs(1)
    nk = N_HEADS * QK_NOPE
    step = i * nj + j
    last_step = pl.num_programs(0) * nj - 1
    ahead = PAGE_BUFFERS - 1
    slot = step % PAGE_BUFFERS
    nxt_slot = (step + ahead) % PAGE_BUFFERS

    def step_base(t):
        t = jnp.minimum(t, last_step)
        return (t // nj) * n_pages + (t % nj) * n_pg

    nxt_base = step_base(step + ahead)

    def page_copies(base, buf_slot, p):
        pid = pt_ref[base + p]
        return (pltpu.make_async_copy(poolc_ref.at[layer, pid], cbuf.at[buf_slot, p], csem.at[buf_slot]),
                pltpu.make_async_copy(poolk_ref.at[layer, pid], kbuf.at[buf_slot, p], ksem.at[buf_slot]))

    def start_page(base, buf_slot, p):
        for cp in page_copies(base, buf_slot, p):
            cp.start()

    @pl.when(step == 0)
    def _():
        for a in range(ahead):
            for p in range(n_pg):
                start_page(step_base(a), a, p)

    for p in range(n_pg):
        for cp in page_copies(i * n_pages + j * n_pg, slot, p):
            cp.wait()

    def at_step(cond):
        return (lambda f: f()) if n_pg == n_pages else pl.when(cond)

    @at_step(j == 0)
    def _():
        lhs_ref[0:nk, :] = wkt_ref[...]
        lhs_ref[nk:nk + HEAD_ROWS, :] = ql_ref[...]
        m_ref[...] = jnp.full(m_ref.shape, -jnp.inf, F32)
        l_ref[...] = jnp.zeros(l_ref.shape, F32)
        acc_ref[...] = jnp.zeros(acc_ref.shape, F32)

    qr = qr_ref[...]

    def update(cb, krt, valid_keys):
        kx = lax.dot_general(lhs_ref[...], cb, (((1,), (1,)), ((), ())), preferred_element_type=F32)
        sq = kx[:nk, :] * kx[:nk, :]
        ssq = jnp.sum(sq.reshape(QK_NOPE, N_HEADS, sq.shape[-1]), axis=0)
        ssq = ssq + jnp.sum(krt * krt, axis=0, keepdims=True)
        inv = lax.rsqrt(ssq * (1.0 / QK_DIM) + EPS)
        st = (kx[nk:, :] + _dot(qr, krt.astype(BF16))) * jnp.concatenate([inv, inv], axis=0)
        if valid_keys is not None:
            kcol = lax.broadcasted_iota(jnp.int32, st.shape, 1)
            st = jnp.where(kcol < valid_keys, st, -jnp.inf)
        m_old = m_ref[...]
        m_new = jnp.maximum(m_old, jnp.max(st, axis=-1, keepdims=True))
        alpha = jnp.exp(m_old - m_new)
        p = jnp.exp(st - m_new)
        l_ref[...] = alpha * l_ref[...] + jnp.sum(p, axis=-1, keepdims=True)
        acc_ref[...] = alpha * acc_ref[...] + _dot(p.astype(BF16), cb)
        m_ref[...] = m_new

    per = n_pg // n_chunk
    for ch in range(n_chunk):
        cb = cbuf[slot, ch * per:(ch + 1) * per].reshape(per * PAGE_SIZE, KV_LORA).astype(BF16)
        krt = jnp.concatenate([kbuf[slot, p] for p in range(ch * per, (ch + 1) * per)], axis=1)
        update(cb, krt, None)
        for p in range(ch * per, (ch + 1) * per):
            start_page(nxt_base, nxt_slot, p)

    @at_step(j == nj - 1)
    def _():
        first = lax.broadcasted_iota(jnp.int32, (LANES, KV_LORA), 0) == 0
        cn = jnp.where(first, jnp.broadcast_to(cnew_ref[...], (LANES, KV_LORA)), 0.0).astype(BF16)
        update(cn, krnew_ref[...], 1)
        o_ref[...] = (acc_ref[...] / l_ref[...])[:N_HEADS, :]

    @pl.when(step == last_step)
    def _():
        for a in range(1, PAGE_BUFFERS):
            for p in range(n_pg):
                for cp in page_copies(nxt_base, (step + a) % PAGE_BUFFERS, p):
                    cp.wait()


def _paged_attention(layer, page_table, wkt, qlat, qr, cnew, krnew_t, pool_c, pool_krt):
    s, n_pages = page_table.shape
    n_pg = min(PAGES_PER_STEP, n_pages)
    assert n_pages % n_pg == 0 and n_pg % PAGES_PER_CHUNK == 0
    nj = n_pages // n_pg
    pt = page_table.reshape(-1)
    nk = N_HEADS * QK_NOPE

    per_seq = lambda *shape: pl.BlockSpec((None,) + shape, lambda i, j, pt_ref: (i,) + (0,) * len(shape))
    const = lambda *shape: pl.BlockSpec(shape, lambda i, j, pt_ref: (0,) * len(shape))
    in_specs = [const(nk, KV_LORA), per_seq(HEAD_ROWS, KV_LORA), per_seq(HEAD_ROWS, QK_ROPE),
                per_seq(1, KV_LORA), per_seq(QK_ROPE, LANES),
                pl.BlockSpec(memory_space=pl.ANY), pl.BlockSpec(memory_space=pl.ANY)]
    grid_spec = pltpu.PrefetchScalarGridSpec(
        num_scalar_prefetch=1,
        grid=(s, nj),
        in_specs=in_specs,
        out_specs=per_seq(N_HEADS, KV_LORA),
        scratch_shapes=[pltpu.VMEM((PAGE_BUFFERS, n_pg, PAGE_SIZE, KV_LORA), F32),
                        pltpu.VMEM((PAGE_BUFFERS, n_pg, QK_ROPE, PAGE_SIZE), F32),
                        pltpu.SemaphoreType.DMA((PAGE_BUFFERS,)), pltpu.SemaphoreType.DMA((PAGE_BUFFERS,)),
                        pltpu.VMEM((nk + HEAD_ROWS, KV_LORA), BF16), pltpu.VMEM((HEAD_ROWS, 1), F32),
                        pltpu.VMEM((HEAD_ROWS, 1), F32), pltpu.VMEM((HEAD_ROWS, KV_LORA), F32)],
    )
    return pl.pallas_call(
        functools.partial(_paged_kernel, layer=layer, n_pages=n_pages, n_pg=n_pg, n_chunk=n_pg // PAGES_PER_CHUNK),
        grid_spec=grid_spec,
        out_shape=jax.ShapeDtypeStruct((s, N_HEADS, KV_LORA), F32),
        compiler_params=_cparams(("arbitrary", "arbitrary")),
        name="paged_sample",
    )(pt, wkt, qlat, qr, cnew, krnew_t, pool_c, pool_krt)


def _uv_kernel(ol_ref, wuv_ref, o_ref):
    for hh in range(N_HEADS):
        ol = ol_ref[:, hh * KV_LORA:(hh + 1) * KV_LORA].astype(BF16)
        o_ref[:, hh * HP:(hh + 1) * HP] = _dot(ol, wuv_ref[:, hh * HP:(hh + 1) * HP]).astype(BF16)


def _uv_project(olat, wuv):
    n = olat.shape[0]
    full = lambda *s: pl.BlockSpec(s, lambda i: (0,) * len(s))
    return pl.pallas_call(
        _uv_kernel, grid=(1,),
        in_specs=[full(n, N_HEADS * KV_LORA), full(KV_LORA, N_HEADS * HP)],
        out_specs=full(n, N_HEADS * HP),
        out_shape=jax.ShapeDtypeStruct((n, N_HEADS * HP), BF16),
        compiler_params=_cparams(("arbitrary",)),
        name="uv_sample",
    )(olat, wuv)


def _merge_kernel(oa_ref, wbra_ref, bg0_ref, part_ref, x_ref, gm_ref, wout_ref, gffn_ref, scf_ref, shf_ref,
                  x1_ref, h2_ref):
    bra = _dot(oa_ref[...], wbra_ref[...])
    merged = bg0_ref[...].astype(F32) * bra + part_ref[...]
    x1 = x_ref[...] + gm_ref[...] * _dot(merged.astype(BF16), wout_ref[...])
    x1_ref[...] = x1
    h2_ref[...] = (_rms_rows(x1) * gffn_ref[...] * (1.0 + scf_ref[...]) + shf_ref[...]).astype(BF16)


def _merge(oa, bg0, part, x, gate_m, scale_f, shift_f, lw, tm):
    g, t, d = x.shape
    mt = gate_m.shape[1]
    tok = lambda w: pl.BlockSpec((None, tm, w), lambda i, j: (i, j, 0))
    if mt == 1:
        mod = pl.BlockSpec((None, 1, d), lambda i, j: (i, 0, 0))
    else:
        mod = pl.BlockSpec((None, tm, d), lambda i, j: (i, j, 0))
    return pl.pallas_call(
        _merge_kernel,
        grid=(g, t // tm),
        in_specs=[tok(N_HEADS * HP), _const_spec((N_HEADS * HP, d)), tok(d), tok(d), tok(d), mod,
                  _const_spec((d, d)), _const_spec((1, d)), mod, mod],
        out_specs=[tok(d), tok(d)],
        out_shape=[jax.ShapeDtypeStruct((g, t, d), F32), jax.ShapeDtypeStruct((g, t, d), BF16)],
        compiler_params=_cparams(("arbitrary", "arbitrary")),
        name="merge",
    )(oa, lw["w_br_a"], bg0, part, x, gate_m, lw["w_out"], lw["g_norm_ffn"], scale_f, shift_f)


def _router_kernel(h_ref, wrt_ref, brt_ref, g_ref, idx_ref, w_ref):
    tm = h_ref.shape[0]
    logits = lax.dot_general(wrt_ref[...], h_ref[...], (((1,), (1,)), ((), ())), preferred_element_type=F32)
    scores = jax.nn.sigmoid(logits)
    choice = scores + brt_ref[...]
    neg = -jnp.inf

    gsc = []
    for g in range(N_EXPERT_GROUPS):
        xg = choice[g * GROUP_SIZE:(g + 1) * GROUP_SIZE, :]
        m1 = jnp.max(xg, axis=0, keepdims=True)
        cnt = jnp.sum(jnp.where(xg == m1, 1.0, 0.0), axis=0, keepdims=True)
        m2 = jnp.max(jnp.where(xg < m1, xg, neg), axis=0, keepdims=True)
        gsc.append(m1 + jnp.where(cnt >= 2.0, m1, m2))

    cands = []
    for g in range(N_EXPERT_GROUPS):
        rank = jnp.zeros((1, tm), F32)
        for o in range(N_EXPERT_GROUPS):
            if o == g:
                continue
            beats = (gsc[o] > gsc[g]) if o > g else (gsc[o] >= gsc[g])
            rank = rank + jnp.where(beats, 1.0, 0.0)
        keep = rank < float(TOPK_GROUPS)
        cands.append(jnp.where(keep, choice[g * GROUP_SIZE:(g + 1) * GROUP_SIZE, :], neg))
    cand = jnp.concatenate(cands, axis=0)

    ridx = lax.broadcasted_iota(jnp.int32, (N_EXPERTS, tm), 0)
    picked = jnp.zeros((N_EXPERTS, tm), F32)
    idxs, ws = [], []
    for _ in range(TOP_K):
        m = jnp.max(cand, axis=0, keepdims=True)
        idx = jnp.min(jnp.where(cand == m, ridx, N_EXPERTS), axis=0, keepdims=True)
        hit = ridx == idx
        picked = jnp.where(hit, 1.0, picked)
        cand = jnp.where(hit, neg, cand)
        idxs.append(idx)
        ws.append(jnp.sum(jnp.where(hit, scores, 0.0), axis=0, keepdims=True))

    w = jnp.where(picked > 0.0, scores, 0.0)
    norm = ROUTED_SCALE / jnp.sum(w, axis=0, keepdims=True)
    gt = w * norm
    g_ref[...] = jnp.concatenate([gt, jnp.zeros((LANES - N_EXPERTS, tm), F32)], axis=0).T
    idx_ref[...] = jnp.concatenate(idxs, axis=0)
    w_ref[...] = jnp.concatenate(ws, axis=0) * norm


def _router(h2, lw, tm):
    m, d = h2.shape
    return pl.pallas_call(
        _router_kernel,
        grid=(m // tm,),
        in_specs=[pl.BlockSpec((tm, d), lambda i: (i, 0)), _const_spec((N_EXPERTS, d)),
                  _const_spec((N_EXPERTS, 1))],
        out_specs=[pl.BlockSpec((tm, LANES), lambda i: (i, 0)), pl.BlockSpec((TOP_K, tm), lambda i: (0, i)),
                   pl.BlockSpec((TOP_K, tm), lambda i: (0, i))],
        out_shape=[jax.ShapeDtypeStruct((m, LANES), F32), jax.ShapeDtypeStruct((TOP_K, m), jnp.int32),
                   jax.ShapeDtypeStruct((TOP_K, m), F32)],
        compiler_params=_cparams(("arbitrary",)),
        name="router",
    )(h2, lw["w_router_t"], lw["b_router_t"])


def _swiglu(x, wgu):
    gu = _dot(x, wgu)
    return _silu(gu[:, :D_EXPERT]) * gu[:, D_EXPERT:]


def _expert_ffn(x, wg, wu, wd):
    h = _silu(_dot(x, wg.astype(BF16))) * _dot(x, wu.astype(BF16))
    return h, wd.astype(BF16)


def _moe_kernel(h_ref, g_ref, wg_ref, wu_ref, wd_ref, wsgu_ref, wsd_ref, x1_ref, gf_ref, o_ref, acc_ref, *, epb):
    e = pl.program_id(2)
    h = h_ref[...]
    tm = h.shape[0]

    @pl.when(e == 0)
    def _():
        acc_ref[...] = _dot(_swiglu(h, wsgu_ref[...]).astype(BF16), wsd_ref[...])

    lane = lax.broadcasted_iota(jnp.int32, (tm, LANES), 1)
    gates = g_ref[...]
    for jj in range(epb):
        gcol = jnp.sum(jnp.where(lane == e * epb + jj, gates, 0.0), axis=-1, keepdims=True)
        hh, wd = _expert_ffn(h, wg_ref[jj], wu_ref[jj], wd_ref[jj])
        acc_ref[...] += _dot((hh * gcol).astype(BF16), wd)

    @pl.when(e == pl.num_programs(2) - 1)
    def _():
        o_ref[...] = x1_ref[...] + gf_ref[...] * acc_ref[...]


def _moe(h2, gates, x1, gate_f, lw, tm):
    g, t, d = x1.shape
    mt = gate_f.shape[1]
    epb = EXPERTS_PER_STEP
    layer = lw["layer"]
    tok = lambda w: pl.BlockSpec((None, tm, w), lambda i, j, e: (i, j, 0))
    if mt == 1:
        mod = pl.BlockSpec((None, 1, d), lambda i, j, e: (i, 0, 0))
    else:
        mod = pl.BlockSpec((None, tm, d), lambda i, j, e: (i, j, 0))
    return pl.pallas_call(
        functools.partial(_moe_kernel, epb=epb),
        grid=(g, t // tm, N_EXPERTS // epb),
        in_specs=[tok(d), tok(LANES),
                  pl.BlockSpec((None, epb, d, D_EXPERT), lambda i, j, e: (layer, e, 0, 0)),
                  pl.BlockSpec((None, epb, d, D_EXPERT), lambda i, j, e: (layer, e, 0, 0)),
                  pl.BlockSpec((None, epb, D_EXPERT, d), lambda i, j, e: (layer, e, 0, 0)),
                  _const_spec((d, 2 * D_EXPERT)), _const_spec((D_EXPERT, d)),
                  tok(d), mod],
        out_specs=tok(d),
        out_shape=jax.ShapeDtypeStruct((g, t, d), F32),
        scratch_shapes=[pltpu.VMEM((tm, d), F32)],
        compiler_params=_cparams(("arbitrary", "arbitrary", "arbitrary")),
        name="moe",
    )(h2, gates, lw["w_e_gate"], lw["w_e_up"], lw["w_e_down"], lw["w_sh_gu"], lw["w_sh_down"], x1, gate_f)


def _dispatch_kernel(x_ref, idx_ref, xs_ref, posk_ref, nch_ref, off_ref, *, rc):
    x = x_ref[...]
    idx = idx_ref[...]
    nb = x.shape[0]

    eiota = lax.broadcasted_iota(jnp.int32, (N_EXPERTS, nb), 0)
    hits = [eiota == idx[k:k + 1, :] for k in range(TOP_K)]
    sel = jnp.zeros((N_EXPERTS, nb), F32)
    for hit in hits:
        sel = sel + jnp.where(hit, 1.0, 0.0)
    cnt = jnp.sum(sel, axis=1, keepdims=True)
    nch = jnp.floor((cnt + (MOE_CHUNK - 1.0)) * (1.0 / MOE_CHUNK))
    nch_b = jnp.broadcast_to(nch, (N_EXPERTS, LANES))
    earlier = (lax.broadcasted_iota(jnp.int32, (N_EXPERTS, N_EXPERTS), 1)
               < lax.broadcasted_iota(jnp.int32, (N_EXPERTS, N_EXPERTS), 0))
    off_b = _dot(jnp.where(earlier, 1.0, 0.0).astype(BF16), nch_b.astype(BF16))
    before = (lax.broadcasted_iota(jnp.int32, (nb, nb), 0) < lax.broadcasted_iota(jnp.int32, (nb, nb), 1))
    rank = _dot(sel.astype(BF16), jnp.where(before, 1.0, 0.0).astype(BF16))
    pos = off_b[:, 0:1] * float(MOE_CHUNK) + rank
    posk = jnp.concatenate([jnp.sum(jnp.where(hit, pos, 0.0), axis=0, keepdims=True) for hit in hits],
                           axis=0).astype(jnp.int32)
    posk_ref[...] = posk
    nch_ref[...] = nch_b.astype(jnp.int32)
    off_ref[...] = off_b.astype(jnp.int32)

    posk16 = posk.astype(jnp.int16)
    for c in range(MOE_ROWS // rc):
        riota = (c * rc + lax.broadcasted_iota(jnp.int32, (rc, nb), 0)).astype(jnp.int16)
        p = jnp.zeros((rc, nb), BF16)
        for k in range(TOP_K):
            p = jnp.where(riota == posk16[k:k + 1, :], jnp.ones((), BF16), p)
        xs_ref[c * rc:(c + 1) * rc, :] = _dot(p, x).astype(BF16)


def _dispatch(h2, idx_t):
    nblk, nb, d = h2.shape
    small = pl.BlockSpec((None, N_EXPERTS, LANES), lambda i: (i, 0, 0))
    return pl.pallas_call(
        functools.partial(_dispatch_kernel, rc=512),
        grid=(nblk,),
        in_specs=[pl.BlockSpec((None, nb, d), lambda i: (i, 0, 0)),
                  pl.BlockSpec((TOP_K, nb), lambda i: (0, i))],
        out_specs=[pl.BlockSpec((None, MOE_ROWS, d), lambda i: (i, 0, 0)),
                   pl.BlockSpec((None, TOP_K, nb), lambda i: (i, 0, 0)), small, small],
        out_shape=[jax.ShapeDtypeStruct((nblk, MOE_ROWS, d), BF16),
                   jax.ShapeDtypeStruct((nblk, TOP_K, nb), jnp.int32),
                   jax.ShapeDtypeStruct((nblk, N_EXPERTS, LANES), jnp.int32),
                   jax.ShapeDtypeStruct((nblk, N_EXPERTS, LANES), jnp.int32)],
        compiler_params=_cparams(("arbitrary",)),
        name="moe_dispatch",
    )(h2, idx_t)


def _start_all(copies):
    for cp in copies:
        cp.start()


def _expert_kernel(te_ref, src_ref, nact_ref, xs_ref, wg_ref, wu_ref, wd_ref, ys_ref, xbuf, sem):
    t = pl.program_id(0)
    nact = nact_ref[0]
    ahead = EXP_BUFFERS - 1
    slot = t % EXP_BUFFERS

    def copies(tile, buf_slot):
        return [pltpu.make_async_copy(xs_ref.at[src_ref[tile * EXP_CHUNKS + c]], xbuf.at[buf_slot, c],
                                      sem.at[buf_slot]) for c in range(EXP_CHUNKS)]

    @pl.when(t == 0)
    def _():
        for a in range(ahead):
            @pl.when(a < nact)
            def _():
                _start_all(copies(a, a))

    @pl.when(t + ahead < nact)
    def _():
        _start_all(copies(t + ahead, (t + ahead) % EXP_BUFFERS))

    @pl.when(t < nact)
    def _():
        for cp in copies(t, slot):
            cp.wait()
        x = xbuf[slot].reshape(EXP_CHUNKS * MOE_CHUNK, xbuf.shape[-1])
        hh, wd = _expert_ffn(x, wg_ref[...], wu_ref[...], wd_ref[...])
        ys_ref[...] = _dot(hh.astype(BF16), wd).astype(BF16)

    @pl.when(t >= nact)
    def _():
        ys_ref[...] = jnp.zeros(ys_ref.shape, BF16)


def _experts(xs_chunks, tile_expert, src_chunk, n_active, lw):
    d = xs_chunks.shape[-1]
    n_tiles = tile_expert.shape[0]
    tm = EXP_CHUNKS * MOE_CHUNK
    layer = lw["layer"]
    grid_spec = pltpu.PrefetchScalarGridSpec(
        num_scalar_prefetch=3,
        grid=(n_tiles,),
        in_specs=[pl.BlockSpec(memory_space=pl.ANY),
                  pl.BlockSpec((None, None, d, D_EXPERT), lambda t, te, src, na: (layer, te[t], 0, 0)),
                  pl.BlockSpec((None, None, d, D_EXPERT), lambda t, te, src, na: (layer, te[t], 0, 0)),
                  pl.BlockSpec((None, None, D_EXPERT, d), lambda t, te, src, na: (layer, te[t], 0, 0))],
        out_specs=pl.BlockSpec((tm, d), lambda t, te, src, na: (t, 0)),
        scratch_shapes=[pltpu.VMEM((EXP_BUFFERS, EXP_CHUNKS, MOE_CHUNK, d), BF16),
                        pltpu.SemaphoreType.DMA((EXP_BUFFERS,))],
    )
    return pl.pallas_call(
        _expert_kernel,
        grid_spec=grid_spec,
        out_shape=jax.ShapeDtypeStruct((n_tiles * tm, d), BF16),
        compiler_params=_cparams(("arbitrary",)),
        name="moe_experts",
    )(tile_expert, src_chunk, n_active, xs_chunks, lw["w_e_gate"], lw["w_e_up"], lw["w_e_down"])


def _combine_kernel(src_ref, ys_ref, pos_ref, w_ref, h_ref, x1_ref, gf_ref, wsgu_ref, wsd_ref,
                    o_ref, ybuf, sem, *, cc):
    b = pl.program_id(0)
    last = pl.num_programs(0) - 1
    ahead = COMBINE_BUFFERS - 1
    slot = b % COMBINE_BUFFERS
    nxt_slot = (b + ahead) % COMBINE_BUFFERS
    nxt = jnp.minimum(b + ahead, last)
    nch = MOE_ROWS // MOE_CHUNK

    def copies(blk, buf_slot, lo=0, hi=nch):
        return [pltpu.make_async_copy(ys_ref.at[src_ref[blk * nch + c]], ybuf.at[buf_slot, c], sem.at[buf_slot])
                for c in range(lo, hi)]

    @pl.when(b == 0)
    def _():
        for a in range(ahead):
            _start_all(copies(jnp.minimum(a, last), a))

    acc = _dot(_swiglu(h_ref[...], wsgu_ref[...]).astype(BF16), wsd_ref[...])

    for cp in copies(b, slot):
        cp.wait()
    pos = pos_ref[...].astype(jnp.int16)
    w = w_ref[...].astype(BF16)
    nb = pos.shape[0]
    per = cc // MOE_CHUNK
    for c in range(MOE_ROWS // cc):
        liota = (c * cc + lax.broadcasted_iota(jnp.int32, (nb, cc), 1)).astype(jnp.int16)
        pw = jnp.zeros((nb, cc), BF16)
        for k in range(TOP_K):
            pw = jnp.where(liota == pos[:, k:k + 1], w[:, k:k + 1], pw)
        y = ybuf[slot, c * per:(c + 1) * per].reshape(cc, ybuf.shape[-1])
        acc = acc + _dot(pw, y)
        _start_all(copies(nxt, nxt_slot, c * per, (c + 1) * per))
    o_ref[...] = x1_ref[...] + gf_ref[...] * acc

    @pl.when(b == last)
    def _():
        for a in range(1, COMBINE_BUFFERS):
            for cp in copies(nxt, (b + a) % COMBINE_BUFFERS):
                cp.wait()


def _combine(ys_chunks, src_chunk, posk, wk, h2, x1, gate_f, lw, blocks_per_seq):
    nblk, nb, d = h2.shape
    nch = MOE_ROWS // MOE_CHUNK
    blk = lambda w: pl.BlockSpec((None, nb, w), lambda i, src: (i, 0, 0))
    const = lambda *s: pl.BlockSpec(s, lambda i, src: (0,) * len(s))
    grid_spec = pltpu.PrefetchScalarGridSpec(
        num_scalar_prefetch=1,
        grid=(nblk,),
        in_specs=[pl.BlockSpec(memory_space=pl.ANY), blk(TOP_K), blk(TOP_K), blk(d), blk(d),
                  pl.BlockSpec((None, 1, d), lambda i, src: (i // blocks_per_seq, 0, 0)),
                  const(d, 2 * D_EXPERT), const(D_EXPERT, d)],
        out_specs=blk(d),
        scratch_shapes=[pltpu.VMEM((COMBINE_BUFFERS, nch, MOE_CHUNK, d), BF16),
                        pltpu.SemaphoreType.DMA((COMBINE_BUFFERS,))],
    )
    return pl.pallas_call(
        functools.partial(_combine_kernel, cc=512),
        grid_spec=grid_spec,
        out_shape=jax.ShapeDtypeStruct((nblk, nb, d), F32),
        compiler_params=_cparams(("arbitrary",)),
        name="moe_combine",
    )(src_chunk, ys_chunks, posk, wk, h2, x1, gate_f, lw["w_sh_gu"], lw["w_sh_down"])


def _excl_cumsum(x, axis):
    n = x.shape[axis]
    earlier = jnp.arange(n)[:, None] > jnp.arange(n)[None, :]
    xm = jnp.moveaxis(x, axis, -1)
    out = jnp.sum(jnp.where(earlier, xm[..., None, :], 0), axis=-1)
    return jnp.moveaxis(out, -1, axis)


def _pick(onehot, table):
    return jnp.sum(jnp.where(onehot, table, 0), axis=-1)


def _moe_plan(nch, off, n_tokens):
    nblk = nch.shape[0]
    i32 = jnp.int32
    cpb = MOE_ROWS // MOE_CHUNK
    experts = jnp.arange(N_EXPERTS, dtype=i32)
    tot = jnp.sum(nch, axis=1)
    ech = jnp.sum(nch, axis=0)
    tiles_e = (ech + EXP_CHUNKS - 1) // EXP_CHUNKS
    tile_start = _excl_cumsum(tiles_e, 0)
    tile_end = tile_start + tiles_e
    n_active = jnp.sum(tiles_e)
    cumb = _excl_cumsum(nch, 0)

    n_tiles = _max_expert_tiles(n_tokens, nblk)
    tid = jnp.arange(n_tiles, dtype=i32)
    te = jnp.sum((tid[:, None] >= tile_end[None, :]).astype(i32), axis=1)
    te_last = jnp.sum((n_active - 1 >= tile_end).astype(i32))
    te = jnp.where(tid < n_active, te, te_last)
    is_e = te[:, None] == experts[None, :]
    ts_t = _pick(is_e, tile_start[None, :])
    ech_t = _pick(is_e, ech[None, :])
    ends_t = _pick(is_e[:, None, :], (cumb + nch)[None, :, :])
    off_t = _pick(is_e[:, None, :], off[None, :, :])
    cumb_t = _pick(is_e[:, None, :], cumb[None, :, :])
    c = (tid - ts_t)[:, None] * EXP_CHUNKS + jnp.arange(EXP_CHUNKS, dtype=i32)[None, :]
    valid = jnp.logical_and(c < ech_t[:, None], (tid < n_active)[:, None])
    sb = jnp.minimum(jnp.sum((c[:, :, None] >= ends_t[:, None, :]).astype(i32), axis=2), nblk - 1)
    is_b = sb[:, :, None] == jnp.arange(nblk, dtype=i32)[None, None, :]
    src_e = jnp.where(valid, sb * cpb + _pick(is_b, off_t[:, None, :]) + c - _pick(is_b, cumb_t[:, None, :]), 0)

    j = jnp.arange(cpb, dtype=i32)
    ce = jnp.minimum(jnp.sum((j[None, :, None] >= (off + nch)[:, None, :]).astype(i32), axis=2), N_EXPERTS - 1)
    is_ce = ce[:, :, None] == experts[None, None, :]
    g = (_pick(is_ce, tile_start[None, None, :]) * EXP_CHUNKS + _pick(is_ce, cumb[:, None, :])
         + j[None, :] - _pick(is_ce, off[:, None, :]))
    src_c = jnp.where(j[None, :] < tot[:, None], g, 0)
    return te, src_e.reshape(-1), n_active.reshape(1), src_c.reshape(-1)


def _max_expert_tiles(t, nblk):
    max_chunks = t * TOP_K // MOE_CHUNK + nblk * N_EXPERTS
    return max_chunks // EXP_CHUNKS + N_EXPERTS


def _moe_sparse(h2, idx_t, w_t, x1, gate_f, lw):
    b, t, d = x1.shape
    nblk = b * t // MOE_BLOCK
    h2b = h2.reshape(nblk, MOE_BLOCK, d)
    xs, posk_t, nch, off = _dispatch(h2b, idx_t)
    te, src_e, n_active, src_c = _moe_plan(nch[:, :, 0], off[:, :, 0], b * t)
    ys = _experts(xs.reshape(nblk * MOE_ROWS // MOE_CHUNK, MOE_CHUNK, d), te, src_e, n_active, lw)
    wk = w_t.reshape(TOP_K, nblk, MOE_BLOCK).transpose(1, 2, 0)
    out = _combine(ys.reshape(-1, MOE_CHUNK, d), src_c, posk_t.transpose(0, 2, 1), wk,
                   h2b, x1.reshape(nblk, MOE_BLOCK, d), gate_f, lw, t // MOE_BLOCK)
    return out.reshape(b, t, d)


def _pad_heads(w, width):
    pad = [(0, 0)] * (w.ndim - 1) + [(0, HP - width)]
    wp = jnp.pad(w, pad)
    return wp.reshape(w.shape[:-2] + (w.shape[-2] * HP,))


def _rot_cols(w):
    half = QK_ROPE // 2
    return jnp.concatenate([-w[..., half:], w[..., :half]], axis=-1)


def _prep_layer(l, p):
    d = D_MODEL
    w_in = p["w_in"][l]
    off_kv = Q_LORA
    off_kr = off_kv + KV_LORA
    off_cb = off_kr + QK_ROPE
    off_g = off_cb + 3 * CONV_DIM + 2 * GM_DIM
    wkr = w_in[:, off_kr:off_cb]
    lead = jnp.zeros((d, QK_NOPE), F32)
    trail = jnp.zeros((d, HP - QK_DIM), F32)
    w2 = jnp.concatenate([
        w_in[:, :off_kr],
        lead, wkr, trail,
        lead, _rot_cols(wkr), trail,
        w_in[:, off_cb:],
    ], axis=1).astype(BF16)
    assert w2.shape[1] == S_END and off_g + N_BRANCH * d == w_in.shape[1]

    wuq = p["w_uq"][l]
    wuq_rot = jnp.concatenate([jnp.zeros_like(wuq[..., :QK_NOPE]), _rot_cols(wuq[..., QK_NOPE:])], axis=-1)
    wuq2 = jnp.concatenate([_pad_heads(wuq, QK_DIM), _pad_heads(wuq_rot, QK_DIM)], axis=-1).astype(BF16)

    def pad_gain(gq):
        return jnp.pad(gq, (0, HP - QK_DIM)).reshape(1, HP)

    w_br_a = jnp.pad(p["w_br_a"][l].reshape(N_HEADS, V_DIM, d), ((0, 0), (0, HP - V_DIM), (0, 0)))
    w_sp = p["w_spatial"][l]
    b_sp = jnp.repeat(p["b_spatial"][l].T, GM_GROUP_DIM, axis=1)
    return dict(
        g_norm_mix=p["g_norm_mix"][l].reshape(1, d), g_norm_ffn=p["g_norm_ffn"][l].reshape(1, d),
        w2=w2, g_q_lat=p["g_q_lat"][l].reshape(1, Q_LORA), wuq2=wuq2,
        g_qk_q=pad_gain(p["g_qk_q"][l]), g_qk_k=pad_gain(p["g_qk_k"][l]),
        g_kv_lat=p["g_kv_lat"][l].reshape(1, KV_LORA),
        wuk=_pad_heads(p["w_uk"][l], QK_NOPE).astype(BF16),
        wkt=p["w_uk"][l].transpose(2, 1, 0).reshape(QK_NOPE * N_HEADS, KV_LORA).astype(BF16),
        wukt_pad=jnp.pad(p["w_uk"][l].transpose(1, 2, 0), ((0, 0), (0, HP - QK_NOPE), (0, 0))).astype(BF16),
        wuv=_pad_heads(p["w_uv"][l], V_DIM).astype(BF16),
        w_br_a=w_br_a.reshape(N_HEADS * HP, d).astype(BF16),
        w_conv=p["w_conv"][l], w_br_b=p["w_br_b"][l].astype(BF16),
        g_v_ln=p["g_v_ln"][l].reshape(1, GM_DIM),
        w_sp=w_sp.reshape(GM_GROUPS * CHUNK, CHUNK), b_sp=b_sp,
        sp_coef=jnp.repeat(w_sp[:, 0, 0], GM_GROUP_DIM).reshape(1, GM_DIM),
        w_br_c=p["w_br_c"][l].astype(BF16), w_out=p["w_out"][l].astype(BF16),
        w_router_t=p["w_router"][l].T.astype(BF16),
        b_router_t=p["b_router"][l].reshape(N_EXPERTS, 1),
        w_sh_gu=jnp.concatenate([p["w_sh_gate"][l], p["w_sh_up"][l]], axis=-1).astype(BF16),
        w_sh_down=p["w_sh_down"][l].astype(BF16),
    )


def _rope_tables(pos):
    inv_freq = ROPE_THETA ** (-jnp.arange(0, QK_ROPE, 2, dtype=F32) / QK_ROPE)
    ang = pos.astype(F32)[:, None] * inv_freq[None, :]
    c, s = jnp.cos(ang), jnp.sin(ang)
    n = pos.shape[0]
    cos = jnp.concatenate([jnp.ones((n, QK_NOPE), F32), c, c, jnp.ones((n, HP - QK_DIM), F32)], axis=1)
    sin = jnp.concatenate([jnp.zeros((n, QK_NOPE), F32), s, s, jnp.zeros((n, HP - QK_DIM), F32)], axis=1)
    return cos, sin


def _split_mod(m):
    return [m[:, i * D_MODEL:(i + 1) * D_MODEL] for i in range(6)]


def _prompt_layer(x, mod, lw, cos, sin):
    b, t, d = x.shape
    shift_m, scale_m, gate_m, shift_f, scale_f, gate_f = [a.reshape(b, 1, d) for a in _split_mod(mod)]
    q, k, v, ckv, krp, tail, cv, bg0, part = _inproj_prompt(x, scale_m, shift_m, lw, cos, sin)
    oa = _flash_attention(q, k, v)
    x1, h2 = _merge(oa, bg0, part, x, gate_m, scale_f, shift_f, lw, TM_MERGE)
    _, idx_t, w_t = _router(h2.reshape(b * t, d), lw, TM_ROUTE)
    y = _moe_sparse(h2, idx_t, w_t, x1, gate_f, lw)
    state = (ckv, krp[:, :, QK_NOPE:QK_DIM], tail[:, 8 - (CONV_W - 1):], cv)
    return y, state


def _sample_layer(layer, x, mod, lw, cos, sin, hist, pool_c, pool_krt, page_table):
    n, d = x.shape
    shift_m, scale_m, gate_m, shift_f, scale_f, gate_f = _split_mod(mod)
    qg, ql, ckv, krp, z, cv, bg0, part = _inproj_sample(x, scale_m, shift_m, lw, cos, sin, hist[:, 0], hist[:, 1])
    head_pad = ((0, 0), (0, HEAD_ROWS - N_HEADS), (0, 0))
    qlat = jnp.pad(ql.transpose(1, 0, 2), head_pad).astype(BF16)
    qr = jnp.pad(qg[:, :, QK_NOPE:QK_DIM].transpose(1, 0, 2), head_pad).astype(BF16)
    kr = krp[:, QK_NOPE:QK_DIM]
    krnew_t = jnp.pad(kr[:, :, None], ((0, 0), (0, 0), (0, LANES - 1)))
    olat = _paged_attention(layer, page_table, lw["wkt"], qlat, qr, ckv.reshape(n, 1, KV_LORA), krnew_t,
                            pool_c, pool_krt)
    oa = _uv_project(olat.reshape(n, N_HEADS * KV_LORA), lw["wuv"])
    g3 = lambda a: a.reshape(1, n, -1)
    x1, h2 = _merge(g3(oa), g3(bg0), g3(part), g3(x), g3(gate_m), g3(scale_f), g3(shift_f), lw, n)
    gates, _, _ = _router(h2.reshape(n, d), lw, n)
    y = _moe(h2, g3(gates), x1, g3(gate_f), lw, n).reshape(n, d)
    state = (ckv.reshape(n, 1, KV_LORA), kr.reshape(n, 1, QK_ROPE),
             jnp.stack([hist[:, 1], z], axis=1), cv.reshape(n, 1, GM_DIM))
    return y, state


def kernel(x_prompt, x_sample, cache_kv_latent, cache_k_rope, state_conv, page_table, c_prompt, c_sample,
           w_ada, b_ada, g_norm_mix, g_norm_ffn, w_in, g_q_lat, w_uq, g_kv_lat, w_uk, w_uv, g_qk_q, g_qk_k,
           w_br_a, w_conv, w_br_b, g_v_ln, w_spatial, b_spatial, w_br_c, w_out, w_router, b_router,
           w_e_gate, w_e_up, w_e_down, w_sh_gate, w_sh_up, w_sh_down):
    params = dict(w_ada=w_ada, b_ada=b_ada, g_norm_mix=g_norm_mix, g_norm_ffn=g_norm_ffn, w_in=w_in,
                  g_q_lat=g_q_lat, w_uq=w_uq, g_kv_lat=g_kv_lat, w_uk=w_uk, w_uv=w_uv, g_qk_q=g_qk_q,
                  g_qk_k=g_qk_k, w_br_a=w_br_a, w_conv=w_conv, w_br_b=w_br_b, g_v_ln=g_v_ln,
                  w_spatial=w_spatial, b_spatial=b_spatial, w_br_c=w_br_c, w_out=w_out, w_router=w_router,
                  b_router=b_router, w_e_gate=w_e_gate, w_e_up=w_e_up, w_e_down=w_e_down,
                  w_sh_gate=w_sh_gate, w_sh_up=w_sh_up, w_sh_down=w_sh_down)
    depth = w_in.shape[0]
    nb, t, d = x_prompt.shape
    ns = x_sample.shape[0]
    assert x_sample.shape[1] == 1 and t % TQ == 0 and t % MOE_BLOCK == 0
    past_len = page_table.shape[1] * PAGE_SIZE

    cos_p, sin_p = _rope_tables(jnp.arange(t))
    cos_s, sin_s = _rope_tables(past_len + jnp.arange(1))
    c_all = jnp.concatenate([c_prompt, c_sample], axis=0)
    cache_krt = jnp.swapaxes(cache_k_rope, 2, 3)
    b_ada3 = b_ada.reshape(depth, 1, -1)

    hp, hs = x_prompt, x_sample.reshape(ns, d)
    st_p, st_s = [], []
    for l in range(depth):
        lw = _prep_layer(l, params)
        lw.update(layer=l, w_e_gate=w_e_gate, w_e_up=w_e_up, w_e_down=w_e_down)
        mod = _adaln(l, c_all, w_ada, b_ada3)
        hp, sp = _prompt_layer(hp, mod[:nb], lw, cos_p, sin_p)
        hs, ss = _sample_layer(l, hs, mod[nb:], lw, cos_s, sin_s, state_conv[l], cache_kv_latent, cache_krt,
                               page_table)
        st_p.append(sp)
        st_s.append(ss)

    stack = lambda sts, i: jnp.stack([s[i] for s in sts])
    return (hp, hs.reshape(ns, 1, d),
            stack(st_p, 0), stack(st_p, 1), stack(st_p, 2), stack(st_p, 3),
            stack(st_s, 0), stack(st_s, 1), stack(st_s, 2), stack(st_s, 3))
```
